```python
import jax, jax.numpy as jnp
from jax import lax
import numpy as np

D_MODEL = 2048
BATCH = 1
SEQ = 8192
DEPTH = 1
DEC_BATCH = 32
DEC_SEQ = 8
PAST_LEN = 16384
PAGE_SIZE = 128

A_HEAD_DIM = 64
A_WIDTH = D_MODEL // 2
A_HEADS = A_WIDTH // A_HEAD_DIM
DECAY_RANK = 64
ICLR_RANK = 64
A_SHIFT_WIDTH = 3 * A_WIDTH + DECAY_RANK + ICLR_RANK
GN_EPS = 64e-5
ATT_GROUPS = ((128, 1), (512, 4), (2048, 16))
N_GROUPS = 3
HEADS_PER_GROUP = 4
B_HEAD_DIM = 128
B_QKV_WIDTH = N_GROUPS * HEADS_PER_GROUP * B_HEAD_DIM
B_OUT_WIDTH = HEADS_PER_GROUP * B_HEAD_DIM
ROPE_THETA = 10000.0
NORM_EPS = 1e-6
PLE_DIM = 256
N_IN = A_SHIFT_WIDTH + A_WIDTH + 3 * B_QKV_WIDTH + B_OUT_WIDTH + 2 * D_MODEL

kernel_name = 'hybrid_rwkv7_dilated_attn_decode_step'


def split_cols(z, sizes):
    out, start = [], 0
    for n in sizes:
        out.append(z[..., start:start + n])
        start += n
    return out


def rms_norm(x, g):
    xf = x.astype(jnp.float32)
    y = xf * lax.rsqrt(jnp.mean(xf * xf, axis=-1, keepdims=True) + NORM_EPS)
    return (y * g.astype(jnp.float32)).astype(x.dtype)


def rope(x, pos):
    half = x.shape[-1] // 2
    inv = ROPE_THETA ** (-jnp.arange(half, dtype=jnp.float32) / half)
    ang = pos.astype(jnp.float32)[:, None] * inv[None, :]
    ang = ang.reshape((1, x.shape[1]) + (1,) * (x.ndim - 3) + (half,))
    cos, sin = jnp.cos(ang), jnp.sin(ang)
    xf = x.astype(jnp.float32)
    x1, x2 = xf[..., :half], xf[..., half:]
    return jnp.concatenate([x1 * cos - x2 * sin, x2 * cos + x1 * sin], axis=-1).astype(x.dtype)


def token_shift(z, prev, mu):
    zs = jnp.concatenate([prev[:, None].astype(z.dtype), z[:, :-1]], axis=1)
    return z + (zs - z) * mu


def wkv7_scan(r, decay, k, v, kk, a, s0):
    def step(s, inp):
        r_t, w_t, k_t, v_t, kk_t, a_t = inp
        sa = jnp.einsum('bhvk,bhk->bhv', s, -kk_t)
        s = (s * w_t[:, :, None, :] + sa[..., None] * (kk_t * a_t)[:, :, None, :]
             + v_t[..., None] * k_t[:, :, None, :])
        return s, jnp.einsum('bhvk,bhk->bhv', s, r_t)
    xs = tuple(jnp.moveaxis(t, 1, 0) for t in (r, decay, k, v, kk, a))
    s, ys = lax.scan(step, s0, xs)
    return jnp.moveaxis(ys, 0, 1), s


def rwkv7_time_mix(zs, s0, w0, w2, a0, a2, k_k, k_a, r_k, gn_w, gn_b):
    bn, t, _ = zs.shape
    f32 = jnp.float32
    r, k, v, w_lo, a_lo = [c.astype(f32) for c in
                           split_cols(zs, (A_WIDTH, A_WIDTH, A_WIDTH, DECAY_RANK, ICLR_RANK))]
    log_w = -jax.nn.softplus(-(w0 + jnp.tanh(w_lo) @ w2)) - 0.5
    decay = jnp.exp(-jnp.exp(log_w))
    a = jax.nn.sigmoid(a0 + a_lo @ a2)
    hs = (bn, t, A_HEADS, A_HEAD_DIM)
    kk = (k * k_k).reshape(hs)
    kk = kk / jnp.maximum(jnp.sqrt(jnp.sum(kk * kk, axis=-1, keepdims=True)), 1e-12)
    k = k * (1.0 + (a - 1.0) * k_a)
    r, k, v, decay, a = (c.reshape(hs) for c in (r, k, v, decay, a))
    y, s = wkv7_scan(r, decay, k, v, kk, a, s0.astype(f32))
    mean = jnp.mean(y, axis=-1, keepdims=True)
    var = jnp.mean(jnp.square(y - mean), axis=-1, keepdims=True)
    y = ((y - mean) * lax.rsqrt(var + GN_EPS)).reshape(bn, t, A_WIDTH) * gn_w + gn_b
    bonus = jnp.sum(r * k * r_k, axis=-1, keepdims=True) * v
    y = y + bonus.reshape(bn, t, A_WIDTH)
    return y.astype(zs.dtype), s


def band_attention(q, k, v, n_back):
    n, length, h, dh = q.shape
    blk = n_back
    nb = length // blk
    f32 = jnp.float32
    qb = q.astype(f32).reshape(n, nb, blk, h, dh)
    pad = jnp.zeros((n, blk, h, dh), f32)
    kp = jnp.concatenate([pad, k.astype(f32)], axis=1).reshape(n, nb + 1, blk, h, dh)
    vp = jnp.concatenate([pad, v.astype(f32)], axis=1).reshape(n, nb + 1, blk, h, dh)
    kb = jnp.concatenate([kp[:, :-1], kp[:, 1:]], axis=2)
    vb = jnp.concatenate([vp[:, :-1], vp[:, 1:]], axis=2)
    s = jnp.einsum('nbqhd,nbkhd->nbhqk', qb, kb) * (dh ** -0.5)
    qi = jnp.arange(blk)[:, None] + blk
    kj = jnp.arange(2 * blk)[None, :]
    first = (jnp.arange(nb) * blk - blk)[:, None, None]
    valid = (kj <= qi) & (kj >= qi - n_back) & (first + kj >= 0)
    s = jnp.where(valid[None, :, None], s, -jnp.inf)
    m = jnp.max(s, axis=-1, keepdims=True)
    pr = jnp.exp(s - m)
    den = jnp.sum(pr, axis=-1, keepdims=True)
    o = jnp.einsum('nbhqk,nbkhd->nbqhd', pr / den, vb)
    lse = jnp.moveaxis((m + jnp.log(den))[..., 0], 2, 3)
    return o.reshape(n, length, h, dh), lse.reshape(n, length, h)


def dilated_attention_prompt(q, k, v, window, dilation):
    b, seq, h, dh = q.shape
    n_back = window // dilation
    span = dilation * n_back
    seq_p = -(-seq // span) * span
    length = seq_p // dilation

    def to_strided(x):
        x = jnp.pad(x, ((0, 0), (0, seq_p - seq), (0, 0), (0, 0)))
        return x.reshape(b, length, dilation, h, dh).transpose(0, 2, 1, 3, 4).reshape(b * dilation, length, h, dh)

    o, lse = band_attention(to_strided(q), to_strided(k), to_strided(v), n_back)
    o = o.reshape(b, dilation, length, h, dh).transpose(0, 2, 1, 3, 4).reshape(b, seq_p, h, dh)[:, :seq]
    lse = lse.reshape(b, dilation, length, h).transpose(0, 2, 1, 3).reshape(b, seq_p, h)[:, :seq]
    return o, lse


def dilated_attention_sample(q, k_all, v_all, window, dilation):
    t_new = q.shape[1]
    past = k_all.shape[1] - t_new
    n_back = window // dilation
    idx = past + jnp.arange(t_new)[:, None] - dilation * jnp.arange(n_back + 1)[None, :]
    valid = idx >= 0
    idx = jnp.maximum(idx, 0)
    f32 = jnp.float32
    kg = k_all.astype(f32)[:, idx]
    vg = v_all.astype(f32)[:, idx]
    s = jnp.einsum('bthd,btmhd->bthm', q.astype(f32), kg) * (q.shape[-1] ** -0.5)
    s = jnp.where(valid[None, :, None, :], s, -jnp.inf)
    m = jnp.max(s, axis=-1, keepdims=True)
    pr = jnp.exp(s - m)
    den = jnp.sum(pr, axis=-1, keepdims=True)
    o = jnp.einsum('bthm,btmhd->bthd', pr / den, vg)
    return o, (m + jnp.log(den))[..., 0]


def mixer_layer(x, p, pos, shift_prev, wkv0, kv_caches,
                ln_g, w_in, mu, w0, w2, a0, a2, k_k, k_a, r_k, gn_w, gn_b,
                q_gain, k_gain, w_up_a, w_up_b, w_out, w_ple_gate, w_ple_proj):
    bn, t, _ = x.shape
    hn = rms_norm(x, ln_g)
    z = hn @ w_in
    z_shift, g_a, q, k, v, g_b, m_a, m_b = split_cols(
        z, (A_SHIFT_WIDTH, A_WIDTH, B_QKV_WIDTH, B_QKV_WIDTH, B_QKV_WIDTH, B_OUT_WIDTH, D_MODEL, D_MODEL))
    new_shift = z_shift[:, -1]
    y_a, new_wkv = rwkv7_time_mix(token_shift(z_shift, shift_prev, mu), wkv0,
                                  w0, w2, a0, a2, k_k, k_a, r_k, gn_w, gn_b)
    y_a = (y_a * jax.nn.silu(g_a)) @ w_up_a
    shp = (bn, t, N_GROUPS, HEADS_PER_GROUP, B_HEAD_DIM)
    q = rope(rms_norm(q.reshape(shp), q_gain), pos)
    k = rope(rms_norm(k.reshape(shp), k_gain), pos)
    v = v.reshape(shp)
    outs, lses, new_kv = [], [], []
    for gi, (window, dilation) in enumerate(ATT_GROUPS):
        qg, kg, vg = q[:, :, gi], k[:, :, gi], v[:, :, gi]
        kv_new = jnp.stack([kg, vg], axis=2)
        if kv_caches is None:
            o, lse = dilated_attention_prompt(qg, kg, vg, window, dilation)
            new_kv.append(kv_new[:, -min(window, t):])
        else:
            cache = kv_caches[gi]
            kv_all = jnp.concatenate([cache.astype(kv_new.dtype), kv_new], axis=1)
            o, lse = dilated_attention_sample(qg, kv_all[:, :, 0], kv_all[:, :, 1], window, dilation)
            new_kv.append(kv_all[:, -cache.shape[1]:])
        outs.append(o)
        lses.append(lse)
    alpha = jax.nn.softmax(jnp.stack(lses, axis=0), axis=0)
    o = jnp.einsum('gbth,gbthd->bthd', alpha, jnp.stack(outs, axis=0)).astype(x.dtype)
    y_b = (o.reshape(bn, t, B_OUT_WIDTH) * jax.nn.silu(g_b)) @ w_up_b
    merged = jax.nn.sigmoid(m_a) * y_a + jax.nn.sigmoid(m_b) * y_b
    h = x + merged @ w_out
    y = h + jax.nn.sigmoid(h @ w_ple_gate) * (p @ w_ple_proj)
    return y, new_shift, new_wkv, new_kv


def setup_inputs(seed: int = 0) -> dict:
    key = jax.random.key(seed)
    ks = iter(jax.random.split(key, 40))
    f32 = jnp.float32

    def nrm(shape, scale):
        return scale * jax.random.normal(next(ks), shape, f32)

    def unif(shape, lo, hi):
        return jax.random.uniform(next(ks), shape, f32, lo, hi)

    L = DEPTH
    kv_shape = lambda w: (L, DEC_BATCH, min(w, PAST_LEN), 2, HEADS_PER_GROUP, B_HEAD_DIM)
    return {
        'x_prompt': nrm((BATCH, SEQ, D_MODEL), 1.0),
        'x_sample': nrm((DEC_BATCH, DEC_SEQ, D_MODEL), 1.0),
        'state_shift': nrm((L, DEC_BATCH, A_SHIFT_WIDTH), 1.0),
        'state_wkv': nrm((L, DEC_BATCH, A_HEADS, A_HEAD_DIM, A_HEAD_DIM), 0.5),
        'cache_kv_w128': nrm(kv_shape(ATT_GROUPS[0][0]), 1.0),
        'cache_kv_w512': nrm(kv_shape(ATT_GROUPS[1][0]), 1.0),
        'cache_kv_w2048': nrm(kv_shape(ATT_GROUPS[2][0]), 1.0),
        'p_prompt': nrm((L, BATCH, SEQ, PLE_DIM), 1.0),
        'p_sample': nrm((L, DEC_BATCH, DEC_SEQ, PLE_DIM), 1.0),
        'ln_g': 1.0 + nrm((L, D_MODEL), 0.02),
        'w_in': nrm((L, D_MODEL, N_IN), D_MODEL ** -0.5),
        'mu': unif((L, A_SHIFT_WIDTH), 0.0, 1.0),
        'w0': unif((L, A_WIDTH), -6.0, -1.0),
        'w2': nrm((L, DECAY_RANK, A_WIDTH), 0.1),
        'a0': nrm((L, A_WIDTH), 0.1),
        'a2': nrm((L, ICLR_RANK, A_WIDTH), 0.3 * ICLR_RANK ** -0.5),
        'k_k': 0.85 + nrm((L, A_WIDTH), 0.02),
        'k_a': 1.0 + nrm((L, A_WIDTH), 0.02),
        'r_k': nrm((L, A_HEADS, A_HEAD_DIM), 0.1),
        'gn_w': 1.0 + nrm((L, A_WIDTH), 0.02),
        'gn_b': nrm((L, A_WIDTH), 0.02),
        'q_gain': 1.0 + nrm((L, B_HEAD_DIM), 0.02),
        'k_gain': 1.0 + nrm((L, B_HEAD_DIM), 0.02),
        'w_up_a': nrm((L, A_WIDTH, D_MODEL), A_WIDTH ** -0.5),
        'w_up_b': nrm((L, B_OUT_WIDTH, D_MODEL), B_OUT_WIDTH ** -0.5),
        'w_out': nrm((L, D_MODEL, D_MODEL), D_MODEL ** -0.5),
        'w_ple_gate': nrm((L, D_MODEL, D_MODEL), D_MODEL ** -0.5),
        'w_ple_proj': nrm((L, PLE_DIM, D_MODEL), PLE_DIM ** -0.5),
    }


def reference(x_prompt, x_sample, state_shift, state_wkv, cache_kv_w128, cache_kv_w512, cache_kv_w2048,
              p_prompt, p_sample, ln_g, w_in, mu, w0, w2, a0, a2, k_k, k_a, r_k, gn_w, gn_b,
              q_gain, k_gain, w_up_a, w_up_b, w_out, w_ple_gate, w_ple_proj):
    bp, tp, _ = x_prompt.shape
    pos_prompt = jnp.arange(tp, dtype=jnp.int32)
    pos_sample = PAST_LEN + jnp.arange(x_sample.shape[1], dtype=jnp.int32)
    hp, hs = x_prompt, x_sample
    sp_l, wp_l, k1p, k2p, k3p = [], [], [], [], []
    ss_l, ws_l, k1s, k2s, k3s = [], [], [], [], []
    for i in range(DEPTH):
        lw = (ln_g[i], w_in[i], mu[i], w0[i], w2[i], a0[i], a2[i], k_k[i], k_a[i], r_k[i], gn_w[i], gn_b[i],
              q_gain[i], k_gain[i], w_up_a[i], w_up_b[i], w_out[i], w_ple_gate[i], w_ple_proj[i])
        hp, sh_p, wkv_p, kv_p = mixer_layer(
            hp, p_prompt[i], pos_prompt, jnp.zeros((bp, A_SHIFT_WIDTH), hp.dtype),
            jnp.zeros((bp, A_HEADS, A_HEAD_DIM, A_HEAD_DIM), jnp.float32), None, *lw)
        hs, sh_s, wkv_s, kv_s = mixer_layer(
            hs, p_sample[i], pos_sample, state_shift[i], state_wkv[i],
            (cache_kv_w128[i], cache_kv_w512[i], cache_kv_w2048[i]), *lw)
        sp_l.append(sh_p); wp_l.append(wkv_p); k1p.append(kv_p[0]); k2p.append(kv_p[1]); k3p.append(kv_p[2])
        ss_l.append(sh_s); ws_l.append(wkv_s); k1s.append(kv_s[0]); k2s.append(kv_s[1]); k3s.append(kv_s[2])
    return (hp, hs,
            jnp.stack(sp_l), jnp.stack(wp_l), jnp.stack(k1p), jnp.stack(k2p), jnp.stack(k3p),
            jnp.stack(ss_l), jnp.stack(ws_l), jnp.stack(k1s), jnp.stack(k2s), jnp.stack(k3s))
```

```python
import functools

import jax
import jax.numpy as jnp
from jax import lax
from jax.experimental import pallas as pl
from jax.experimental.pallas import tpu as pltpu

F32 = jnp.float32
BF16 = jnp.bfloat16

LANES = 128
SUBLANES = 8
D_MODEL = 2048
A_HEAD_DIM = 64
A_WIDTH = 1024
A_HEADS = 16
A_PAIRS = A_WIDTH // LANES
LORA_RANK = 64
A_SHIFT_WIDTH = 3 * A_WIDTH + 2 * LORA_RANK
GN_EPS = 64e-5
NORM_EPS = 1e-6
ATT_GROUPS = ((128, 1), (512, 4), (2048, 16))
N_BACK = 128
HEADS_PER_GROUP = 4
B_HEAD_DIM = 128
B_GROUP_WIDTH = HEADS_PER_GROUP * B_HEAD_DIM
B_QKV_WIDTH = 3 * B_GROUP_WIDTH
ROPE_THETA = 10000.0
PAST_LEN = 16384
ZA_WIDTH = A_SHIFT_WIDTH + A_WIDTH
ZB_WIDTH = 3 * B_QKV_WIDTH
ZC_WIDTH = 2 * D_MODEL + B_GROUP_WIDTH
VMEM_LIMIT = 48 * 1024 * 1024


def _cparams(sem):
    return pltpu.CompilerParams(dimension_semantics=sem, vmem_limit_bytes=VMEM_LIMIT)


def _sigmoid(x):
    return 1.0 / (1.0 + jnp.exp(-x))


def _rmsnorm_kernel(xp_ref, xs_ref, g_ref, o_ref, *, n_prompt_tiles):
    def emit(x):
        y = x * lax.rsqrt(jnp.mean(x * x, axis=-1, keepdims=True) + NORM_EPS)
        o_ref[...] = (y * g_ref[...]).astype(BF16)

    i = pl.program_id(0)

    @pl.when(i < n_prompt_tiles)
    def _():
        emit(xp_ref[...])

    @pl.when(i >= n_prompt_tiles)
    def _():
        emit(xs_ref[...])


def _rmsnorm(xp, xs, g):
    rp, d = xp.shape
    rs = xs.shape[0]
    tm = rs
    n_p = rp // tm
    return pl.pallas_call(
        functools.partial(_rmsnorm_kernel, n_prompt_tiles=n_p),
        grid=(n_p + 1,),
        in_specs=[
            pl.BlockSpec((tm, d), lambda i: (jnp.minimum(i, n_p - 1), 0)),
            pl.BlockSpec((tm, d), lambda i: (0, 0)),
            pl.BlockSpec((1, d), lambda i: (0, 0)),
        ],
        out_specs=pl.BlockSpec((tm, d), lambda i: (i, 0)),
        out_shape=jax.ShapeDtypeStruct((rp + rs, d), BF16),
        compiler_params=_cparams(("arbitrary",)),
    )(xp, xs, g)


def _mm_kernel(a_ref, w_ref, o_ref):
    o_ref[...] = jnp.dot(a_ref[...], w_ref[...], preferred_element_type=F32)


def _matmul(a, w, tm, tn):
    m, k = a.shape
    n = w.shape[1]
    return pl.pallas_call(
        _mm_kernel,
        grid=(n // tn, m // tm),
        in_specs=[
            pl.BlockSpec((tm, k), lambda j, i: (i, 0)),
            pl.BlockSpec((k, tn), lambda j, i: (0, j)),
        ],
        out_specs=pl.BlockSpec((tm, tn), lambda j, i: (i, j)),
        out_shape=jax.ShapeDtypeStruct((m, n), F32),
        compiler_params=_cparams(("parallel", "arbitrary")),
    )(a, w)


def _split_dot(x, ones_bf16):
    hi = x.astype(BF16)
    r1 = x - hi.astype(F32)
    mid = r1.astype(BF16)
    lo = (r1 - mid.astype(F32)).astype(BF16)
    out = jnp.dot(hi, ones_bf16, preferred_element_type=F32)
    out += jnp.dot(mid, ones_bf16, preferred_element_type=F32)
    out += jnp.dot(lo, ones_bf16, preferred_element_type=F32)
    return out


def _rwkv_kernel(z_ref, shift0_ref, s0_ref, mu_ref, w0_ref, w2_ref, a0_ref, a2_ref, kk_ref, ka_ref,
                 rk_ref, gnw_ref, gnb_ref,
                 ya_ref, sT_ref,
                 carry_ref, s_ref, nkk_ref, w_ref, b_ref, k_ref, r_ref, vh_ref, vl_ref, y_ref, bonus_ref,
                 *, tc):
    c = pl.program_id(1)
    n_c = pl.num_programs(1)

    @pl.when(c == 0)
    def _():
        carry_ref[...] = shift0_ref[0]
        s_ref[...] = s0_ref[0]

    lane = lax.broadcasted_iota(jnp.int32, (LANES, LANES), 1)
    row = lax.broadcasted_iota(jnp.int32, (LANES, LANES), 0)
    head_ones = (lane // A_HEAD_DIM == row // A_HEAD_DIM).astype(BF16)
    lane_s = lax.broadcasted_iota(jnp.int32, (A_HEAD_DIM, LANES), 1)
    row_s = lax.broadcasted_iota(jnp.int32, (A_HEAD_DIM, LANES), 0)
    diag = (lane_s % A_HEAD_DIM) == row_s
    head_ones2 = jnp.concatenate([head_ones, head_ones], axis=0)

    z = z_ref[:, :A_SHIFT_WIDTH]
    prev = pltpu.roll(z, 1, axis=0)
    first = lax.broadcasted_iota(jnp.int32, (tc, 1), 0) == 0
    prev = jnp.where(first, carry_ref[...], prev)
    carry_ref[...] = z[tc - 1:tc, :]
    zs = z + (prev - z) * mu_ref[...]

    lora = zs[:, 3 * A_WIDTH:]
    w_lo = jnp.tanh(lora[:, :LORA_RANK]).astype(BF16)
    a_lo = lora[:, LORA_RANK:].astype(BF16)
    lw = w0_ref[...] + jnp.dot(w_lo, w2_ref[...].astype(BF16), preferred_element_type=F32)
    nlw = -lw
    log_w = -(jnp.maximum(nlw, 0.0) + jnp.log1p(jnp.exp(-jnp.abs(nlw)))) - 0.5
    decay = jnp.exp(-jnp.exp(log_w))
    a = _sigmoid(a0_ref[...] + jnp.dot(a_lo, a2_ref[...].astype(BF16), preferred_element_type=F32))

    for j in range(A_PAIRS):
        sl = slice(j * LANES, (j + 1) * LANES)
        r_j = zs[:, j * LANES:(j + 1) * LANES]
        k_j = zs[:, A_WIDTH + j * LANES:A_WIDTH + (j + 1) * LANES]
        v_j = zs[:, 2 * A_WIDTH + j * LANES:2 * A_WIDTH + (j + 1) * LANES]
        a_j = a[:, sl]
        kk = k_j * kk_ref[:, sl]
        n2 = _split_dot(kk * kk, head_ones)
        kk = kk / jnp.maximum(jnp.sqrt(n2), 1e-12)
        k_mod = k_j * (1.0 + (a_j - 1.0) * ka_ref[:, sl])
        nkk_ref[:, sl] = -kk
        w_ref[:, sl] = decay[:, sl]
        b_ref[:, sl] = kk * a_j
        k_ref[:, sl] = k_mod
        r_ref[:, sl] = r_j
        vh = v_j.astype(BF16).astype(F32)
        vh_ref[:, sl] = vh
        vl_ref[:, sl] = (v_j - vh).astype(BF16).astype(F32)
        bonus_ref[:, sl] = _split_dot(r_j * k_mod * rk_ref[:, sl], head_ones) * v_j

    row8 = lax.broadcasted_iota(jnp.int32, (SUBLANES, LANES), 0)

    def group(g, carry):
        base = pl.multiple_of(g * SUBLANES, SUBLANES)
        rows = pl.ds(base, SUBLANES)
        for j in range(A_PAIRS):
            sl = slice(j * LANES, (j + 1) * LANES)
            nkk8, w8, b8, k8 = nkk_ref[rows, sl], w_ref[rows, sl], b_ref[rows, sl], k_ref[rows, sl]
            r8, vh8, vl8 = r_ref[rows, sl], vh_ref[rows, sl], vl_ref[rows, sl]
            s = s_ref[j]
            ytile = jnp.zeros((SUBLANES, LANES), F32)
            for u in range(SUBLANES):
                one = slice(u, u + 1)
                sa = jnp.dot((s * nkk8[one]).astype(BF16), head_ones, preferred_element_type=F32)
                vsel = jnp.concatenate(
                    [jnp.where(diag, vh8[one], 0.0), jnp.where(diag, vl8[one], 0.0)], axis=1).astype(BF16)
                vcol = jnp.dot(vsel, head_ones2, preferred_element_type=F32)
                s = s * w8[one] + sa * b8[one] + vcol * k8[one]
                ycol = jnp.dot((s * r8[one]).astype(BF16), head_ones, preferred_element_type=F32)
                yrow = jnp.sum(jnp.where(diag, ycol, 0.0), axis=0, keepdims=True)
                ytile = jnp.where(row8 == u, yrow, ytile)
            s_ref[j] = s
            y_ref[rows, sl] = ytile
        return carry

    lax.fori_loop(0, tc // SUBLANES, group, 0)

    for j in range(A_PAIRS):
        sl = slice(j * LANES, (j + 1) * LANES)
        y = y_ref[:, sl]
        mean = _split_dot(y, head_ones) * (1.0 / A_HEAD_DIM)
        yc = y - mean
        var = _split_dot(yc * yc, head_ones) * (1.0 / A_HEAD_DIM)
        yn = yc * lax.rsqrt(var + GN_EPS) * gnw_ref[:, sl] + gnb_ref[:, sl]
        yn = yn + bonus_ref[:, sl]
        g = z_ref[:, A_SHIFT_WIDTH + j * LANES:A_SHIFT_WIDTH + (j + 1) * LANES]
        ya_ref[:, sl] = (yn * (g * _sigmoid(g))).astype(BF16)

    @pl.when(c == n_c - 1)
    def _():
        sT_ref[0] = s_ref[...]


def _rwkv(za, row_block0, n_seq, t_len, tc, shift0, s0, mu, w0, w2, a0, a2, k_k, k_a, r_k, gn_w, gn_b):
    n_c = t_len // tc
    vec = lambda n: pl.BlockSpec((1, n), lambda b, c: (0, 0))
    row_spec = pl.BlockSpec((tc, ZA_WIDTH), lambda b, c: (row_block0 + b * n_c + c, 0))
    scratch_rows = [pltpu.VMEM((tc, A_WIDTH), F32) for _ in range(9)]
    return pl.pallas_call(
        functools.partial(_rwkv_kernel, tc=tc),
        grid=(n_seq, n_c),
        in_specs=[
            row_spec,
            pl.BlockSpec((1, 1, A_SHIFT_WIDTH), lambda b, c: (b, 0, 0)),
            pl.BlockSpec((1, A_PAIRS, A_HEAD_DIM, LANES), lambda b, c: (b, 0, 0, 0)),
            vec(A_SHIFT_WIDTH), vec(A_WIDTH),
            pl.BlockSpec((LORA_RANK, A_WIDTH), lambda b, c: (0, 0)),
            vec(A_WIDTH),
            pl.BlockSpec((LORA_RANK, A_WIDTH), lambda b, c: (0, 0)),
            vec(A_WIDTH), vec(A_WIDTH), vec(A_WIDTH), vec(A_WIDTH), vec(A_WIDTH),
        ],
        out_specs=[
            pl.BlockSpec((tc, A_WIDTH), lambda b, c: (b * n_c + c, 0)),
            pl.BlockSpec((1, A_PAIRS, A_HEAD_DIM, LANES), lambda b, c: (b, 0, 0, 0)),
        ],
        out_shape=[
            jax.ShapeDtypeStruct((n_seq * t_len, A_WIDTH), BF16),
            jax.ShapeDtypeStruct((n_seq, A_PAIRS, A_HEAD_DIM, LANES), F32),
        ],
        scratch_shapes=[pltpu.VMEM((1, A_SHIFT_WIDTH), F32),
                        pltpu.VMEM((A_PAIRS, A_HEAD_DIM, LANES), F32)] + scratch_rows,
        compiler_params=_cparams(("arbitrary", "arbitrary")),
    )(za, shift0, s0, mu, w0, w2, a0, a2, k_k, k_a, r_k, gn_w, gn_b)


def _state_to_pairs(s):
    b = s.shape[0]
    return s.reshape(b, A_PAIRS, 2, A_HEAD_DIM, A_HEAD_DIM).transpose(0, 1, 3, 2, 4).reshape(
        b, A_PAIRS, A_HEAD_DIM, LANES)


def _pairs_to_state(s):
    b = s.shape[0]
    return s.reshape(b, A_PAIRS, A_HEAD_DIM, 2, A_HEAD_DIM).transpose(0, 1, 3, 2, 4).reshape(
        b, A_HEADS, A_HEAD_DIM, A_HEAD_DIM)


def _norm_rope(x, gain, cos, sin):
    y = x * lax.rsqrt(jnp.mean(x * x, axis=-1, keepdims=True) + NORM_EPS) * gain
    return y * cos + pltpu.roll(y, B_HEAD_DIM // 2, axis=1) * sin


def _dot_nt(a, b):
    return lax.dot_general(a, b, (((1,), (1,)), ((), ())), preferred_element_type=F32)


def _attn_prompt_kernel(q_ref, k_ref, v_ref, cos_ref, sin_ref, qg_ref, kg_ref,
                        o_ref, lse_ref, kr_ref, kprev_ref, vprev_ref):
    i = pl.program_id(1)
    blk = N_BACK

    @pl.when(i == 0)
    def _():
        kprev_ref[...] = jnp.zeros_like(kprev_ref)
        vprev_ref[...] = jnp.zeros_like(vprev_ref)

    rowi = lax.broadcasted_iota(jnp.int32, (blk, blk), 0)
    coli = lax.broadcasted_iota(jnp.int32, (blk, blk), 1)
    cur_ok = coli <= rowi
    prev_ok = jnp.logical_and(coli >= rowi, i > 0)
    cos = cos_ref[...]
    sin = sin_ref[...]
    scale = B_HEAD_DIM ** -0.5
    for h in range(HEADS_PER_GROUP):
        sl = slice(h * B_HEAD_DIM, (h + 1) * B_HEAD_DIM)
        qn = _norm_rope(q_ref[:, sl], qg_ref[...], cos, sin).astype(BF16)
        kr = _norm_rope(k_ref[:, sl], kg_ref[...], cos, sin)
        kr_ref[:, sl] = kr
        kb = kr.astype(BF16)
        vb = v_ref[:, sl].astype(BF16)
        s_cur = jnp.where(cur_ok, _dot_nt(qn, kb) * scale, -jnp.inf)
        s_prev = jnp.where(prev_ok, _dot_nt(qn, kprev_ref[:, sl]) * scale, -jnp.inf)
        m = jnp.maximum(jnp.max(s_cur, axis=-1, keepdims=True), jnp.max(s_prev, axis=-1, keepdims=True))
        p_cur = jnp.exp(s_cur - m)
        p_prev = jnp.exp(s_prev - m)
        den = jnp.sum(p_cur, axis=-1, keepdims=True) + jnp.sum(p_prev, axis=-1, keepdims=True)
        o = jnp.dot((p_cur / den).astype(BF16), vb, preferred_element_type=F32)
        o += jnp.dot((p_prev / den).astype(BF16), vprev_ref[:, sl], preferred_element_type=F32)
        o_ref[:, sl] = o
        lse_ref[:, sl] = jnp.broadcast_to(m + jnp.log(den), (blk, B_HEAD_DIM))
        kprev_ref[:, sl] = kb
        vprev_ref[:, sl] = vb


def _attn_prompt(zb, gi, dil, seq, cos, sin, q_gain, k_gain):
    rows = zb.shape[0] // dil
    zv = zb.reshape(rows, dil * ZB_WIDTH)
    n_blk = seq // dil // N_BACK
    per_rho = ZB_WIDTH // B_GROUP_WIDTH
    cs = cos.reshape(seq // dil, dil * B_HEAD_DIM)
    sn = sin.reshape(seq // dil, dil * B_HEAD_DIM)
    blk = lambda off: pl.BlockSpec((N_BACK, B_GROUP_WIDTH), lambda r, i: (i, r * per_rho + off + gi))
    tab = pl.BlockSpec((N_BACK, B_HEAD_DIM), lambda r, i: (i, r))
    gain = pl.BlockSpec((1, B_HEAD_DIM), lambda r, i: (0, 0))
    out = pl.BlockSpec((N_BACK, B_GROUP_WIDTH), lambda r, i: (i, r))
    shp = jax.ShapeDtypeStruct((seq // dil, dil * B_GROUP_WIDTH), F32)
    o, lse, kr = pl.pallas_call(
        _attn_prompt_kernel,
        grid=(dil, n_blk),
        in_specs=[blk(0), blk(3), blk(6), tab, tab, gain, gain],
        out_specs=[out, out, out],
        out_shape=[shp, shp, shp],
        scratch_shapes=[pltpu.VMEM((N_BACK, B_GROUP_WIDTH), BF16), pltpu.VMEM((N_BACK, B_GROUP_WIDTH), BF16)],
        compiler_params=_cparams(("arbitrary", "arbitrary")),
    )(zv, zv, zv, cs, sn, q_gain, k_gain)
    return (o.reshape(seq, B_GROUP_WIDTH), lse.reshape(seq, B_GROUP_WIDTH), kr.reshape(seq, B_GROUP_WIDTH))


def _attn_sample_kernel(q_ref, k_ref, v_ref, cache_ref, cos_ref, sin_ref, qg_ref, kg_ref,
                        o_ref, lse_ref, kr_ref, *, window, dil, t_new):
    cos = cos_ref[...]
    sin = sin_ref[...]
    scale = B_HEAD_DIM ** -0.5
    tq = lax.broadcasted_iota(jnp.int32, (t_new, window), 0)
    cc = lax.broadcasted_iota(jnp.int32, (t_new, window), 1)
    cache_ok = jnp.logical_and(((cc - tq + dil * N_BACK) & (dil - 1)) == 0, cc >= tq)
    tn = lax.broadcasted_iota(jnp.int32, (t_new, t_new), 0)
    un = lax.broadcasted_iota(jnp.int32, (t_new, t_new), 1)
    new_ok = jnp.logical_and(((tn - un + dil * N_BACK) & (dil - 1)) == 0, un <= tn)
    for h in range(HEADS_PER_GROUP):
        sl = slice(h * B_HEAD_DIM, (h + 1) * B_HEAD_DIM)
        qn = _norm_rope(q_ref[:, sl], qg_ref[...], cos, sin).astype(BF16)
        kr = _norm_rope(k_ref[:, sl], kg_ref[...], cos, sin)
        kr_ref[:, sl] = kr
        kc = cache_ref[0, :, sl].astype(BF16)
        vc = cache_ref[0, :, B_GROUP_WIDTH + h * B_HEAD_DIM:B_GROUP_WIDTH + (h + 1) * B_HEAD_DIM].astype(BF16)
        s_c = jnp.where(cache_ok, _dot_nt(qn, kc) * scale, -jnp.inf)
        s_n = jnp.where(new_ok, _dot_nt(qn, kr.astype(BF16)) * scale, -jnp.inf)
        m = jnp.maximum(jnp.max(s_c, axis=-1, keepdims=True), jnp.max(s_n, axis=-1, keepdims=True))
        p_c = jnp.exp(s_c - m)
        p_n = jnp.exp(s_n - m)
        den = jnp.sum(p_c, axis=-1, keepdims=True) + jnp.sum(p_n, axis=-1, keepdims=True)
        o = jnp.dot((p_c / den).astype(BF16), vc, preferred_element_type=F32)
        o += jnp.dot((p_n / den).astype(BF16), v_ref[:, sl].astype(BF16), preferred_element_type=F32)
        o_ref[:, sl] = o
        lse_ref[:, sl] = jnp.broadcast_to(m + jnp.log(den), (t_new, B_HEAD_DIM))


def _attn_sample(zb, gi, window, dil, row_block0, n_seq, t_new, cache, cos, sin, q_gain, k_gain):
    per_rho = ZB_WIDTH // B_GROUP_WIDTH
    del per_rho
    cache2 = cache.reshape(n_seq, window, 2 * B_GROUP_WIDTH)
    blk = lambda off: pl.BlockSpec((t_new, B_GROUP_WIDTH), lambda b: (row_block0 + b, off + gi))
    tab = pl.BlockSpec((t_new, B_HEAD_DIM), lambda b: (0, 0))
    gain = pl.BlockSpec((1, B_HEAD_DIM), lambda b: (0, 0))
    out = pl.BlockSpec((t_new, B_GROUP_WIDTH), lambda b: (b, 0))
    shp = jax.ShapeDtypeStruct((n_seq * t_new, B_GROUP_WIDTH), F32)
    return pl.pallas_call(
        functools.partial(_attn_sample_kernel, window=window, dil=dil, t_new=t_new),
        grid=(n_seq,),
        in_specs=[blk(0), blk(3), blk(6),
                  pl.BlockSpec((1, window, 2 * B_GROUP_WIDTH), lambda b: (b, 0, 0)),
                  tab, tab, gain, gain],
        out_specs=[out, out, out],
        out_shape=[shp, shp, shp],
        compiler_params=_cparams(("arbitrary",)),
    )(zb, zb, zb, cache2, cos, sin, q_gain, k_gain)


def _combine_kernel(o0, o1, o2, l0, l1, l2, g_ref, y_ref):
    a, b, c = l0[...], l1[...], l2[...]
    m = jnp.maximum(jnp.maximum(a, b), c)
    ea, eb, ec = jnp.exp(a - m), jnp.exp(b - m), jnp.exp(c - m)
    o = (ea * o0[...] + eb * o1[...] + ec * o2[...]) / (ea + eb + ec)
    g = g_ref[...]
    y_ref[...] = (o * (g * _sigmoid(g))).astype(BF16)


def _combine(outs, lses, zc, row_block0, tm):
    rows = outs[0].shape[0]
    spec = pl.BlockSpec((tm, B_GROUP_WIDTH), lambda i: (i, 0))
    gate_block = 2 * D_MODEL // B_GROUP_WIDTH
    return pl.pallas_call(
        _combine_kernel,
        grid=(rows // tm,),
        in_specs=[spec] * 6 + [pl.BlockSpec((tm, B_GROUP_WIDTH), lambda i: (row_block0 + i, gate_block))],
        out_specs=spec,
        out_shape=jax.ShapeDtypeStruct((rows, B_GROUP_WIDTH), BF16),
        compiler_params=_cparams(("parallel",)),
    )(*outs, *lses, zc)


def _merge_kernel(ya_ref, yb_ref, ma_ref, mb_ref, wa_ref, wb_ref, o_ref):
    ua = jnp.dot(ya_ref[...], wa_ref[...], preferred_element_type=F32)
    ub = jnp.dot(yb_ref[...], wb_ref[...], preferred_element_type=F32)
    o_ref[...] = (_sigmoid(ma_ref[...]) * ua + _sigmoid(mb_ref[...]) * ub).astype(BF16)


def _merge(ya, yb, zc, wa, wb, tm, tn):
    m = ya.shape[0]
    nb = D_MODEL // tn
    return pl.pallas_call(
        _merge_kernel,
        grid=(nb, m // tm),
        in_specs=[
            pl.BlockSpec((tm, A_WIDTH), lambda j, i: (i, 0)),
            pl.BlockSpec((tm, B_GROUP_WIDTH), lambda j, i: (i, 0)),
            pl.BlockSpec((tm, tn), lambda j, i: (i, j)),
            pl.BlockSpec((tm, tn), lambda j, i: (i, nb + j)),
            pl.BlockSpec((A_WIDTH, tn), lambda j, i: (0, j)),
            pl.BlockSpec((B_GROUP_WIDTH, tn), lambda j, i: (0, j)),
        ],
        out_specs=pl.BlockSpec((tm, tn), lambda j, i: (i, j)),
        out_shape=jax.ShapeDtypeStruct((m, D_MODEL), BF16),
        compiler_params=_cparams(("parallel", "arbitrary")),
    )(ya, yb, zc, zc, wa, wb)


def _outproj_kernel(x_ref, a_ref, w_ref, o_ref):
    o_ref[...] = x_ref[...] + jnp.dot(a_ref[...], w_ref[...], preferred_element_type=F32)


def _outproj(x, merged, row_block0, w, tm, tn):
    m = x.shape[0]
    return pl.pallas_call(
        _outproj_kernel,
        grid=(D_MODEL // tn, m // tm),
        in_specs=[
            pl.BlockSpec((tm, tn), lambda j, i: (i, j)),
            pl.BlockSpec((tm, D_MODEL), lambda j, i: (row_block0 + i, 0)),
            pl.BlockSpec((D_MODEL, tn), lambda j, i: (0, j)),
        ],
        out_specs=pl.BlockSpec((tm, tn), lambda j, i: (i, j)),
        out_shape=jax.ShapeDtypeStruct((m, D_MODEL), F32),
        compiler_params=_cparams(("parallel", "arbitrary")),
    )(x, merged, w)


def _ple_kernel(hrow_ref, h_ref, p_ref, wg_ref, wp_ref, o_ref):
    gate = _sigmoid(jnp.dot(hrow_ref[...].astype(BF16), wg_ref[...], preferred_element_type=F32))
    proj = jnp.dot(p_ref[...].astype(BF16), wp_ref[...], preferred_element_type=F32)
    o_ref[...] = h_ref[...] + gate * proj


def _ple(h, p, wg, wp, tm, tn):
    m = h.shape[0]
    pd = p.shape[1]
    return pl.pallas_call(
        _ple_kernel,
        grid=(D_MODEL // tn, m // tm),
        in_specs=[
            pl.BlockSpec((tm, D_MODEL), lambda j, i: (i, 0)),
            pl.BlockSpec((tm, tn), lambda j, i: (i, j)),
            pl.BlockSpec((tm, pd), lambda j, i: (i, 0)),
            pl.BlockSpec((D_MODEL, tn), lambda j, i: (0, j)),
            pl.BlockSpec((pd, tn), lambda j, i: (0, j)),
        ],
        out_specs=pl.BlockSpec((tm, tn), lambda j, i: (i, j)),
        out_shape=jax.ShapeDtypeStruct((m, D_MODEL), F32),
        compiler_params=_cparams(("parallel", "arbitrary")),
    )(h, h, p, wg, wp)


def _rope_tables(pos):
    half = B_HEAD_DIM // 2
    inv = ROPE_THETA ** (-jnp.arange(half, dtype=F32) / half)
    ang = pos.astype(F32)[:, None] * inv[None, :]
    cos, sin = jnp.cos(ang), jnp.sin(ang)
    return jnp.concatenate([cos, cos], axis=1), jnp.concatenate([-sin, sin], axis=1)


def kernel(x_prompt, x_sample, state_shift, state_wkv, cache_kv_w128, cache_kv_w512, cache_kv_w2048,
           p_prompt, p_sample, ln_g, w_in, mu, w0, w2, a0, a2, k_k, k_a, r_k, gn_w, gn_b,
           q_gain, k_gain, w_up_a, w_up_b, w_out, w_ple_gate, w_ple_proj):
    depth = ln_g.shape[0]
    assert depth == 1
    bp, seq, d = x_prompt.shape
    bs, t_new, _ = x_sample.shape
    assert bp == 1 and d == D_MODEL
    rs = bs * t_new
    caches = (cache_kv_w128, cache_kv_w512, cache_kv_w2048)

    xp = x_prompt.reshape(seq, d)
    xs = x_sample.reshape(rs, d)
    hn = _rmsnorm(xp, xs, ln_g)

    w_in0 = w_in[0]
    off_b = ZA_WIDTH
    off_c = ZA_WIDTH + ZB_WIDTH
    wa_in = w_in0[:, :off_b].astype(BF16)
    wb_in = w_in0[:, off_b:off_c].astype(BF16)
    wc_in = jnp.concatenate([w_in0[:, off_c + B_GROUP_WIDTH:], w_in0[:, off_c:off_c + B_GROUP_WIDTH]],
                            axis=1).astype(BF16)
    tm_in = 768
    za = _matmul(hn, wa_in, tm_in, ZA_WIDTH // 3)
    zb = _matmul(hn, wb_in, tm_in, ZB_WIDTH // 3)
    zc = _matmul(hn, wc_in, tm_in, ZC_WIDTH // 3)

    flat = lambda v: v.reshape(1, -1)
    rw = (mu, flat(w0[0]), w2[0], flat(a0[0]), a2[0], flat(k_k[0]), flat(k_a[0]), flat(r_k[0]),
          flat(gn_w[0]), flat(gn_b[0]))
    tc = 128
    ya_p, s_p = _rwkv(za, 0, 1, seq, tc, jnp.zeros((1, 1, A_SHIFT_WIDTH), F32),
                      jnp.zeros((1, A_PAIRS, A_HEAD_DIM, LANES), F32), *rw)
    ya_s, s_s = _rwkv(za, seq // t_new, bs, t_new, t_new, state_shift[0][:, None, :],
                      _state_to_pairs(state_wkv[0]), *rw)
    shift_p = za[seq - 1, :A_SHIFT_WIDTH].reshape(1, 1, A_SHIFT_WIDTH)
    shift_s = za[seq:, :A_SHIFT_WIDTH].reshape(bs, t_new, A_SHIFT_WIDTH)[:, -1][None]
    wkv_p = _pairs_to_state(s_p)[None]
    wkv_s = _pairs_to_state(s_s)[None]

    cos_p, sin_p = _rope_tables(jnp.arange(seq, dtype=jnp.int32))
    cos_s, sin_s = _rope_tables(PAST_LEN + jnp.arange(t_new, dtype=jnp.int32))
    o_p, l_p, o_s, l_s, kv_p, kv_s = [], [], [], [], [], []
    for gi, (window, dil) in enumerate(ATT_GROUPS):
        o, lse, kr = _attn_prompt(zb, gi, dil, seq, cos_p, sin_p, q_gain, k_gain)
        o_p.append(o)
        l_p.append(lse)
        w_keep = min(window, seq)
        v_g = zb[seq - w_keep:seq, 2 * B_QKV_WIDTH + gi * B_GROUP_WIDTH:2 * B_QKV_WIDTH + (gi + 1) * B_GROUP_WIDTH]
        kv_p.append(jnp.stack([kr[seq - w_keep:].reshape(w_keep, HEADS_PER_GROUP, B_HEAD_DIM),
                               v_g.reshape(w_keep, HEADS_PER_GROUP, B_HEAD_DIM)], axis=1)[None, None])
        cache = caches[gi][0]
        o, lse, kr = _attn_sample(zb, gi, cache.shape[1], dil, seq // t_new, bs, t_new, cache,
                                  cos_s, sin_s, q_gain, k_gain)
        o_s.append(o)
        l_s.append(lse)
        v_n = zb[seq:, 2 * B_QKV_WIDTH + gi * B_GROUP_WIDTH:2 * B_QKV_WIDTH + (gi + 1) * B_GROUP_WIDTH]
        kv_new = jnp.stack([kr.reshape(bs, t_new, HEADS_PER_GROUP, B_HEAD_DIM),
                            v_n.reshape(bs, t_new, HEADS_PER_GROUP, B_HEAD_DIM)], axis=2)
        kv_s.append(jnp.concatenate([cache[:, t_new:], kv_new], axis=1)[None])
    yb_p = _combine(o_p, l_p, zc, 0, 1024)
    yb_s = _combine(o_s, l_s, zc, seq // rs, rs)

    ya = jnp.concatenate([ya_p, ya_s], axis=0)
    yb = jnp.concatenate([yb_p, yb_s], axis=0)
    merged = _merge(ya, yb, zc, w_up_a[0].astype(BF16), w_up_b[0].astype(BF16), tm_in, 1024)
    w_out_b = w_out[0].astype(BF16)
    h_p = _outproj(xp, merged, 0, w_out_b, 1024, 1024)
    h_s = _outproj(xs, merged, seq // rs, w_out_b, rs, 1024)
    wg = w_ple_gate[0].astype(BF16)
    wp = w_ple_proj[0].astype(BF16)
    y_p = _ple(h_p, p_prompt[0, 0], wg, wp, 512, 1024)
    y_s = _ple(h_s, p_sample[0].reshape(rs, -1), wg, wp, rs, 1024)

    return (y_p.reshape(bp, seq, d), y_s.reshape(bs, t_new, d),
            shift_p, wkv_p, kv_p[0], kv_p[1], kv_p[2],
            shift_s, wkv_s, kv_s[0], kv_s[1], kv_s[2])
```

```python
import functools

import jax
import jax.numpy as jnp
from jax import lax
from jax.experimental import pallas as pl
from jax.experimental.pallas import tpu as pltpu

F32 = jnp.float32
BF16 = jnp.bfloat16

LANES = 128
SUBLANES = 8
D_MODEL = 2048
A_HEAD_DIM = 64
A_WIDTH = 1024
A_HEADS = 16
QUAD_LANES = 256
A_QUADS = A_WIDTH // QUAD_LANES
LORA_RANK = 64
A_SHIFT_WIDTH = 3 * A_WIDTH + 2 * LORA_RANK
GN_EPS = 64e-5
NORM_EPS = 1e-6
ATT_GROUPS = ((128, 1), (512, 4), (2048, 16))
N_BACK = 128
HEADS_PER_GROUP = 4
B_HEAD_DIM = 128
B_GROUP_WIDTH = HEADS_PER_GROUP * B_HEAD_DIM
B_QKV_WIDTH = 3 * B_GROUP_WIDTH
ROPE_THETA = 10000.0
PAST_LEN = 16384
ZA_WIDTH = A_SHIFT_WIDTH + A_WIDTH
ZB_WIDTH = 3 * B_QKV_WIDTH
ZC_WIDTH = 2 * D_MODEL + B_GROUP_WIDTH
VMEM_LIMIT = 48 * 1024 * 1024


def _cparams(sem):
    return pltpu.CompilerParams(dimension_semantics=sem, vmem_limit_bytes=VMEM_LIMIT)


def _sigmoid(x):
    return 1.0 / (1.0 + jnp.exp(-x))


def _rmsnorm_kernel(xp_ref, xs_ref, g_ref, o_ref, *, n_prompt_tiles):
    def emit(x):
        y = x * lax.rsqrt(jnp.mean(x * x, axis=-1, keepdims=True) + NORM_EPS)
        o_ref[...] = (y * g_ref[...]).astype(BF16)

    i = pl.program_id(0)

    @pl.when(i < n_prompt_tiles)
    def _():
        emit(xp_ref[...])

    @pl.when(i >= n_prompt_tiles)
    def _():
        emit(xs_ref[...])


def _rmsnorm(xp, xs, g):
    rp, d = xp.shape
    rs = xs.shape[0]
    tm = rs
    n_p = rp // tm
    return pl.pallas_call(
        functools.partial(_rmsnorm_kernel, n_prompt_tiles=n_p),
        grid=(n_p + 1,),
        in_specs=[
            pl.BlockSpec((tm, d), lambda i: (jnp.minimum(i, n_p - 1), 0)),
            pl.BlockSpec((tm, d), lambda i: (0, 0)),
            pl.BlockSpec((1, d), lambda i: (0, 0)),
        ],
        out_specs=pl.BlockSpec((tm, d), lambda i: (i, 0)),
        out_shape=jax.ShapeDtypeStruct((rp + rs, d), BF16),
        compiler_params=_cparams(("arbitrary",)),
    )(xp, xs, g)


def _mm_kernel(a_ref, w_ref, o_ref):
    o_ref[...] = jnp.dot(a_ref[...], w_ref[...], preferred_element_type=F32)


def _matmul(a, w, tm, tn):
    m, k = a.shape
    n = w.shape[1]
    return pl.pallas_call(
        _mm_kernel,
        grid=(n // tn, m // tm),
        in_specs=[
            pl.BlockSpec((tm, k), lambda j, i: (i, 0)),
            pl.BlockSpec((k, tn), lambda j, i: (0, j)),
        ],
        out_specs=pl.BlockSpec((tm, tn), lambda j, i: (i, j)),
        out_shape=jax.ShapeDtypeStruct((m, n), F32),
        compiler_params=_cparams(("parallel", "arbitrary")),
    )(a, w)


def _split_dot(x, ones_bf16):
    hi = x.astype(BF16)
    r1 = x - hi.astype(F32)
    mid = r1.astype(BF16)
    lo = (r1 - mid.astype(F32)).astype(BF16)
    out = jnp.dot(hi, ones_bf16, preferred_element_type=F32)
    out += jnp.dot(mid, ones_bf16, preferred_element_type=F32)
    out += jnp.dot(lo, ones_bf16, preferred_element_type=F32)
    return out


def _rwkv_kernel(z_ref, shift0_ref, s0_ref, mu_ref, w0_ref, w2_ref, a0_ref, a2_ref, kk_ref, ka_ref,
                 rk_ref, gnw_ref, gnb_ref,
                 ya_ref, sT_ref,
                 carry_ref, s_ref, nkk_ref, w_ref, b_ref, k_ref, r_ref, v_ref, y_ref, bonus_ref,
                 *, tc):
    c = pl.program_id(1)
    n_c = pl.num_programs(1)
    s_rows = A_QUADS * A_HEAD_DIM

    @pl.when(c == 0)
    def _():
        carry_ref[...] = shift0_ref[0]
        s_ref[...] = s0_ref[0].reshape(s_rows, QUAD_LANES)

    lane = lax.broadcasted_iota(jnp.int32, (QUAD_LANES, QUAD_LANES), 1)
    row = lax.broadcasted_iota(jnp.int32, (QUAD_LANES, QUAD_LANES), 0)
    head_ones = (lane // A_HEAD_DIM == row // A_HEAD_DIM).astype(BF16)
    diag = (lane % A_HEAD_DIM) == (row % A_HEAD_DIM)

    z = z_ref[:, :A_SHIFT_WIDTH]
    prev = pltpu.roll(z, 1, axis=0)
    first = lax.broadcasted_iota(jnp.int32, (tc, 1), 0) == 0
    prev = jnp.where(first, carry_ref[...], prev)
    carry_ref[...] = z[tc - 1:tc, :]
    zs = z + (prev - z) * mu_ref[...]

    lora = zs[:, 3 * A_WIDTH:]
    w_lo = jnp.tanh(lora[:, :LORA_RANK]).astype(BF16)
    a_lo = lora[:, LORA_RANK:].astype(BF16)
    lw = w0_ref[...] + jnp.dot(w_lo, w2_ref[...].astype(BF16), preferred_element_type=F32)
    nlw = -lw
    log_w = -(jnp.maximum(nlw, 0.0) + jnp.log1p(jnp.exp(-jnp.abs(nlw)))) - 0.5
    w_ref[...] = jnp.exp(-jnp.exp(log_w))
    a = _sigmoid(a0_ref[...] + jnp.dot(a_lo, a2_ref[...].astype(BF16), preferred_element_type=F32))

    for q in range(A_QUADS):
        sl = slice(q * QUAD_LANES, (q + 1) * QUAD_LANES)
        r_q = zs[:, q * QUAD_LANES:(q + 1) * QUAD_LANES]
        k_q = zs[:, A_WIDTH + q * QUAD_LANES:A_WIDTH + (q + 1) * QUAD_LANES]
        v_q = zs[:, 2 * A_WIDTH + q * QUAD_LANES:2 * A_WIDTH + (q + 1) * QUAD_LANES]
        a_q = a[:, sl]
        kk = k_q * kk_ref[:, sl]
        n2 = _split_dot(kk * kk, head_ones)
        kk = kk / jnp.maximum(jnp.sqrt(n2), 1e-12)
        k_mod = k_q * (1.0 + (a_q - 1.0) * ka_ref[:, sl])
        nkk_ref[:, sl] = -kk
        b_ref[:, sl] = kk * a_q
        k_ref[:, sl] = k_mod
        r_ref[:, sl] = r_q
        v_ref[:, sl] = v_q
        bonus_ref[:, sl] = _split_dot(r_q * k_mod * rk_ref[:, sl], head_ones) * v_q

    row8 = lax.broadcasted_iota(jnp.int32, (SUBLANES, QUAD_LANES), 0)

    def expand(tiles, u):
        return jnp.concatenate(
            [jnp.broadcast_to(t[u:u + 1], (A_HEAD_DIM, QUAD_LANES)) for t in tiles], axis=0)

    def group(g, carry):
        rows = pl.ds(pl.multiple_of(g * SUBLANES, SUBLANES), SUBLANES)
        tiles = lambda ref: [ref[rows, q * QUAD_LANES:(q + 1) * QUAD_LANES] for q in range(A_QUADS)]
        nkk8, w8, b8, k8, r8, v8 = (tiles(ref) for ref in (nkk_ref, w_ref, b_ref, k_ref, r_ref, v_ref))
        s = s_ref[...]
        ytiles = [jnp.zeros((SUBLANES, QUAD_LANES), F32) for _ in range(A_QUADS)]
        for u in range(SUBLANES):
            sa = jnp.dot((s * expand(nkk8, u)).astype(BF16), head_ones, preferred_element_type=F32)
            vcol = jnp.dot(jnp.where(diag, expand(v8, u), 0.0).astype(BF16), head_ones,
                           preferred_element_type=F32)
            s = s * expand(w8, u) + sa * expand(b8, u) + vcol * expand(k8, u)
            ycol = jnp.dot((s * expand(r8, u)).astype(BF16), head_ones, preferred_element_type=F32)
            ysel = jnp.where(diag, ycol, 0.0)
            for q in range(A_QUADS):
                yrow = jnp.sum(ysel[q * A_HEAD_DIM:(q + 1) * A_HEAD_DIM], axis=0, keepdims=True)
                ytiles[q] = jnp.where(row8 == u, yrow, ytiles[q])
        s_ref[...] = s
        for q in range(A_QUADS):
            y_ref[rows, q * QUAD_LANES:(q + 1) * QUAD_LANES] = ytiles[q]
        return carry

    lax.fori_loop(0, tc // SUBLANES, group, 0)

    for q in range(A_QUADS):
        sl = slice(q * QUAD_LANES, (q + 1) * QUAD_LANES)
        y = y_ref[:, sl]
        mean = _split_dot(y, head_ones) * (1.0 / A_HEAD_DIM)
        yc = y - mean
        var = _split_dot(yc * yc, head_ones) * (1.0 / A_HEAD_DIM)
        yn = yc * lax.rsqrt(var + GN_EPS) * gnw_ref[:, sl] + gnb_ref[:, sl]
        yn = yn + bonus_ref[:, sl]
        g = z_ref[:, A_SHIFT_WIDTH + q * QUAD_LANES:A_SHIFT_WIDTH + (q + 1) * QUAD_LANES]
        ya_ref[:, sl] = (yn * (g * _sigmoid(g))).astype(BF16)

    @pl.when(c == n_c - 1)
    def _():
        sT_ref[0] = s_ref[...].reshape(A_QUADS, A_HEAD_DIM, QUAD_LANES)


def _rwkv(za, row_block0, n_seq, t_len, tc, shift0, s0, mu, w0, w2, a0, a2, k_k, k_a, r_k, gn_w, gn_b):
    n_c = t_len // tc
    vec = lambda n: pl.BlockSpec((1, n), lambda b, c: (0, 0))
    row_spec = pl.BlockSpec((tc, ZA_WIDTH), lambda b, c: (row_block0 + b * n_c + c, 0))
    state_spec = pl.BlockSpec((1, A_QUADS, A_HEAD_DIM, QUAD_LANES), lambda b, c: (b, 0, 0, 0))
    return pl.pallas_call(
        functools.partial(_rwkv_kernel, tc=tc),
        grid=(n_seq, n_c),
        in_specs=[
            row_spec,
            pl.BlockSpec((1, 1, A_SHIFT_WIDTH), lambda b, c: (b, 0, 0)),
            state_spec,
            vec(A_SHIFT_WIDTH), vec(A_WIDTH),
            pl.BlockSpec((LORA_RANK, A_WIDTH), lambda b, c: (0, 0)),
            vec(A_WIDTH),
            pl.BlockSpec((LORA_RANK, A_WIDTH), lambda b, c: (0, 0)),
            vec(A_WIDTH), vec(A_WIDTH), vec(A_WIDTH), vec(A_WIDTH), vec(A_WIDTH),
        ],
        out_specs=[
            pl.BlockSpec((tc, A_WIDTH), lambda b, c: (b * n_c + c, 0)),
            state_spec,
        ],
        out_shape=[
            jax.ShapeDtypeStruct((n_seq * t_len, A_WIDTH), BF16),
            jax.ShapeDtypeStruct((n_seq, A_QUADS, A_HEAD_DIM, QUAD_LANES), F32),
        ],
        scratch_shapes=[pltpu.VMEM((1, A_SHIFT_WIDTH), F32),
                        pltpu.VMEM((A_QUADS * A_HEAD_DIM, QUAD_LANES), F32)]
        + [pltpu.VMEM((tc, A_WIDTH), F32) for _ in range(8)],
        compiler_params=_cparams(("arbitrary", "arbitrary")),
    )(za, shift0, s0, mu, w0, w2, a0, a2, k_k, k_a, r_k, gn_w, gn_b)


def _state_to_quads(s):
    b = s.shape[0]
    return s.reshape(b, A_QUADS, 4, A_HEAD_DIM, A_HEAD_DIM).transpose(0, 1, 3, 2, 4).reshape(
        b, A_QUADS, A_HEAD_DIM, QUAD_LANES)


def _quads_to_state(s):
    b = s.shape[0]
    return s.reshape(b, A_QUADS, A_HEAD_DIM, 4, A_HEAD_DIM).transpose(0, 1, 3, 2, 4).reshape(
        b, A_HEADS, A_HEAD_DIM, A_HEAD_DIM)


def _norm_rope(x, gain, cos, sin):
    y = x * lax.rsqrt(jnp.mean(x * x, axis=-1, keepdims=True) + NORM_EPS) * gain
    return y * cos + pltpu.roll(y, B_HEAD_DIM // 2, axis=1) * sin


def _dot_nt(a, b):
    return lax.dot_general(a, b, (((1,), (1,)), ((), ())), preferred_element_type=F32)


def _attn_prompt_kernel(q_ref, k_ref, v_ref, cos_ref, sin_ref, qg_ref, kg_ref,
                        o_ref, lse_ref, kr_ref, kprev_ref, vprev_ref):
    i = pl.program_id(1)
    blk = N_BACK

    @pl.when(i == 0)
    def _():
        kprev_ref[...] = jnp.zeros_like(kprev_ref)
        vprev_ref[...] = jnp.zeros_like(vprev_ref)

    rowi = lax.broadcasted_iota(jnp.int32, (blk, blk), 0)
    coli = lax.broadcasted_iota(jnp.int32, (blk, blk), 1)
    cur_ok = coli <= rowi
    prev_ok = jnp.logical_and(coli >= rowi, i > 0)
    cos = cos_ref[...]
    sin = sin_ref[...]
    scale = B_HEAD_DIM ** -0.5
    for h in range(HEADS_PER_GROUP):
        sl = slice(h * B_HEAD_DIM, (h + 1) * B_HEAD_DIM)
        qn = _norm_rope(q_ref[:, sl], qg_ref[...], cos, sin).astype(BF16)
        kr = _norm_rope(k_ref[:, sl], kg_ref[...], cos, sin)
        kr_ref[:, sl] = kr
        kb = kr.astype(BF16)
        vb = v_ref[:, sl].astype(BF16)
        s_cur = jnp.where(cur_ok, _dot_nt(qn, kb) * scale, -jnp.inf)
        s_prev = jnp.where(prev_ok, _dot_nt(qn, kprev_ref[:, sl]) * scale, -jnp.inf)
        m = jnp.maximum(jnp.max(s_cur, axis=-1, keepdims=True), jnp.max(s_prev, axis=-1, keepdims=True))
        p_cur = jnp.exp(s_cur - m)
        p_prev = jnp.exp(s_prev - m)
        den = jnp.sum(p_cur, axis=-1, keepdims=True) + jnp.sum(p_prev, axis=-1, keepdims=True)
        o = jnp.dot((p_cur / den).astype(BF16), vb, preferred_element_type=F32)
        o += jnp.dot((p_prev / den).astype(BF16), vprev_ref[:, sl], preferred_element_type=F32)
        o_ref[:, sl] = o
        lse_ref[:, sl] = jnp.broadcast_to(m + jnp.log(den), (blk, B_HEAD_DIM))
        kprev_ref[:, sl] = kb
        vprev_ref[:, sl] = vb


def _attn_prompt(zb, gi, dil, seq, cos, sin, q_gain, k_gain):
    rows = zb.shape[0] // dil
    zv = zb.reshape(rows, dil * ZB_WIDTH)
    n_blk = seq // dil // N_BACK
    per_rho = ZB_WIDTH // B_GROUP_WIDTH
    cs = cos.reshape(seq // dil, dil * B_HEAD_DIM)
    sn = sin.reshape(seq // dil, dil * B_HEAD_DIM)
    blk = lambda off: pl.BlockSpec((N_BACK, B_GROUP_WIDTH), lambda r, i: (i, r * per_rho + off + gi))
    tab = pl.BlockSpec((N_BACK, B_HEAD_DIM), lambda r, i: (i, r))
    gain = pl.BlockSpec((1, B_HEAD_DIM), lambda r, i: (0, 0))
    out = pl.BlockSpec((N_BACK, B_GROUP_WIDTH), lambda r, i: (i, r))
    shp = jax.ShapeDtypeStruct((seq // dil, dil * B_GROUP_WIDTH), F32)
    o, lse, kr = pl.pallas_call(
        _attn_prompt_kernel,
        grid=(dil, n_blk),
        in_specs=[blk(0), blk(3), blk(6), tab, tab, gain, gain],
        out_specs=[out, out, out],
        out_shape=[shp, shp, shp],
        scratch_shapes=[pltpu.VMEM((N_BACK, B_GROUP_WIDTH), BF16), pltpu.VMEM((N_BACK, B_GROUP_WIDTH), BF16)],
        compiler_params=_cparams(("arbitrary", "arbitrary")),
    )(zv, zv, zv, cs, sn, q_gain, k_gain)
    return (o.reshape(seq, B_GROUP_WIDTH), lse.reshape(seq, B_GROUP_WIDTH), kr.reshape(seq, B_GROUP_WIDTH))


def _attn_sample_kernel(q_ref, k_ref, v_ref, cache_ref, cos_ref, sin_ref, qg_ref, kg_ref,
                        o_ref, lse_ref, kr_ref, *, window, dil, t_new):
    cos = cos_ref[...]
    sin = sin_ref[...]
    scale = B_HEAD_DIM ** -0.5
    tq = lax.broadcasted_iota(jnp.int32, (t_new, window), 0)
    cc = lax.broadcasted_iota(jnp.int32, (t_new, window), 1)
    cache_ok = jnp.logical_and(((cc - tq + dil * N_BACK) & (dil - 1)) == 0, cc >= tq)
    tn = lax.broadcasted_iota(jnp.int32, (t_new, t_new), 0)
    un = lax.broadcasted_iota(jnp.int32, (t_new, t_new), 1)
    new_ok = jnp.logical_and(((tn - un + dil * N_BACK) & (dil - 1)) == 0, un <= tn)
    for h in range(HEADS_PER_GROUP):
        sl = slice(h * B_HEAD_DIM, (h + 1) * B_HEAD_DIM)
        qn = _norm_rope(q_ref[:, sl], qg_ref[...], cos, sin).astype(BF16)
        kr = _norm_rope(k_ref[:, sl], kg_ref[...], cos, sin)
        kr_ref[:, sl] = kr
        kc = cache_ref[0, :, sl].astype(BF16)
        vc = cache_ref[0, :, B_GROUP_WIDTH + h * B_HEAD_DIM:B_GROUP_WIDTH + (h + 1) * B_HEAD_DIM].astype(BF16)
        s_c = jnp.where(cache_ok, _dot_nt(qn, kc) * scale, -jnp.inf)
        s_n = jnp.where(new_ok, _dot_nt(qn, kr.astype(BF16)) * scale, -jnp.inf)
        m = jnp.maximum(jnp.max(s_c, axis=-1, keepdims=True), jnp.max(s_n, axis=-1, keepdims=True))
        p_c = jnp.exp(s_c - m)
        p_n = jnp.exp(s_n - m)
        den = jnp.sum(p_c, axis=-1, keepdims=True) + jnp.sum(p_n, axis=-1, keepdims=True)
        o = jnp.dot((p_c / den).astype(BF16), vc, preferred_element_type=F32)
        o += jnp.dot((p_n / den).astype(BF16), v_ref[:, sl].astype(BF16), preferred_element_type=F32)
        o_ref[:, sl] = o
        lse_ref[:, sl] = jnp.broadcast_to(m + jnp.log(den), (t_new, B_HEAD_DIM))


def _attn_sample(zb, gi, window, dil, row_block0, n_seq, t_new, cache, cos, sin, q_gain, k_gain):
    cache2 = cache.reshape(n_seq, window, 2 * B_GROUP_WIDTH)
    blk = lambda off: pl.BlockSpec((t_new, B_GROUP_WIDTH), lambda b: (row_block0 + b, off + gi))
    tab = pl.BlockSpec((t_new, B_HEAD_DIM), lambda b: (0, 0))
    gain = pl.BlockSpec((1, B_HEAD_DIM), lambda b: (0, 0))
    out = pl.BlockSpec((t_new, B_GROUP_WIDTH), lambda b: (b, 0))
    shp = jax.ShapeDtypeStruct((n_seq * t_new, B_GROUP_WIDTH), F32)
    return pl.pallas_call(
        functools.partial(_attn_sample_kernel, window=window, dil=dil, t_new=t_new),
        grid=(n_seq,),
        in_specs=[blk(0), blk(3), blk(6),
                  pl.BlockSpec((1, window, 2 * B_GROUP_WIDTH), lambda b: (b, 0, 0)),
                  tab, tab, gain, gain],
        out_specs=[out, out, out],
        out_shape=[shp, shp, shp],
        compiler_params=_cparams(("arbitrary",)),
    )(zb, zb, zb, cache2, cos, sin, q_gain, k_gain)


def _combine_kernel(o0, o1, o2, l0, l1, l2, g_ref, y_ref):
    a, b, c = l0[...], l1[...], l2[...]
    m = jnp.maximum(jnp.maximum(a, b), c)
    ea, eb, ec = jnp.exp(a - m), jnp.exp(b - m), jnp.exp(c - m)
    o = (ea * o0[...] + eb * o1[...] + ec * o2[...]) / (ea + eb + ec)
    g = g_ref[...]
    y_ref[...] = (o * (g * _sigmoid(g))).astype(BF16)


def _combine(outs, lses, zc, row_block0, tm):
    rows = outs[0].shape[0]
    spec = pl.BlockSpec((tm, B_GROUP_WIDTH), lambda i: (i, 0))
    gate_block = 2 * D_MODEL // B_GROUP_WIDTH
    return pl.pallas_call(
        _combine_kernel,
        grid=(rows // tm,),
        in_specs=[spec] * 6 + [pl.BlockSpec((tm, B_GROUP_WIDTH), lambda i: (row_block0 + i, gate_block))],
        out_specs=spec,
        out_shape=jax.ShapeDtypeStruct((rows, B_GROUP_WIDTH), BF16),
        compiler_params=_cparams(("parallel",)),
    )(*outs, *lses, zc)


def _merge_kernel(ya_ref, yb_ref, ma_ref, mb_ref, wa_ref, wb_ref, o_ref):
    ua = jnp.dot(ya_ref[...], wa_ref[...], preferred_element_type=F32)
    ub = jnp.dot(yb_ref[...], wb_ref[...], preferred_element_type=F32)
    o_ref[...] = (_sigmoid(ma_ref[...]) * ua + _sigmoid(mb_ref[...]) * ub).astype(BF16)


def _merge(ya, yb, zc, wa, wb, tm, tn):
    m = ya.shape[0]
    nb = D_MODEL // tn
    return pl.pallas_call(
        _merge_kernel,
        grid=(nb, m // tm),
        in_specs=[
            pl.BlockSpec((tm, A_WIDTH), lambda j, i: (i, 0)),
            pl.BlockSpec((tm, B_GROUP_WIDTH), lambda j, i: (i, 0)),
            pl.BlockSpec((tm, tn), lambda j, i: (i, j)),
            pl.BlockSpec((tm, tn), lambda j, i: (i, nb + j)),
            pl.BlockSpec((A_WIDTH, tn), lambda j, i: (0, j)),
            pl.BlockSpec((B_GROUP_WIDTH, tn), lambda j, i: (0, j)),
        ],
        out_specs=pl.BlockSpec((tm, tn), lambda j, i: (i, j)),
        out_shape=jax.ShapeDtypeStruct((m, D_MODEL), BF16),
        compiler_params=_cparams(("parallel", "arbitrary")),
    )(ya, yb, zc, zc, wa, wb)


def _outproj_kernel(x_ref, a_ref, w_ref, o_ref):
    o_ref[...] = x_ref[...] + jnp.dot(a_ref[...], w_ref[...], preferred_element_type=F32)


def _outproj(x, merged, row_block0, w, tm, tn):
    m = x.shape[0]
    return pl.pallas_call(
        _outproj_kernel,
        grid=(D_MODEL // tn, m // tm),
        in_specs=[
            pl.BlockSpec((tm, tn), lambda j, i: (i, j)),
            pl.BlockSpec((tm, D_MODEL), lambda j, i: (row_block0 + i, 0)),
            pl.BlockSpec((D_MODEL, tn), lambda j, i: (0, j)),
        ],
        out_specs=pl.BlockSpec((tm, tn), lambda j, i: (i, j)),
        out_shape=jax.ShapeDtypeStruct((m, D_MODEL), F32),
        compiler_params=_cparams(("parallel", "arbitrary")),
    )(x, merged, w)


def _ple_kernel(hrow_ref, h_ref, p_ref, wg_ref, wp_ref, o_ref):
    gate = _sigmoid(jnp.dot(hrow_ref[...].astype(BF16), wg_ref[...], preferred_element_type=F32))
    proj = jnp.dot(p_ref[...].astype(BF16), wp_ref[...], preferred_element_type=F32)
    o_ref[...] = h_ref[...] + gate * proj


def _ple(h, p, wg, wp, tm, tn):
    m = h.shape[0]
    pd = p.shape[1]
    return pl.pallas_call(
        _ple_kernel,
        grid=(D_MODEL // tn, m // tm),
        in_specs=[
            pl.BlockSpec((tm, D_MODEL), lambda j, i: (i, 0)),
            pl.BlockSpec((tm, tn), lambda j, i: (i, j)),
            pl.BlockSpec((tm, pd), lambda j, i: (i, 0)),
            pl.BlockSpec((D_MODEL, tn), lambda j, i: (0, j)),
            pl.BlockSpec((pd, tn), lambda j, i: (0, j)),
        ],
        out_specs=pl.BlockSpec((tm, tn), lambda j, i: (i, j)),
        out_shape=jax.ShapeDtypeStruct((m, D_MODEL), F32),
        compiler_params=_cparams(("parallel", "arbitrary")),
    )(h, h, p, wg, wp)


def _rope_tables(pos):
    half = B_HEAD_DIM // 2
    inv = ROPE_THETA ** (-jnp.arange(half, dtype=F32) / half)
    ang = pos.astype(F32)[:, None] * inv[None, :]
    cos, sin = jnp.cos(ang), jnp.sin(ang)
    return jnp.concatenate([cos, cos], axis=1), jnp.concatenate([-sin, sin], axis=1)


def kernel(x_prompt, x_sample, state_shift, state_wkv, cache_kv_w128, cache_kv_w512, cache_kv_w2048,
           p_prompt, p_sample, ln_g, w_in, mu, w0, w2, a0, a2, k_k, k_a, r_k, gn_w, gn_b,
           q_gain, k_gain, w_up_a, w_up_b, w_out, w_ple_gate, w_ple_proj):
    depth = ln_g.shape[0]
    assert depth == 1
    bp, seq, d = x_prompt.shape
    bs, t_new, _ = x_sample.shape
    assert bp == 1 and d == D_MODEL
    rs = bs * t_new
    caches = (cache_kv_w128, cache_kv_w512, cache_kv_w2048)

    xp = x_prompt.reshape(seq, d)
    xs = x_sample.reshape(rs, d)
    hn = _rmsnorm(xp, xs, ln_g)

    w_in0 = w_in[0]
    off_b = ZA_WIDTH
    off_c = ZA_WIDTH + ZB_WIDTH
    wa_in = w_in0[:, :off_b].astype(BF16)
    wb_in = w_in0[:, off_b:off_c].astype(BF16)
    wc_in = jnp.concatenate([w_in0[:, off_c + B_GROUP_WIDTH:], w_in0[:, off_c:off_c + B_GROUP_WIDTH]],
                            axis=1).astype(BF16)
    tm_in = 768
    za = _matmul(hn, wa_in, tm_in, ZA_WIDTH // 3)
    zb = _matmul(hn, wb_in, tm_in, ZB_WIDTH // 3)
    zc = _matmul(hn, wc_in, tm_in, ZC_WIDTH // 3)

    flat = lambda v: v.reshape(1, -1)
    rw = (mu, flat(w0[0]), w2[0], flat(a0[0]), a2[0], flat(k_k[0]), flat(k_a[0]), flat(r_k[0]),
          flat(gn_w[0]), flat(gn_b[0]))
    tc = 128
    ya_p, s_p = _rwkv(za, 0, 1, seq, tc, jnp.zeros((1, 1, A_SHIFT_WIDTH), F32),
                      jnp.zeros((1, A_QUADS, A_HEAD_DIM, QUAD_LANES), F32), *rw)
    ya_s, s_s = _rwkv(za, seq // t_new, bs, t_new, t_new, state_shift[0][:, None, :],
                      _state_to_quads(state_wkv[0]), *rw)
    shift_p = za[seq - 1, :A_SHIFT_WIDTH].reshape(1, 1, A_SHIFT_WIDTH)
    shift_s = za[seq:, :A_SHIFT_WIDTH].reshape(bs, t_new, A_SHIFT_WIDTH)[:, -1][None]
    wkv_p = _quads_to_state(s_p)[None]
    wkv_s = _quads_to_state(s_s)[None]

    cos_p, sin_p = _rope_tables(jnp.arange(seq, dtype=jnp.int32))
    cos_s, sin_s = _rope_tables(PAST_LEN + jnp.arange(t_new, dtype=jnp.int32))
    o_p, l_p, o_s, l_s, kv_p, kv_s = [], [], [], [], [], []
    for gi, (window, dil) in enumerate(ATT_GROUPS):
        o, lse, kr = _attn_prompt(zb, gi, dil, seq, cos_p, sin_p, q_gain, k_gain)
        o_p.append(o)
        l_p.append(lse)
        w_keep = min(window, seq)
        v_g = zb[seq - w_keep:seq, 2 * B_QKV_WIDTH + gi * B_GROUP_WIDTH:2 * B_QKV_WIDTH + (gi + 1) * B_GROUP_WIDTH]
        kv_p.append(jnp.stack([kr[seq - w_keep:].reshape(w_keep, HEADS_PER_GROUP, B_HEAD_DIM),
                               v_g.reshape(w_keep, HEADS_PER_GROUP, B_HEAD_DIM)], axis=1)[None, None])
        cache = caches[gi][0]
        o, lse, kr = _attn_sample(zb, gi, cache.shape[1], dil, seq // t_new, bs, t_new, cache,
                                  cos_s, sin_s, q_gain, k_gain)
        o_s.append(o)
        l_s.append(lse)
        v_n = zb[seq:, 2 * B_QKV_WIDTH + gi * B_GROUP_WIDTH:2 * B_QKV_WIDTH + (gi + 1) * B_GROUP_WIDTH]
        kv_new = jnp.stack([kr.reshape(bs, t_new, HEADS_PER_GROUP, B_HEAD_DIM),
                            v_n.reshape(bs, t_new, HEADS_PER_GROUP, B_HEAD_DIM)], axis=2)
        kv_s.append(jnp.concatenate([cache[:, t_new:], kv_new], axis=1)[None])
    yb_p = _combine(o_p, l_p, zc, 0, 1024)
    yb_s = _combine(o_s, l_s, zc, seq // rs, rs)

    ya = jnp.concatenate([ya_p, ya_s], axis=0)
    yb = jnp.concatenate([yb_p, yb_s], axis=0)
    merged = _merge(ya, yb, zc, w_up_a[0].astype(BF16), w_up_b[0].astype(BF16), tm_in, 1024)
    w_out_b = w_out[0].astype(BF16)
    h_p = _outproj(xp, merged, 0, w_out_b, 1024, 1024)
    h_s = _outproj(xs, merged, seq // rs, w_out_b, rs, 1024)
    wg = w_ple_gate[0].astype(BF16)
    wp = w_ple_proj[0].astype(BF16)
    y_p = _ple(h_p, p_prompt[0, 0], wg, wp, 512, 1024)
    y_s = _ple(h_s, p_sample[0].reshape(rs, -1), wg, wp, rs, 1024)

    return (y_p.reshape(bp, seq, d), y_s.reshape(bs, t_new, d),
            shift_p, wkv_p, kv_p[0], kv_p[1], kv_p[2],
            shift_s, wkv_s, kv_s[0], kv_s[1], kv_s[2])
```

```python
import functools

import jax
import jax.numpy as jnp
from jax import lax
from jax.experimental import pallas as pl
from jax.experimental.pallas import tpu as pltpu

F32 = jnp.float32
BF16 = jnp.bfloat16

LANES = 128
SUBLANES = 8
D_MODEL = 2048
A_HEAD_DIM = 64
A_WIDTH = 1024
A_HEADS = 16
QUAD_LANES = 256
A_QUADS = A_WIDTH // QUAD_LANES
LORA_RANK = 64
A_SHIFT_WIDTH = 3 * A_WIDTH + 2 * LORA_RANK
GN_EPS = 64e-5
NORM_EPS = 1e-6
ATT_GROUPS = ((128, 1), (512, 4), (2048, 16))
N_BACK = 128
HEADS_PER_GROUP = 4
B_HEAD_DIM = 128
B_GROUP_WIDTH = HEADS_PER_GROUP * B_HEAD_DIM
B_QKV_WIDTH = 3 * B_GROUP_WIDTH
ATT_PROMPT_ROWS = 1024
ROPE_THETA = 10000.0
PAST_LEN = 16384
ZA_WIDTH = A_SHIFT_WIDTH + A_WIDTH
ZB_WIDTH = 3 * B_QKV_WIDTH
ZC_WIDTH = 2 * D_MODEL + B_GROUP_WIDTH
VMEM_LIMIT = 48 * 1024 * 1024


def _cparams(sem):
    return pltpu.CompilerParams(dimension_semantics=sem, vmem_limit_bytes=VMEM_LIMIT)


def _sigmoid(x):
    return 1.0 / (1.0 + jnp.exp(-x))


def _rmsnorm_kernel(xp_ref, xs_ref, g_ref, o_ref, *, n_prompt_tiles):
    def emit(x):
        y = x * lax.rsqrt(jnp.mean(x * x, axis=-1, keepdims=True) + NORM_EPS)
        o_ref[...] = (y * g_ref[...]).astype(BF16)

    i = pl.program_id(0)

    @pl.when(i < n_prompt_tiles)
    def _():
        emit(xp_ref[...])

    @pl.when(i >= n_prompt_tiles)
    def _():
        emit(xs_ref[...])


def _rmsnorm(xp, xs, g):
    rp, d = xp.shape
    rs = xs.shape[0]
    tm = rs
    n_p = rp // tm
    return pl.pallas_call(
        functools.partial(_rmsnorm_kernel, n_prompt_tiles=n_p),
        grid=(n_p + 1,),
        in_specs=[
            pl.BlockSpec((tm, d), lambda i: (jnp.minimum(i, n_p - 1), 0)),
            pl.BlockSpec((tm, d), lambda i: (0, 0)),
            pl.BlockSpec((1, d), lambda i: (0, 0)),
        ],
        out_specs=pl.BlockSpec((tm, d), lambda i: (i, 0)),
        out_shape=jax.ShapeDtypeStruct((rp + rs, d), BF16),
        compiler_params=_cparams(("arbitrary",)),
    )(xp, xs, g)


def _mm_kernel(a_ref, w_ref, o_ref):
    o_ref[...] = jnp.dot(a_ref[...], w_ref[...], preferred_element_type=F32)


def _matmul(a, w, tm, tn):
    m, k = a.shape
    n = w.shape[1]
    return pl.pallas_call(
        _mm_kernel,
        grid=(n // tn, m // tm),
        in_specs=[
            pl.BlockSpec((tm, k), lambda j, i: (i, 0)),
            pl.BlockSpec((k, tn), lambda j, i: (0, j)),
        ],
        out_specs=pl.BlockSpec((tm, tn), lambda j, i: (i, j)),
        out_shape=jax.ShapeDtypeStruct((m, n), F32),
        compiler_params=_cparams(("parallel", "arbitrary")),
    )(a, w)


def _split_dot(x, ones_bf16):
    hi = x.astype(BF16)
    r1 = x - hi.astype(F32)
    mid = r1.astype(BF16)
    lo = (r1 - mid.astype(F32)).astype(BF16)
    out = jnp.dot(hi, ones_bf16, preferred_element_type=F32)
    out += jnp.dot(mid, ones_bf16, preferred_element_type=F32)
    out += jnp.dot(lo, ones_bf16, preferred_element_type=F32)
    return out


def _rwkv_kernel(z_ref, shift0_ref, s0_ref, mu_ref, w0_ref, w2_ref, a0_ref, a2_ref, kk_ref, ka_ref,
                 rk_ref, gnw_ref, gnb_ref,
                 ya_ref, sT_ref,
                 carry_ref, s_ref, nkk_ref, w_ref, b_ref, k_ref, r_ref, v_ref, y_ref, bonus_ref,
                 *, tc):
    c = pl.program_id(1)
    n_c = pl.num_programs(1)
    s_rows = A_QUADS * A_HEAD_DIM

    @pl.when(c == 0)
    def _():
        carry_ref[...] = shift0_ref[0]
        s_ref[...] = s0_ref[0].reshape(s_rows, QUAD_LANES)

    lane = lax.broadcasted_iota(jnp.int32, (QUAD_LANES, QUAD_LANES), 1)
    row = lax.broadcasted_iota(jnp.int32, (QUAD_LANES, QUAD_LANES), 0)
    head_ones = (lane // A_HEAD_DIM == row // A_HEAD_DIM).astype(BF16)
    diag = (lane % A_HEAD_DIM) == (row % A_HEAD_DIM)

    z = z_ref[:, :A_SHIFT_WIDTH]
    prev = pltpu.roll(z, 1, axis=0)
    first = lax.broadcasted_iota(jnp.int32, (tc, 1), 0) == 0
    prev = jnp.where(first, carry_ref[...], prev)
    carry_ref[...] = z[tc - 1:tc, :]
    zs = z + (prev - z) * mu_ref[...]

    lora = zs[:, 3 * A_WIDTH:]
    w_lo = jnp.tanh(lora[:, :LORA_RANK]).astype(BF16)
    a_lo = lora[:, LORA_RANK:].astype(BF16)
    lw = w0_ref[...] + jnp.dot(w_lo, w2_ref[...].astype(BF16), preferred_element_type=F32)
    nlw = -lw
    log_w = -(jnp.maximum(nlw, 0.0) + jnp.log1p(jnp.exp(-jnp.abs(nlw)))) - 0.5
    w_ref[...] = jnp.exp(-jnp.exp(log_w))
    a = _sigmoid(a0_ref[...] + jnp.dot(a_lo, a2_ref[...].astype(BF16), preferred_element_type=F32))

    for q in range(A_QUADS):
        sl = slice(q * QUAD_LANES, (q + 1) * QUAD_LANES)
        r_q = zs[:, q * QUAD_LANES:(q + 1) * QUAD_LANES]
        k_q = zs[:, A_WIDTH + q * QUAD_LANES:A_WIDTH + (q + 1) * QUAD_LANES]
        v_q = zs[:, 2 * A_WIDTH + q * QUAD_LANES:2 * A_WIDTH + (q + 1) * QUAD_LANES]
        a_q = a[:, sl]
        kk = k_q * kk_ref[:, sl]
        n2 = _split_dot(kk * kk, head_ones)
        kk = kk / jnp.maximum(jnp.sqrt(n2), 1e-12)
        k_mod = k_q * (1.0 + (a_q - 1.0) * ka_ref[:, sl])
        nkk_ref[:, sl] = -kk
        b_ref[:, sl] = kk * a_q
        k_ref[:, sl] = k_mod
        r_ref[:, sl] = r_q
        v_ref[:, sl] = v_q
        bonus_ref[:, sl] = _split_dot(r_q * k_mod * rk_ref[:, sl], head_ones) * v_q

    row8 = lax.broadcasted_iota(jnp.int32, (SUBLANES, QUAD_LANES), 0)

    def expand(tiles, u):
        return jnp.concatenate(
            [jnp.broadcast_to(t[u:u + 1], (A_HEAD_DIM, QUAD_LANES)) for t in tiles], axis=0)

    def group(g, carry):
        rows = pl.ds(pl.multiple_of(g * SUBLANES, SUBLANES), SUBLANES)
        tiles = lambda ref: [ref[rows, q * QUAD_LANES:(q + 1) * QUAD_LANES] for q in range(A_QUADS)]
        nkk8, w8, b8, k8, r8, v8 = (tiles(ref) for ref in (nkk_ref, w_ref, b_ref, k_ref, r_ref, v_ref))
        s = s_ref[...]
        ytiles = [jnp.zeros((SUBLANES, QUAD_LANES), F32) for _ in range(A_QUADS)]
        for u in range(SUBLANES):
            sa = jnp.dot((s * expand(nkk8, u)).astype(BF16), head_ones, preferred_element_type=F32)
            vcol = jnp.dot(jnp.where(diag, expand(v8, u), 0.0).astype(BF16), head_ones,
                           preferred_element_type=F32)
            s = s * expand(w8, u) + sa * expand(b8, u) + vcol * expand(k8, u)
            ycol = jnp.dot((s * expand(r8, u)).astype(BF16), head_ones, preferred_element_type=F32)
            ysel = jnp.where(diag, ycol, 0.0)
            for q in range(A_QUADS):
                yrow = jnp.sum(ysel[q * A_HEAD_DIM:(q + 1) * A_HEAD_DIM], axis=0, keepdims=True)
                ytiles[q] = jnp.where(row8 == u, yrow, ytiles[q])
        s_ref[...] = s
        for q in range(A_QUADS):
            y_ref[rows, q * QUAD_LANES:(q + 1) * QUAD_LANES] = ytiles[q]
        return carry

    lax.fori_loop(0, tc // SUBLANES, group, 0)

    for q in range(A_QUADS):
        sl = slice(q * QUAD_LANES, (q + 1) * QUAD_LANES)
        y = y_ref[:, sl]
        mean = _split_dot(y, head_ones) * (1.0 / A_HEAD_DIM)
        yc = y - mean
        var = _split_dot(yc * yc, head_ones) * (1.0 / A_HEAD_DIM)
        yn = yc * lax.rsqrt(var + GN_EPS) * gnw_ref[:, sl] + gnb_ref[:, sl]
        yn = yn + bonus_ref[:, sl]
        g = z_ref[:, A_SHIFT_WIDTH + q * QUAD_LANES:A_SHIFT_WIDTH + (q + 1) * QUAD_LANES]
        ya_ref[:, sl] = (yn * (g * _sigmoid(g))).astype(BF16)

    @pl.when(c == n_c - 1)
    def _():
        sT_ref[0] = s_ref[...].reshape(A_QUADS, A_HEAD_DIM, QUAD_LANES)


def _rwkv(za, row_block0, n_seq, t_len, tc, shift0, s0, mu, w0, w2, a0, a2, k_k, k_a, r_k, gn_w, gn_b):
    n_c = t_len // tc
    vec = lambda n: pl.BlockSpec((1, n), lambda b, c: (0, 0))
    row_spec = pl.BlockSpec((tc, ZA_WIDTH), lambda b, c: (row_block0 + b * n_c + c, 0))
    state_spec = pl.BlockSpec((1, A_QUADS, A_HEAD_DIM, QUAD_LANES), lambda b, c: (b, 0, 0, 0))
    return pl.pallas_call(
        functools.partial(_rwkv_kernel, tc=tc),
        grid=(n_seq, n_c),
        in_specs=[
            row_spec,
            pl.BlockSpec((1, 1, A_SHIFT_WIDTH), lambda b, c: (b, 0, 0)),
            state_spec,
            vec(A_SHIFT_WIDTH), vec(A_WIDTH),
            pl.BlockSpec((LORA_RANK, A_WIDTH), lambda b, c: (0, 0)),
            vec(A_WIDTH),
            pl.BlockSpec((LORA_RANK, A_WIDTH), lambda b, c: (0, 0)),
            vec(A_WIDTH), vec(A_WIDTH), vec(A_WIDTH), vec(A_WIDTH), vec(A_WIDTH),
        ],
        out_specs=[
            pl.BlockSpec((tc, A_WIDTH), lambda b, c: (b * n_c + c, 0)),
            state_spec,
        ],
        out_shape=[
            jax.ShapeDtypeStruct((n_seq * t_len, A_WIDTH), BF16),
            jax.ShapeDtypeStruct((n_seq, A_QUADS, A_HEAD_DIM, QUAD_LANES), F32),
        ],
        scratch_shapes=[pltpu.VMEM((1, A_SHIFT_WIDTH), F32),
                        pltpu.VMEM((A_QUADS * A_HEAD_DIM, QUAD_LANES), F32)]
        + [pltpu.VMEM((tc, A_WIDTH), F32) for _ in range(8)],
        compiler_params=_cparams(("arbitrary", "arbitrary")),
    )(za, shift0, s0, mu, w0, w2, a0, a2, k_k, k_a, r_k, gn_w, gn_b)


def _state_to_quads(s):
    b = s.shape[0]
    return s.reshape(b, A_QUADS, 4, A_HEAD_DIM, A_HEAD_DIM).transpose(0, 1, 3, 2, 4).reshape(
        b, A_QUADS, A_HEAD_DIM, QUAD_LANES)


def _quads_to_state(s):
    b = s.shape[0]
    return s.reshape(b, A_QUADS, A_HEAD_DIM, 4, A_HEAD_DIM).transpose(0, 1, 3, 2, 4).reshape(
        b, A_HEADS, A_HEAD_DIM, A_HEAD_DIM)


def _norm_rope(x, gain, cos, sin):
    y = x * lax.rsqrt(jnp.mean(x * x, axis=-1, keepdims=True) + NORM_EPS) * gain
    return y * cos + pltpu.roll(y, B_HEAD_DIM // 2, axis=1) * sin


def _dot_nt(a, b):
    return lax.dot_general(a, b, (((1,), (1,)), ((), ())), preferred_element_type=F32)


def _attn_prompt_kernel(q_ref, k_ref, v_ref, cos_ref, sin_ref, qg_ref, kg_ref,
                        o_ref, lse_ref, kr_ref, kprev_ref, vprev_ref, *, dil, n_sub):
    i = pl.program_id(0)
    hd = pl.program_id(1)
    blk = N_BACK

    @pl.when(i == 0)
    def _():
        kprev_ref[hd] = jnp.zeros((dil, blk, B_HEAD_DIM), BF16)
        vprev_ref[hd] = jnp.zeros((dil, blk, B_HEAD_DIM), BF16)

    rowi = lax.broadcasted_iota(jnp.int32, (blk, blk), 0)
    coli = lax.broadcasted_iota(jnp.int32, (blk, blk), 1)
    cur_ok = coli <= rowi
    scale = B_HEAD_DIM ** -0.5
    for sub in range(n_sub):
        prev_ok = coli >= rowi if sub > 0 else jnp.logical_and(coli >= rowi, i > 0)
        for rho in range(dil):
            rows = pl.ds(sub * blk * dil + rho, blk, stride=dil)
            cos = cos_ref[rows, :]
            sin = sin_ref[rows, :]
            qn = _norm_rope(q_ref[rows, :], qg_ref[...], cos, sin).astype(BF16)
            kr = _norm_rope(k_ref[rows, :], kg_ref[...], cos, sin)
            kr_ref[rows, :] = kr
            kb = kr.astype(BF16)
            vb = v_ref[rows, :].astype(BF16)
            s_cur = jnp.where(cur_ok, _dot_nt(qn, kb) * scale, -jnp.inf)
            s_prev = jnp.where(prev_ok, _dot_nt(qn, kprev_ref[hd, rho]) * scale, -jnp.inf)
            m = jnp.maximum(jnp.max(s_cur, axis=-1, keepdims=True), jnp.max(s_prev, axis=-1, keepdims=True))
            p_cur = jnp.exp(s_cur - m)
            p_prev = jnp.exp(s_prev - m)
            den = jnp.sum(p_cur, axis=-1, keepdims=True) + jnp.sum(p_prev, axis=-1, keepdims=True)
            o = jnp.dot((p_cur / den).astype(BF16), vb, preferred_element_type=F32)
            o += jnp.dot((p_prev / den).astype(BF16), vprev_ref[hd, rho], preferred_element_type=F32)
            o_ref[rows, :] = o
            lse_ref[rows, :] = jnp.broadcast_to(m + jnp.log(den), (blk, B_HEAD_DIM))
            kprev_ref[hd, rho] = kb
            vprev_ref[hd, rho] = vb


def _attn_prompt(zb, gi, dil, seq, cos, sin, q_gain, k_gain):
    n_sub = max(1, ATT_PROMPT_ROWS // (N_BACK * dil))
    tr = N_BACK * dil * n_sub
    heads_qkv = B_QKV_WIDTH // B_HEAD_DIM
    blk = lambda which: pl.BlockSpec(
        (tr, B_HEAD_DIM), lambda i, hd: (i, which * heads_qkv + gi * HEADS_PER_GROUP + hd))
    tab = pl.BlockSpec((tr, B_HEAD_DIM), lambda i, hd: (i, 0))
    gain = pl.BlockSpec((1, B_HEAD_DIM), lambda i, hd: (0, 0))
    out = pl.BlockSpec((tr, B_HEAD_DIM), lambda i, hd: (i, hd))
    shp = jax.ShapeDtypeStruct((seq, B_GROUP_WIDTH), F32)
    prev = pltpu.VMEM((HEADS_PER_GROUP, dil, N_BACK, B_HEAD_DIM), BF16)
    return pl.pallas_call(
        functools.partial(_attn_prompt_kernel, dil=dil, n_sub=n_sub),
        grid=(seq // tr, HEADS_PER_GROUP),
        in_specs=[blk(0), blk(1), blk(2), tab, tab, gain, gain],
        out_specs=[out, out, out],
        out_shape=[shp, shp, shp],
        scratch_shapes=[prev, prev],
        compiler_params=_cparams(("arbitrary", "arbitrary")),
    )(zb, zb, zb, cos, sin, q_gain, k_gain)


def _attn_sample_kernel(q_ref, k_ref, v_ref, cache_ref, cos_ref, sin_ref, qg_ref, kg_ref,
                        o_ref, lse_ref, new_ref, *, window, dil, t_new):
    per_pos = 2 * HEADS_PER_GROUP
    new_ref[0, 0:(window - t_new) * per_pos, :] = cache_ref[0, t_new * per_pos:window * per_pos, :]
    cos = cos_ref[...]
    sin = sin_ref[...]
    scale = B_HEAD_DIM ** -0.5
    tq = lax.broadcasted_iota(jnp.int32, (t_new, window), 0)
    cc = lax.broadcasted_iota(jnp.int32, (t_new, window), 1)
    cache_ok = jnp.logical_and(((cc - tq + dil * N_BACK) & (dil - 1)) == 0, cc >= tq)
    tn = lax.broadcasted_iota(jnp.int32, (t_new, t_new), 0)
    un = lax.broadcasted_iota(jnp.int32, (t_new, t_new), 1)
    new_ok = jnp.logical_and(((tn - un + dil * N_BACK) & (dil - 1)) == 0, un <= tn)
    for h in range(HEADS_PER_GROUP):
        sl = slice(h * B_HEAD_DIM, (h + 1) * B_HEAD_DIM)
        qn = _norm_rope(q_ref[:, sl], qg_ref[...], cos, sin).astype(BF16)
        kr = _norm_rope(k_ref[:, sl], kg_ref[...], cos, sin)
        v_new = v_ref[:, sl]
        new_ref[0, pl.ds((window - t_new) * per_pos + h, t_new, stride=per_pos), :] = kr
        new_ref[0, pl.ds((window - t_new) * per_pos + HEADS_PER_GROUP + h, t_new, stride=per_pos), :] = v_new
        kc = cache_ref[0, pl.ds(h, window, stride=per_pos), :].astype(BF16)
        vc = cache_ref[0, pl.ds(HEADS_PER_GROUP + h, window, stride=per_pos), :].astype(BF16)
        s_c = jnp.where(cache_ok, _dot_nt(qn, kc) * scale, -jnp.inf)
        s_n = jnp.where(new_ok, _dot_nt(qn, kr.astype(BF16)) * scale, -jnp.inf)
        m = jnp.maximum(jnp.max(s_c, axis=-1, keepdims=True), jnp.max(s_n, axis=-1, keepdims=True))
        p_c = jnp.exp(s_c - m)
        p_n = jnp.exp(s_n - m)
        den = jnp.sum(p_c, axis=-1, keepdims=True) + jnp.sum(p_n, axis=-1, keepdims=True)
        o = jnp.dot((p_c / den).astype(BF16), vc, preferred_element_type=F32)
        o += jnp.dot((p_n / den).astype(BF16), v_new.astype(BF16), preferred_element_type=F32)
        o_ref[:, sl] = o
        lse_ref[:, sl] = jnp.broadcast_to(m + jnp.log(den), (t_new, B_HEAD_DIM))


def _attn_sample(zb, gi, window, dil, row_block0, n_seq, t_new, cache, cos, sin, q_gain, k_gain):
    rows = window * 2 * HEADS_PER_GROUP
    cache2 = cache.reshape(n_seq, rows, B_HEAD_DIM)
    blk = lambda off: pl.BlockSpec((t_new, B_GROUP_WIDTH), lambda b: (row_block0 + b, off + gi))
    tab = pl.BlockSpec((t_new, B_HEAD_DIM), lambda b: (0, 0))
    gain = pl.BlockSpec((1, B_HEAD_DIM), lambda b: (0, 0))
    out = pl.BlockSpec((t_new, B_GROUP_WIDTH), lambda b: (b, 0))
    win = pl.BlockSpec((1, rows, B_HEAD_DIM), lambda b: (b, 0, 0))
    shp = jax.ShapeDtypeStruct((n_seq * t_new, B_GROUP_WIDTH), F32)
    o, lse, new = pl.pallas_call(
        functools.partial(_attn_sample_kernel, window=window, dil=dil, t_new=t_new),
        grid=(n_seq,),
        in_specs=[blk(0), blk(3), blk(6), win, tab, tab, gain, gain],
        out_specs=[out, out, win],
        out_shape=[shp, shp, jax.ShapeDtypeStruct((n_seq, rows, B_HEAD_DIM), F32)],
        compiler_params=_cparams(("arbitrary",)),
    )(zb, zb, zb, cache2, cos, sin, q_gain, k_gain)
    return o, lse, new.reshape(cache.shape)


def _combine_kernel(o0, o1, o2, l0, l1, l2, g_ref, y_ref):
    a, b, c = l0[...], l1[...], l2[...]
    m = jnp.maximum(jnp.maximum(a, b), c)
    ea, eb, ec = jnp.exp(a - m), jnp.exp(b - m), jnp.exp(c - m)
    o = (ea * o0[...] + eb * o1[...] + ec * o2[...]) / (ea + eb + ec)
    g = g_ref[...]
    y_ref[...] = (o * (g * _sigmoid(g))).astype(BF16)


def _combine(outs, lses, zc, row_block0, tm):
    rows = outs[0].shape[0]
    spec = pl.BlockSpec((tm, B_GROUP_WIDTH), lambda i: (i, 0))
    gate_block = 2 * D_MODEL // B_GROUP_WIDTH
    return pl.pallas_call(
        _combine_kernel,
        grid=(rows // tm,),
        in_specs=[spec] * 6 + [pl.BlockSpec((tm, B_GROUP_WIDTH), lambda i: (row_block0 + i, gate_block))],
        out_specs=spec,
        out_shape=jax.ShapeDtypeStruct((rows, B_GROUP_WIDTH), BF16),
        compiler_params=_cparams(("parallel",)),
    )(*outs, *lses, zc)


def _merge_kernel(ya_ref, yb_ref, ma_ref, mb_ref, wa_ref, wb_ref, o_ref):
    ua = jnp.dot(ya_ref[...], wa_ref[...], preferred_element_type=F32)
    ub = jnp.dot(yb_ref[...], wb_ref[...], preferred_element_type=F32)
    o_ref[...] = (_sigmoid(ma_ref[...]) * ua + _sigmoid(mb_ref[...]) * ub).astype(BF16)


def _merge(ya, yb, zc, wa, wb, tm, tn):
    m = ya.shape[0]
    nb = D_MODEL // tn
    return pl.pallas_call(
        _merge_kernel,
        grid=(nb, m // tm),
        in_specs=[
            pl.BlockSpec((tm, A_WIDTH), lambda j, i: (i, 0)),
            pl.BlockSpec((tm, B_GROUP_WIDTH), lambda j, i: (i, 0)),
            pl.BlockSpec((tm, tn), lambda j, i: (i, j)),
            pl.BlockSpec((tm, tn), lambda j, i: (i, nb + j)),
            pl.BlockSpec((A_WIDTH, tn), lambda j, i: (0, j)),
            pl.BlockSpec((B_GROUP_WIDTH, tn), lambda j, i: (0, j)),
        ],
        out_specs=pl.BlockSpec((tm, tn), lambda j, i: (i, j)),
        out_shape=jax.ShapeDtypeStruct((m, D_MODEL), BF16),
        compiler_params=_cparams(("parallel", "arbitrary")),
    )(ya, yb, zc, zc, wa, wb)


def _outproj_kernel(x_ref, a_ref, w_ref, o_ref):
    o_ref[...] = x_ref[...] + jnp.dot(a_ref[...], w_ref[...], preferred_element_type=F32)


def _outproj(x, merged, row_block0, w, tm, tn):
    m = x.shape[0]
    return pl.pallas_call(
        _outproj_kernel,
        grid=(D_MODEL // tn, m // tm),
        in_specs=[
            pl.BlockSpec((tm, tn), lambda j, i: (i, j)),
            pl.BlockSpec((tm, D_MODEL), lambda j, i: (row_block0 + i, 0)),
            pl.BlockSpec((D_MODEL, tn), lambda j, i: (0, j)),
        ],
        out_specs=pl.BlockSpec((tm, tn), lambda j, i: (i, j)),
        out_shape=jax.ShapeDtypeStruct((m, D_MODEL), F32),
        compiler_params=_cparams(("parallel", "arbitrary")),
    )(x, merged, w)


def _ple_kernel(hrow_ref, h_ref, p_ref, wg_ref, wp_ref, o_ref):
    gate = _sigmoid(jnp.dot(hrow_ref[...].astype(BF16), wg_ref[...], preferred_element_type=F32))
    proj = jnp.dot(p_ref[...].astype(BF16), wp_ref[...], preferred_element_type=F32)
    o_ref[...] = h_ref[...] + gate * proj


def _ple(h, p, wg, wp, tm, tn):
    m = h.shape[0]
    pd = p.shape[1]
    return pl.pallas_call(
        _ple_kernel,
        grid=(D_MODEL // tn, m // tm),
        in_specs=[
            pl.BlockSpec((tm, D_MODEL), lambda j, i: (i, 0)),
            pl.BlockSpec((tm, tn), lambda j, i: (i, j)),
            pl.BlockSpec((tm, pd), lambda j, i: (i, 0)),
            pl.BlockSpec((D_MODEL, tn), lambda j, i: (0, j)),
            pl.BlockSpec((pd, tn), lambda j, i: (0, j)),
        ],
        out_specs=pl.BlockSpec((tm, tn), lambda j, i: (i, j)),
        out_shape=jax.ShapeDtypeStruct((m, D_MODEL), F32),
        compiler_params=_cparams(("parallel", "arbitrary")),
    )(h, h, p, wg, wp)


def _rope_tables(pos):
    half = B_HEAD_DIM // 2
    inv = ROPE_THETA ** (-jnp.arange(half, dtype=F32) / half)
    ang = pos.astype(F32)[:, None] * inv[None, :]
    cos, sin = jnp.cos(ang), jnp.sin(ang)
    return jnp.concatenate([cos, cos], axis=1), jnp.concatenate([-sin, sin], axis=1)


def kernel(x_prompt, x_sample, state_shift, state_wkv, cache_kv_w128, cache_kv_w512, cache_kv_w2048,
           p_prompt, p_sample, ln_g, w_in, mu, w0, w2, a0, a2, k_k, k_a, r_k, gn_w, gn_b,
           q_gain, k_gain, w_up_a, w_up_b, w_out, w_ple_gate, w_ple_proj):
    depth = ln_g.shape[0]
    assert depth == 1
    bp, seq, d = x_prompt.shape
    bs, t_new, _ = x_sample.shape
    assert bp == 1 and d == D_MODEL
    rs = bs * t_new
    caches = (cache_kv_w128, cache_kv_w512, cache_kv_w2048)

    xp = x_prompt.reshape(seq, d)
    xs = x_sample.reshape(rs, d)
    hn = _rmsnorm(xp, xs, ln_g)

    w_in0 = w_in[0]
    off_b = ZA_WIDTH
    off_c = ZA_WIDTH + ZB_WIDTH
    wa_in = w_in0[:, :off_b].astype(BF16)
    wb_in = w_in0[:, off_b:off_c].astype(BF16)
    wc_in = jnp.concatenate([w_in0[:, off_c + B_GROUP_WIDTH:], w_in0[:, off_c:off_c + B_GROUP_WIDTH]],
                            axis=1).astype(BF16)
    tm_in = 768
    za = _matmul(hn, wa_in, tm_in, ZA_WIDTH // 3)
    zb = _matmul(hn, wb_in, tm_in, ZB_WIDTH // 3)
    zc = _matmul(hn, wc_in, tm_in, ZC_WIDTH // 3)

    flat = lambda v: v.reshape(1, -1)
    rw = (mu, flat(w0[0]), w2[0], flat(a0[0]), a2[0], flat(k_k[0]), flat(k_a[0]), flat(r_k[0]),
          flat(gn_w[0]), flat(gn_b[0]))
    tc = 128
    ya_p, s_p = _rwkv(za, 0, 1, seq, tc, jnp.zeros((1, 1, A_SHIFT_WIDTH), F32),
                      jnp.zeros((1, A_QUADS, A_HEAD_DIM, QUAD_LANES), F32), *rw)
    ya_s, s_s = _rwkv(za, seq // t_new, bs, t_new, t_new, state_shift[0][:, None, :],
                      _state_to_quads(state_wkv[0]), *rw)
    shift_p = za[seq - 1, :A_SHIFT_WIDTH].reshape(1, 1, A_SHIFT_WIDTH)
    shift_s = za[seq:, :A_SHIFT_WIDTH].reshape(bs, t_new, A_SHIFT_WIDTH)[:, -1][None]
    wkv_p = _quads_to_state(s_p)[None]
    wkv_s = _quads_to_state(s_s)[None]

    cos_p, sin_p = _rope_tables(jnp.arange(seq, dtype=jnp.int32))
    cos_s, sin_s = _rope_tables(PAST_LEN + jnp.arange(t_new, dtype=jnp.int32))
    o_p, l_p, o_s, l_s, kv_p, kv_s = [], [], [], [], [], []
    for gi, (window, dil) in enumerate(ATT_GROUPS):
        o, lse, kr = _attn_prompt(zb, gi, dil, seq, cos_p, sin_p, q_gain, k_gain)
        o_p.append(o)
        l_p.append(lse)
        w_keep = min(window, seq)
        v_g = zb[seq - w_keep:seq, 2 * B_QKV_WIDTH + gi * B_GROUP_WIDTH:2 * B_QKV_WIDTH + (gi + 1) * B_GROUP_WIDTH]
        kv_p.append(jnp.stack([kr[seq - w_keep:].reshape(w_keep, HEADS_PER_GROUP, B_HEAD_DIM),
                               v_g.reshape(w_keep, HEADS_PER_GROUP, B_HEAD_DIM)], axis=1)[None, None])
        cache = caches[gi][0]
        o, lse, new_cache = _attn_sample(zb, gi, cache.shape[1], dil, seq // t_new, bs, t_new, cache,
                                         cos_s, sin_s, q_gain, k_gain)
        o_s.append(o)
        l_s.append(lse)
        kv_s.append(new_cache[None])
    yb_p = _combine(o_p, l_p, zc, 0, 1024)
    yb_s = _combine(o_s, l_s, zc, seq // rs, rs)

    ya = jnp.concatenate([ya_p, ya_s], axis=0)
    yb = jnp.concatenate([yb_p, yb_s], axis=0)
    merged = _merge(ya, yb, zc, w_up_a[0].astype(BF16), w_up_b[0].astype(BF16), tm_in, 1024)
    w_out_b = w_out[0].astype(BF16)
    h_p = _outproj(xp, merged, 0, w_out_b, 1024, 1024)
    h_s = _outproj(xs, merged, seq // rs, w_out_b, rs, 1024)
    wg = w_ple_gate[0].astype(BF16)
    wp = w_ple_proj[0].astype(BF16)
    y_p = _ple(h_p, p_prompt[0, 0], wg, wp, 512, 1024)
    y_s = _ple(h_s, p_sample[0].reshape(rs, -1), wg, wp, rs, 1024)

    return (y_p.reshape(bp, seq, d), y_s.reshape(bs, t_new, d),
            shift_p, wkv_p, kv_p[0], kv_p[1], kv_p[2],
            shift_s, wkv_s, kv_s[0], kv_s[1], kv_s[2])
```

```python
import functools

import jax
import jax.numpy as jnp
from jax import lax
from jax.experimental import pallas as pl
from jax.experimental.pallas import tpu as pltpu

F32 = jnp.float32
BF16 = jnp.bfloat16

LANES = 128
SUBLANES = 8
D_MODEL = 2048
A_HEAD_DIM = 64
A_WIDTH = 1024
A_HEADS = 16
QUAD_LANES = 256
A_QUADS = A_WIDTH // QUAD_LANES
CHUNK = A_HEAD_DIM
LORA_RANK = 64
A_SHIFT_WIDTH = 3 * A_WIDTH + 2 * LORA_RANK
GN_EPS = 64e-5
NORM_EPS = 1e-6
ATT_GROUPS = ((128, 1), (512, 4), (2048, 16))
N_BACK = 128
HEADS_PER_GROUP = 4
B_HEAD_DIM = 128
B_GROUP_WIDTH = HEADS_PER_GROUP * B_HEAD_DIM
B_QKV_WIDTH = 3 * B_GROUP_WIDTH
ATT_PROMPT_ROWS = 1024
ROPE_THETA = 10000.0
PAST_LEN = 16384
ZA_WIDTH = A_SHIFT_WIDTH + A_WIDTH
ZB_WIDTH = 3 * B_QKV_WIDTH
ZC_WIDTH = 2 * D_MODEL + B_GROUP_WIDTH
VMEM_LIMIT = 48 * 1024 * 1024


def _cparams(sem):
    return pltpu.CompilerParams(dimension_semantics=sem, vmem_limit_bytes=VMEM_LIMIT)


def _sigmoid(x):
    return 1.0 / (1.0 + jnp.exp(-x))


def _rmsnorm_kernel(xp_ref, xs_ref, g_ref, o_ref, *, n_prompt_tiles):
    def emit(x):
        y = x * lax.rsqrt(jnp.mean(x * x, axis=-1, keepdims=True) + NORM_EPS)
        o_ref[...] = (y * g_ref[...]).astype(BF16)

    i = pl.program_id(0)

    @pl.when(i < n_prompt_tiles)
    def _():
        emit(xp_ref[...])

    @pl.when(i >= n_prompt_tiles)
    def _():
        emit(xs_ref[...])


def _rmsnorm(xp, xs, g):
    rp, d = xp.shape
    rs = xs.shape[0]
    tm = rs
    n_p = rp // tm
    return pl.pallas_call(
        functools.partial(_rmsnorm_kernel, n_prompt_tiles=n_p),
        grid=(n_p + 1,),
        in_specs=[
            pl.BlockSpec((tm, d), lambda i: (jnp.minimum(i, n_p - 1), 0)),
            pl.BlockSpec((tm, d), lambda i: (0, 0)),
            pl.BlockSpec((1, d), lambda i: (0, 0)),
        ],
        out_specs=pl.BlockSpec((tm, d), lambda i: (i, 0)),
        out_shape=jax.ShapeDtypeStruct((rp + rs, d), BF16),
        compiler_params=_cparams(("arbitrary",)),
    )(xp, xs, g)


def _mm_kernel(a_ref, w_ref, o_ref):
    o_ref[...] = jnp.dot(a_ref[...], w_ref[...], preferred_element_type=F32)


def _matmul(a, w, tm, tn):
    m, k = a.shape
    n = w.shape[1]
    return pl.pallas_call(
        _mm_kernel,
        grid=(n // tn, m // tm),
        in_specs=[
            pl.BlockSpec((tm, k), lambda j, i: (i, 0)),
            pl.BlockSpec((k, tn), lambda j, i: (0, j)),
        ],
        out_specs=pl.BlockSpec((tm, tn), lambda j, i: (i, j)),
        out_shape=jax.ShapeDtypeStruct((m, n), F32),
        compiler_params=_cparams(("parallel", "arbitrary")),
    )(a, w)


def _split_dot(x, ones_bf16):
    hi = x.astype(BF16)
    r1 = x - hi.astype(F32)
    mid = r1.astype(BF16)
    lo = (r1 - mid.astype(F32)).astype(BF16)
    out = jnp.dot(hi, ones_bf16, preferred_element_type=F32)
    out += jnp.dot(mid, ones_bf16, preferred_element_type=F32)
    out += jnp.dot(lo, ones_bf16, preferred_element_type=F32)
    return out


def _rwkv_kernel(z_ref, shift0_ref, s0_ref, mu_ref, w0_ref, w2_ref, a0_ref, a2_ref, kk_ref, ka_ref,
                 rk_ref, gnw_ref, gnb_ref,
                 ya_ref, sT_ref,
                 carry_ref, s_ref, nkk_ref, w_ref, b_ref, k_ref, r_ref, v_ref, y_ref, bonus_ref,
                 *, tc):
    c = pl.program_id(1)
    n_c = pl.num_programs(1)
    s_rows = A_QUADS * A_HEAD_DIM

    @pl.when(c == 0)
    def _():
        carry_ref[...] = shift0_ref[0]
        s_ref[...] = s0_ref[0].reshape(s_rows, QUAD_LANES)

    lane = lax.broadcasted_iota(jnp.int32, (QUAD_LANES, QUAD_LANES), 1)
    row = lax.broadcasted_iota(jnp.int32, (QUAD_LANES, QUAD_LANES), 0)
    head_ones = (lane // A_HEAD_DIM == row // A_HEAD_DIM).astype(BF16)
    diag = (lane % A_HEAD_DIM) == (row % A_HEAD_DIM)

    z = z_ref[:, :A_SHIFT_WIDTH]
    prev = pltpu.roll(z, 1, axis=0)
    first = lax.broadcasted_iota(jnp.int32, (tc, 1), 0) == 0
    prev = jnp.where(first, carry_ref[...], prev)
    carry_ref[...] = z[tc - 1:tc, :]
    zs = z + (prev - z) * mu_ref[...]

    lora = zs[:, 3 * A_WIDTH:]
    w_lo = jnp.tanh(lora[:, :LORA_RANK]).astype(BF16)
    a_lo = lora[:, LORA_RANK:].astype(BF16)
    lw = w0_ref[...] + jnp.dot(w_lo, w2_ref[...].astype(BF16), preferred_element_type=F32)
    nlw = -lw
    log_w = -(jnp.maximum(nlw, 0.0) + jnp.log1p(jnp.exp(-jnp.abs(nlw)))) - 0.5
    w_ref[...] = jnp.exp(-jnp.exp(log_w))
    a = _sigmoid(a0_ref[...] + jnp.dot(a_lo, a2_ref[...].astype(BF16), preferred_element_type=F32))

    for q in range(A_QUADS):
        sl = slice(q * QUAD_LANES, (q + 1) * QUAD_LANES)
        r_q = zs[:, q * QUAD_LANES:(q + 1) * QUAD_LANES]
        k_q = zs[:, A_WIDTH + q * QUAD_LANES:A_WIDTH + (q + 1) * QUAD_LANES]
        v_q = zs[:, 2 * A_WIDTH + q * QUAD_LANES:2 * A_WIDTH + (q + 1) * QUAD_LANES]
        a_q = a[:, sl]
        kk = k_q * kk_ref[:, sl]
        n2 = _split_dot(kk * kk, head_ones)
        kk = kk / jnp.maximum(jnp.sqrt(n2), 1e-12)
        k_mod = k_q * (1.0 + (a_q - 1.0) * ka_ref[:, sl])
        nkk_ref[:, sl] = -kk
        b_ref[:, sl] = kk * a_q
        k_ref[:, sl] = k_mod
        r_ref[:, sl] = r_q
        v_ref[:, sl] = v_q
        bonus_ref[:, sl] = _split_dot(r_q * k_mod * rk_ref[:, sl], head_ones) * v_q

    row8 = lax.broadcasted_iota(jnp.int32, (SUBLANES, QUAD_LANES), 0)

    def expand(tiles, u):
        return jnp.concatenate(
            [jnp.broadcast_to(t[u:u + 1], (A_HEAD_DIM, QUAD_LANES)) for t in tiles], axis=0)

    def group(g, carry):
        rows = pl.ds(pl.multiple_of(g * SUBLANES, SUBLANES), SUBLANES)
        tiles = lambda ref: [ref[rows, q * QUAD_LANES:(q + 1) * QUAD_LANES] for q in range(A_QUADS)]
        nkk8, w8, b8, k8, r8, v8 = (tiles(ref) for ref in (nkk_ref, w_ref, b_ref, k_ref, r_ref, v_ref))
        s = s_ref[...]
        ytiles = [jnp.zeros((SUBLANES, QUAD_LANES), F32) for _ in range(A_QUADS)]
        for u in range(SUBLANES):
            sa = jnp.dot((s * expand(nkk8, u)).astype(BF16), head_ones, preferred_element_type=F32)
            vcol = jnp.dot(jnp.where(diag, expand(v8, u), 0.0).astype(BF16), head_ones,
                           preferred_element_type=F32)
            s = s * expand(w8, u) + sa * expand(b8, u) + vcol * expand(k8, u)
            ycol = jnp.dot((s * expand(r8, u)).astype(BF16), head_ones, preferred_element_type=F32)
            ysel = jnp.where(diag, ycol, 0.0)
            for q in range(A_QUADS):
                yrow = jnp.sum(ysel[q * A_HEAD_DIM:(q + 1) * A_HEAD_DIM], axis=0, keepdims=True)
                ytiles[q] = jnp.where(row8 == u, yrow, ytiles[q])
        s_ref[...] = s
        for q in range(A_QUADS):
            y_ref[rows, q * QUAD_LANES:(q + 1) * QUAD_LANES] = ytiles[q]
        return carry

    lax.fori_loop(0, tc // SUBLANES, group, 0)

    for q in range(A_QUADS):
        sl = slice(q * QUAD_LANES, (q + 1) * QUAD_LANES)
        y = y_ref[:, sl]
        mean = _split_dot(y, head_ones) * (1.0 / A_HEAD_DIM)
        yc = y - mean
        var = _split_dot(yc * yc, head_ones) * (1.0 / A_HEAD_DIM)
        yn = yc * lax.rsqrt(var + GN_EPS) * gnw_ref[:, sl] + gnb_ref[:, sl]
        yn = yn + bonus_ref[:, sl]
        g = z_ref[:, A_SHIFT_WIDTH + q * QUAD_LANES:A_SHIFT_WIDTH + (q + 1) * QUAD_LANES]
        ya_ref[:, sl] = (yn * (g * _sigmoid(g))).astype(BF16)

    @pl.when(c == n_c - 1)
    def _():
        sT_ref[0] = s_ref[...].reshape(A_QUADS, A_HEAD_DIM, QUAD_LANES)


def _rwkv(za, row_block0, n_seq, t_len, tc, shift0, s0, mu, w0, w2, a0, a2, k_k, k_a, r_k, gn_w, gn_b):
    n_c = t_len // tc
    vec = lambda n: pl.BlockSpec((1, n), lambda b, c: (0, 0))
    row_spec = pl.BlockSpec((tc, ZA_WIDTH), lambda b, c: (row_block0 + b * n_c + c, 0))
    state_spec = pl.BlockSpec((1, A_QUADS, A_HEAD_DIM, QUAD_LANES), lambda b, c: (b, 0, 0, 0))
    return pl.pallas_call(
        functools.partial(_rwkv_kernel, tc=tc),
        grid=(n_seq, n_c),
        in_specs=[
            row_spec,
            pl.BlockSpec((1, 1, A_SHIFT_WIDTH), lambda b, c: (b, 0, 0)),
            state_spec,
            vec(A_SHIFT_WIDTH), vec(A_WIDTH),
            pl.BlockSpec((LORA_RANK, A_WIDTH), lambda b, c: (0, 0)),
            vec(A_WIDTH),
            pl.BlockSpec((LORA_RANK, A_WIDTH), lambda b, c: (0, 0)),
            vec(A_WIDTH), vec(A_WIDTH), vec(A_WIDTH), vec(A_WIDTH), vec(A_WIDTH),
        ],
        out_specs=[
            pl.BlockSpec((tc, A_WIDTH), lambda b, c: (b * n_c + c, 0)),
            state_spec,
        ],
        out_shape=[
            jax.ShapeDtypeStruct((n_seq * t_len, A_WIDTH), BF16),
            jax.ShapeDtypeStruct((n_seq, A_QUADS, A_HEAD_DIM, QUAD_LANES), F32),
        ],
        scratch_shapes=[pltpu.VMEM((1, A_SHIFT_WIDTH), F32),
                        pltpu.VMEM((A_QUADS * A_HEAD_DIM, QUAD_LANES), F32)]
        + [pltpu.VMEM((tc, A_WIDTH), F32) for _ in range(8)],
        compiler_params=_cparams(("arbitrary", "arbitrary")),
    )(za, shift0, s0, mu, w0, w2, a0, a2, k_k, k_a, r_k, gn_w, gn_b)


def _state_to_quads(s):
    b = s.shape[0]
    return s.reshape(b, A_QUADS, 4, A_HEAD_DIM, A_HEAD_DIM).transpose(0, 1, 3, 2, 4).reshape(
        b, A_QUADS, A_HEAD_DIM, QUAD_LANES)


def _quads_to_state(s):
    b = s.shape[0]
    return s.reshape(b, A_QUADS, A_HEAD_DIM, 4, A_HEAD_DIM).transpose(0, 1, 3, 2, 4).reshape(
        b, A_HEADS, A_HEAD_DIM, A_HEAD_DIM)


def _head_blocks(y, same_head):
    return jnp.where(same_head, jnp.concatenate([y, y, y, y], axis=0), jnp.zeros((), y.dtype))


def _split2(x):
    hi = x.astype(BF16)
    return hi, (x - hi.astype(F32)).astype(BF16)


def _head_matmul3(x, ys, same_head):
    xh, xl = _split2(x)
    outs = []
    for y in ys:
        yh, yl = _split2(y)
        bh = _head_blocks(yh, same_head)
        out = jnp.dot(xh, bh, preferred_element_type=F32)
        out += jnp.dot(xl, bh, preferred_element_type=F32)
        out += jnp.dot(xh, _head_blocks(yl, same_head), preferred_element_type=F32)
        outs.append(out)
    return outs


def _head_matmul(x, y, same_head):
    return jnp.dot(x.astype(BF16), _head_blocks(y.astype(BF16), same_head), preferred_element_type=F32)


def _rwkv_chunk_kernel(z_ref, shift0_ref, s0_ref, mu_ref, w0_ref, w2_ref, a0_ref, a2_ref, kk_ref, ka_ref,
                       rk_ref, gnw_ref, gnb_ref,
                       ya_ref, sT_ref, carry_ref, st_ref):
    c = pl.program_id(1)
    n_c = pl.num_programs(1)
    tc = CHUNK

    @pl.when(c == 0)
    def _():
        carry_ref[...] = shift0_ref[0]
        st_ref[...] = s0_ref[0]

    lane = lax.broadcasted_iota(jnp.int32, (QUAD_LANES, QUAD_LANES), 1)
    row = lax.broadcasted_iota(jnp.int32, (QUAD_LANES, QUAD_LANES), 0)
    same_head = lane // A_HEAD_DIM == row // A_HEAD_DIM
    head_ones = same_head.astype(BF16)
    t_idx = lax.broadcasted_iota(jnp.int32, (tc, QUAD_LANES), 0)
    lane_t = lax.broadcasted_iota(jnp.int32, (tc, QUAD_LANES), 1)
    s_idx = lane_t % A_HEAD_DIM
    strict = s_idx < t_idx
    incl = s_idx <= t_idx
    eye = (s_idx == t_idx).astype(F32)
    lane_head = lane_t // A_HEAD_DIM
    tri = (lax.broadcasted_iota(jnp.int32, (tc, tc), 1) <= lax.broadcasted_iota(jnp.int32, (tc, tc), 0)
           ).astype(BF16)

    z = z_ref[:, :A_SHIFT_WIDTH]
    prev = pltpu.roll(z, 1, axis=0)
    first = lax.broadcasted_iota(jnp.int32, (tc, 1), 0) == 0
    prev = jnp.where(first, carry_ref[...], prev)
    carry_ref[...] = z[tc - 1:tc, :]
    zs = z + (prev - z) * mu_ref[...]

    lora = zs[:, 3 * A_WIDTH:]
    w_lo = jnp.tanh(lora[:, :LORA_RANK]).astype(BF16)
    a_lo = lora[:, LORA_RANK:].astype(BF16)
    lw = w0_ref[...] + jnp.dot(w_lo, w2_ref[...].astype(BF16), preferred_element_type=F32)
    nlw = -lw
    log_w = -(jnp.maximum(nlw, 0.0) + jnp.log1p(jnp.exp(-jnp.abs(nlw)))) - 0.5
    log_decay = -jnp.exp(log_w)
    a = _sigmoid(a0_ref[...] + jnp.dot(a_lo, a2_ref[...].astype(BF16), preferred_element_type=F32))

    quads = range(A_QUADS)
    lanes = [slice(q * QUAD_LANES, (q + 1) * QUAD_LANES) for q in quads]

    def tokens(q):
        sl = lanes[q]
        r_q = zs[:, q * QUAD_LANES:(q + 1) * QUAD_LANES]
        k_q = zs[:, A_WIDTH + q * QUAD_LANES:A_WIDTH + (q + 1) * QUAD_LANES]
        v_q = zs[:, 2 * A_WIDTH + q * QUAD_LANES:2 * A_WIDTH + (q + 1) * QUAD_LANES]
        a_q = a[:, sl]
        kk = k_q * kk_ref[:, sl]
        n2 = _split_dot(kk * kk, head_ones)
        kk = kk / jnp.maximum(jnp.sqrt(n2), 1e-12)
        k_mod = k_q * (1.0 + (a_q - 1.0) * ka_ref[:, sl])
        beta = kk * a_q
        bonus = _split_dot(r_q * k_mod * rk_ref[:, sl], head_ones) * v_q
        ld = log_decay[:, sl]
        h1, h2 = _split2(ld)
        h3 = (ld - h1.astype(F32) - h2.astype(F32)).astype(BF16)
        cum = (jnp.dot(tri, h1, preferred_element_type=F32) + jnp.dot(tri, h2, preferred_element_type=F32)
               + jnp.dot(tri, h3, preferred_element_type=F32))
        cum_last = cum[tc - 1:tc, :]
        alpha_p = -kk * jnp.exp(cum - ld)
        r_p = r_q * jnp.exp(cum)
        inv_p = jnp.exp(-cum)
        to_end = jnp.exp(cum_last - cum)
        lhs = jnp.concatenate([alpha_p, r_p], axis=0).astype(BF16)
        g_b = _dot_nt(lhs, _head_blocks((beta * inv_p).astype(BF16), same_head))
        g_k = _dot_nt(lhs, _head_blocks((k_mod * inv_p).astype(BF16), same_head))
        lhs_t = jnp.concatenate([beta * to_end, k_mod * to_end], axis=0).astype(BF16)
        p_col = _split_dot(jnp.where(eye > 0, jnp.exp(cum_last), 0.0), head_ones)
        return dict(v=v_q, bonus=bonus, alpha_p=alpha_p.astype(BF16), r_p=r_p.astype(BF16),
                    a_ab=jnp.where(strict, g_b[:tc], 0.0), a_rb=jnp.where(incl, g_b[tc:], 0.0),
                    a_ak=jnp.where(strict, g_k[:tc], 0.0), a_rk=jnp.where(incl, g_k[tc:], 0.0),
                    lhs_t=lhs_t, p_col=p_col)

    tk = [tokens(q) for q in quads]

    inv = [eye + t["a_ab"] for t in tk]
    x = [_head_matmul3(t["a_ab"], [t["a_ab"]], same_head)[0] for t in tk]
    for _ in range(4):
        res = [_head_matmul3(x[q], [inv[q], x[q]], same_head) for q in quads]
        inv = [inv[q] + res[q][0] for q in quads]
        x = [res[q][1] for q in quads]
    inv = [inv[q] + _head_matmul3(x[q], [inv[q]], same_head)[0] for q in quads]
    w1 = [_head_matmul(t["a_ak"], t["v"], same_head) for t in tk]
    y_v = [_head_matmul(t["a_rk"], t["v"], same_head) for t in tk]

    st0 = [st_ref[q] for q in quads]
    st0_blocks = [_head_blocks(s.astype(BF16), same_head) for s in st0]
    rhs = [w1[q] + jnp.dot(tk[q]["alpha_p"], st0_blocks[q], preferred_element_type=F32) for q in quads]
    u = [_head_matmul3(inv[q], [rhs[q]], same_head)[0] for q in quads]
    for q in quads:
        t = tk[q]
        y = (jnp.dot(t["r_p"], st0_blocks[q], preferred_element_type=F32)
             + _head_matmul(t["a_rb"], u[q], same_head) + y_v[q])
        rhs_t = jnp.concatenate([u[q], t["v"]], axis=0).astype(BF16)
        cross = lax.dot_general(t["lhs_t"], rhs_t, (((0,), (0,)), ((), ())), preferred_element_type=F32)
        new = t["p_col"] * st0[q]
        for h in range(QUAD_LANES // A_HEAD_DIM):
            new += jnp.where(lane_head == h, cross[h * A_HEAD_DIM:(h + 1) * A_HEAD_DIM], 0.0)
        st_ref[q] = new

        sl = lanes[q]
        mean = _split_dot(y, head_ones) * (1.0 / A_HEAD_DIM)
        yc = y - mean
        var = _split_dot(yc * yc, head_ones) * (1.0 / A_HEAD_DIM)
        yn = yc * lax.rsqrt(var + GN_EPS) * gnw_ref[:, sl] + gnb_ref[:, sl]
        yn = yn + t["bonus"]
        g = z_ref[:, A_SHIFT_WIDTH + q * QUAD_LANES:A_SHIFT_WIDTH + (q + 1) * QUAD_LANES]
        ya_ref[:, sl] = (yn * (g * _sigmoid(g))).astype(BF16)

    @pl.when(c == n_c - 1)
    def _():
        sT_ref[0] = st_ref[...]


def _rwkv_chunked(za, t_len, shift0, s0, mu, w0, w2, a0, a2, k_k, k_a, r_k, gn_w, gn_b):
    n_c = t_len // CHUNK
    vec = lambda n: pl.BlockSpec((1, n), lambda b, c: (0, 0))
    state_spec = pl.BlockSpec((1, A_QUADS, A_HEAD_DIM, QUAD_LANES), lambda b, c: (b, 0, 0, 0))
    return pl.pallas_call(
        _rwkv_chunk_kernel,
        grid=(1, n_c),
        in_specs=[
            pl.BlockSpec((CHUNK, ZA_WIDTH), lambda b, c: (c, 0)),
            pl.BlockSpec((1, 1, A_SHIFT_WIDTH), lambda b, c: (b, 0, 0)),
            state_spec,
            vec(A_SHIFT_WIDTH), vec(A_WIDTH),
            pl.BlockSpec((LORA_RANK, A_WIDTH), lambda b, c: (0, 0)),
            vec(A_WIDTH),
            pl.BlockSpec((LORA_RANK, A_WIDTH), lambda b, c: (0, 0)),
            vec(A_WIDTH), vec(A_WIDTH), vec(A_WIDTH), vec(A_WIDTH), vec(A_WIDTH),
        ],
        out_specs=[
            pl.BlockSpec((CHUNK, A_WIDTH), lambda b, c: (c, 0)),
            state_spec,
        ],
        out_shape=[
            jax.ShapeDtypeStruct((t_len, A_WIDTH), BF16),
            jax.ShapeDtypeStruct((1, A_QUADS, A_HEAD_DIM, QUAD_LANES), F32),
        ],
        scratch_shapes=[pltpu.VMEM((1, A_SHIFT_WIDTH), F32),
                        pltpu.VMEM((A_QUADS, A_HEAD_DIM, QUAD_LANES), F32)],
        compiler_params=_cparams(("arbitrary", "arbitrary")),
    )(za, shift0, s0, mu, w0, w2, a0, a2, k_k, k_a, r_k, gn_w, gn_b)


def _keymajor_to_state(s):
    b = s.shape[0]
    return s.reshape(b, A_QUADS, A_HEAD_DIM, 4, A_HEAD_DIM).transpose(0, 1, 3, 4, 2).reshape(
        b, A_HEADS, A_HEAD_DIM, A_HEAD_DIM)


def _norm_rope(x, gain, cos, sin):
    y = x * lax.rsqrt(jnp.mean(x * x, axis=-1, keepdims=True) + NORM_EPS) * gain
    return y * cos + pltpu.roll(y, B_HEAD_DIM // 2, axis=1) * sin


def _dot_nt(a, b):
    return lax.dot_general(a, b, (((1,), (1,)), ((), ())), preferred_element_type=F32)


def _attn_prompt_kernel(q_ref, k_ref, v_ref, cos_ref, sin_ref, qg_ref, kg_ref,
                        o_ref, lse_ref, kr_ref, kprev_ref, vprev_ref, *, dil, n_sub):
    i = pl.program_id(0)
    hd = pl.program_id(1)
    blk = N_BACK

    @pl.when(i == 0)
    def _():
        kprev_ref[hd] = jnp.zeros((dil, blk, B_HEAD_DIM), BF16)
        vprev_ref[hd] = jnp.zeros((dil, blk, B_HEAD_DIM), BF16)

    rowi = lax.broadcasted_iota(jnp.int32, (blk, blk), 0)
    coli = lax.broadcasted_iota(jnp.int32, (blk, blk), 1)
    cur_ok = coli <= rowi
    scale = B_HEAD_DIM ** -0.5
    for sub in range(n_sub):
        prev_ok = coli >= rowi if sub > 0 else jnp.logical_and(coli >= rowi, i > 0)
        for rho in range(dil):
            rows = pl.ds(sub * blk * dil + rho, blk, stride=dil)
            cos = cos_ref[rows, :]
            sin = sin_ref[rows, :]
            qn = _norm_rope(q_ref[rows, :], qg_ref[...], cos, sin).astype(BF16)
            kr = _norm_rope(k_ref[rows, :], kg_ref[...], cos, sin)
            kr_ref[rows, :] = kr
            kb = kr.astype(BF16)
            vb = v_ref[rows, :].astype(BF16)
            s_cur = jnp.where(cur_ok, _dot_nt(qn, kb) * scale, -jnp.inf)
            s_prev = jnp.where(prev_ok, _dot_nt(qn, kprev_ref[hd, rho]) * scale, -jnp.inf)
            m = jnp.maximum(jnp.max(s_cur, axis=-1, keepdims=True), jnp.max(s_prev, axis=-1, keepdims=True))
            p_cur = jnp.exp(s_cur - m)
            p_prev = jnp.exp(s_prev - m)
            den = jnp.sum(p_cur, axis=-1, keepdims=True) + jnp.sum(p_prev, axis=-1, keepdims=True)
            o = jnp.dot((p_cur / den).astype(BF16), vb, preferred_element_type=F32)
            o += jnp.dot((p_prev / den).astype(BF16), vprev_ref[hd, rho], preferred_element_type=F32)
            o_ref[rows, :] = o
            lse_ref[rows, :] = jnp.broadcast_to(m + jnp.log(den), (blk, B_HEAD_DIM))
            kprev_ref[hd, rho] = kb
            vprev_ref[hd, rho] = vb


def _attn_prompt(zb, gi, dil, seq, cos, sin, q_gain, k_gain):
    n_sub = max(1, ATT_PROMPT_ROWS // (N_BACK * dil))
    tr = N_BACK * dil * n_sub
    heads_qkv = B_QKV_WIDTH // B_HEAD_DIM
    blk = lambda which: pl.BlockSpec(
        (tr, B_HEAD_DIM), lambda i, hd: (i, which * heads_qkv + gi * HEADS_PER_GROUP + hd))
    tab = pl.BlockSpec((tr, B_HEAD_DIM), lambda i, hd: (i, 0))
    gain = pl.BlockSpec((1, B_HEAD_DIM), lambda i, hd: (0, 0))
    out = pl.BlockSpec((tr, B_HEAD_DIM), lambda i, hd: (i, hd))
    shp = jax.ShapeDtypeStruct((seq, B_GROUP_WIDTH), F32)
    prev = pltpu.VMEM((HEADS_PER_GROUP, dil, N_BACK, B_HEAD_DIM), BF16)
    return pl.pallas_call(
        functools.partial(_attn_prompt_kernel, dil=dil, n_sub=n_sub),
        grid=(seq // tr, HEADS_PER_GROUP),
        in_specs=[blk(0), blk(1), blk(2), tab, tab, gain, gain],
        out_specs=[out, out, out],
        out_shape=[shp, shp, shp],
        scratch_shapes=[prev, prev],
        compiler_params=_cparams(("arbitrary", "arbitrary")),
    )(zb, zb, zb, cos, sin, q_gain, k_gain)


def _attn_sample_kernel(q_ref, k_ref, v_ref, cache_ref, cos_ref, sin_ref, qg_ref, kg_ref,
                        o_ref, lse_ref, new_ref, *, window, dil, t_new):
    per_pos = 2 * HEADS_PER_GROUP
    new_ref[0, 0:(window - t_new) * per_pos, :] = cache_ref[0, t_new * per_pos:window * per_pos, :]
    cos = cos_ref[...]
    sin = sin_ref[...]
    scale = B_HEAD_DIM ** -0.5
    tq = lax.broadcasted_iota(jnp.int32, (t_new, window), 0)
    cc = lax.broadcasted_iota(jnp.int32, (t_new, window), 1)
    cache_ok = jnp.logical_and(((cc - tq + dil * N_BACK) & (dil - 1)) == 0, cc >= tq)
    tn = lax.broadcasted_iota(jnp.int32, (t_new, t_new), 0)
    un = lax.broadcasted_iota(jnp.int32, (t_new, t_new), 1)
    new_ok = jnp.logical_and(((tn - un + dil * N_BACK) & (dil - 1)) == 0, un <= tn)
    for h in range(HEADS_PER_GROUP):
        sl = slice(h * B_HEAD_DIM, (h + 1) * B_HEAD_DIM)
        qn = _norm_rope(q_ref[:, sl], qg_ref[...], cos, sin).astype(BF16)
        kr = _norm_rope(k_ref[:, sl], kg_ref[...], cos, sin)
        v_new = v_ref[:, sl]
        new_ref[0, pl.ds((window - t_new) * per_pos + h, t_new, stride=per_pos), :] = kr
        new_ref[0, pl.ds((window - t_new) * per_pos + HEADS_PER_GROUP + h, t_new, stride=per_pos), :] = v_new
        kc = cache_ref[0, pl.ds(h, window, stride=per_pos), :].astype(BF16)
        vc = cache_ref[0, pl.ds(HEADS_PER_GROUP + h, window, stride=per_pos), :].astype(BF16)
        s_c = jnp.where(cache_ok, _dot_nt(qn, kc) * scale, -jnp.inf)
        s_n = jnp.where(new_ok, _dot_nt(qn, kr.astype(BF16)) * scale, -jnp.inf)
        m = jnp.maximum(jnp.max(s_c, axis=-1, keepdims=True), jnp.max(s_n, axis=-1, keepdims=True))
        p_c = jnp.exp(s_c - m)
        p_n = jnp.exp(s_n - m)
        den = jnp.sum(p_c, axis=-1, keepdims=True) + jnp.sum(p_n, axis=-1, keepdims=True)
        o = jnp.dot((p_c / den).astype(BF16), vc, preferred_element_type=F32)
        o += jnp.dot((p_n / den).astype(BF16), v_new.astype(BF16), preferred_element_type=F32)
        o_ref[:, sl] = o
        lse_ref[:, sl] = jnp.broadcast_to(m + jnp.log(den), (t_new, B_HEAD_DIM))


def _attn_sample(zb, gi, window, dil, row_block0, n_seq, t_new, cache, cos, sin, q_gain, k_gain):
    rows = window * 2 * HEADS_PER_GROUP
    cache2 = cache.reshape(n_seq, rows, B_HEAD_DIM)
    blk = lambda off: pl.BlockSpec((t_new, B_GROUP_WIDTH), lambda b: (row_block0 + b, off + gi))
    tab = pl.BlockSpec((t_new, B_HEAD_DIM), lambda b: (0, 0))
    gain = pl.BlockSpec((1, B_HEAD_DIM), lambda b: (0, 0))
    out = pl.BlockSpec((t_new, B_GROUP_WIDTH), lambda b: (b, 0))
    win = pl.BlockSpec((1, rows, B_HEAD_DIM), lambda b: (b, 0, 0))
    shp = jax.ShapeDtypeStruct((n_seq * t_new, B_GROUP_WIDTH), F32)
    o, lse, new = pl.pallas_call(
        functools.partial(_attn_sample_kernel, window=window, dil=dil, t_new=t_new),
        grid=(n_seq,),
        in_specs=[blk(0), blk(3), blk(6), win, tab, tab, gain, gain],
        out_specs=[out, out, win],
        out_shape=[shp, shp, jax.ShapeDtypeStruct((n_seq, rows, B_HEAD_DIM), F32)],
        compiler_params=_cparams(("arbitrary",)),
    )(zb, zb, zb, cache2, cos, sin, q_gain, k_gain)
    return o, lse, new.reshape(cache.shape)


def _combine_kernel(o0, o1, o2, l0, l1, l2, g_ref, y_ref):
    a, b, c = l0[...], l1[...], l2[...]
    m = jnp.maximum(jnp.maximum(a, b), c)
    ea, eb, ec = jnp.exp(a - m), jnp.exp(b - m), jnp.exp(c - m)
    o = (ea * o0[...] + eb * o1[...] + ec * o2[...]) / (ea + eb + ec)
    g = g_ref[...]
    y_ref[...] = (o * (g * _sigmoid(g))).astype(BF16)


def _combine(outs, lses, zc, row_block0, tm):
    rows = outs[0].shape[0]
    spec = pl.BlockSpec((tm, B_GROUP_WIDTH), lambda i: (i, 0))
    gate_block = 2 * D_MODEL // B_GROUP_WIDTH
    return pl.pallas_call(
        _combine_kernel,
        grid=(rows // tm,),
        in_specs=[spec] * 6 + [pl.BlockSpec((tm, B_GROUP_WIDTH), lambda i: (row_block0 + i, gate_block))],
        out_specs=spec,
        out_shape=jax.ShapeDtypeStruct((rows, B_GROUP_WIDTH), BF16),
        compiler_params=_cparams(("parallel",)),
    )(*outs, *lses, zc)


def _merge_kernel(ya_ref, yb_ref, ma_ref, mb_ref, wa_ref, wb_ref, o_ref):
    ua = jnp.dot(ya_ref[...], wa_ref[...], preferred_element_type=F32)
    ub = jnp.dot(yb_ref[...], wb_ref[...], preferred_element_type=F32)
    o_ref[...] = (_sigmoid(ma_ref[...]) * ua + _sigmoid(mb_ref[...]) * ub).astype(BF16)


def _merge(ya, yb, zc, wa, wb, tm, tn):
    m = ya.shape[0]
    nb = D_MODEL // tn
    return pl.pallas_call(
        _merge_kernel,
        grid=(nb, m // tm),
        in_specs=[
            pl.BlockSpec((tm, A_WIDTH), lambda j, i: (i, 0)),
            pl.BlockSpec((tm, B_GROUP_WIDTH), lambda j, i: (i, 0)),
            pl.BlockSpec((tm, tn), lambda j, i: (i, j)),
            pl.BlockSpec((tm, tn), lambda j, i: (i, nb + j)),
            pl.BlockSpec((A_WIDTH, tn), lambda j, i: (0, j)),
            pl.BlockSpec((B_GROUP_WIDTH, tn), lambda j, i: (0, j)),
        ],
        out_specs=pl.BlockSpec((tm, tn), lambda j, i: (i, j)),
        out_shape=jax.ShapeDtypeStruct((m, D_MODEL), BF16),
        compiler_params=_cparams(("parallel", "arbitrary")),
    )(ya, yb, zc, zc, wa, wb)


def _outproj_kernel(x_ref, a_ref, w_ref, o_ref):
    o_ref[...] = x_ref[...] + jnp.dot(a_ref[...], w_ref[...], preferred_element_type=F32)


def _outproj(x, merged, row_block0, w, tm, tn):
    m = x.shape[0]
    return pl.pallas_call(
        _outproj_kernel,
        grid=(D_MODEL // tn, m // tm),
        in_specs=[
            pl.BlockSpec((tm, tn), lambda j, i: (i, j)),
            pl.BlockSpec((tm, D_MODEL), lambda j, i: (row_block0 + i, 0)),
            pl.BlockSpec((D_MODEL, tn), lambda j, i: (0, j)),
        ],
        out_specs=pl.BlockSpec((tm, tn), lambda j, i: (i, j)),
        out_shape=jax.ShapeDtypeStruct((m, D_MODEL), F32),
        compiler_params=_cparams(("parallel", "arbitrary")),
    )(x, merged, w)


def _ple_kernel(hrow_ref, h_ref, p_ref, wg_ref, wp_ref, o_ref):
    gate = _sigmoid(jnp.dot(hrow_ref[...].astype(BF16), wg_ref[...], preferred_element_type=F32))
    proj = jnp.dot(p_ref[...].astype(BF16), wp_ref[...], preferred_element_type=F32)
    o_ref[...] = h_ref[...] + gate * proj


def _ple(h, p, wg, wp, tm, tn):
    m = h.shape[0]
    pd = p.shape[1]
    return pl.pallas_call(
        _ple_kernel,
        grid=(D_MODEL // tn, m // tm),
        in_specs=[
            pl.BlockSpec((tm, D_MODEL), lambda j, i: (i, 0)),
            pl.BlockSpec((tm, tn), lambda j, i: (i, j)),
            pl.BlockSpec((tm, pd), lambda j, i: (i, 0)),
            pl.BlockSpec((D_MODEL, tn), lambda j, i: (0, j)),
            pl.BlockSpec((pd, tn), lambda j, i: (0, j)),
        ],
        out_specs=pl.BlockSpec((tm, tn), lambda j, i: (i, j)),
        out_shape=jax.ShapeDtypeStruct((m, D_MODEL), F32),
        compiler_params=_cparams(("parallel", "arbitrary")),
    )(h, h, p, wg, wp)


def _rope_tables(pos):
    half = B_HEAD_DIM // 2
    inv = ROPE_THETA ** (-jnp.arange(half, dtype=F32) / half)
    ang = pos.astype(F32)[:, None] * inv[None, :]
    cos, sin = jnp.cos(ang), jnp.sin(ang)
    return jnp.concatenate([cos, cos], axis=1), jnp.concatenate([-sin, sin], axis=1)


def kernel(x_prompt, x_sample, state_shift, state_wkv, cache_kv_w128, cache_kv_w512, cache_kv_w2048,
           p_prompt, p_sample, ln_g, w_in, mu, w0, w2, a0, a2, k_k, k_a, r_k, gn_w, gn_b,
           q_gain, k_gain, w_up_a, w_up_b, w_out, w_ple_gate, w_ple_proj):
    depth = ln_g.shape[0]
    assert depth == 1
    bp, seq, d = x_prompt.shape
    bs, t_new, _ = x_sample.shape
    assert bp == 1 and d == D_MODEL
    rs = bs * t_new
    caches = (cache_kv_w128, cache_kv_w512, cache_kv_w2048)

    xp = x_prompt.reshape(seq, d)
    xs = x_sample.reshape(rs, d)
    hn = _rmsnorm(xp, xs, ln_g)

    w_in0 = w_in[0]
    off_b = ZA_WIDTH
    off_c = ZA_WIDTH + ZB_WIDTH
    wa_in = w_in0[:, :off_b].astype(BF16)
    wb_in = w_in0[:, off_b:off_c].astype(BF16)
    wc_in = jnp.concatenate([w_in0[:, off_c + B_GROUP_WIDTH:], w_in0[:, off_c:off_c + B_GROUP_WIDTH]],
                            axis=1).astype(BF16)
    tm_in = 768
    za = _matmul(hn, wa_in, tm_in, ZA_WIDTH // 3)
    zb = _matmul(hn, wb_in, tm_in, ZB_WIDTH // 3)
    zc = _matmul(hn, wc_in, tm_in, ZC_WIDTH // 3)

    flat = lambda v: v.reshape(1, -1)
    rw = (mu, flat(w0[0]), w2[0], flat(a0[0]), a2[0], flat(k_k[0]), flat(k_a[0]), flat(r_k[0]),
          flat(gn_w[0]), flat(gn_b[0]))
    ya_p, s_p = _rwkv_chunked(za, seq, jnp.zeros((1, 1, A_SHIFT_WIDTH), F32),
                              jnp.zeros((1, A_QUADS, A_HEAD_DIM, QUAD_LANES), F32), *rw)
    ya_s, s_s = _rwkv(za, seq // t_new, bs, t_new, t_new, state_shift[0][:, None, :],
                      _state_to_quads(state_wkv[0]), *rw)
    shift_p = za[seq - 1, :A_SHIFT_WIDTH].reshape(1, 1, A_SHIFT_WIDTH)
    shift_s = za[seq:, :A_SHIFT_WIDTH].reshape(bs, t_new, A_SHIFT_WIDTH)[:, -1][None]
    wkv_p = _keymajor_to_state(s_p)[None]
    wkv_s = _quads_to_state(s_s)[None]

    cos_p, sin_p = _rope_tables(jnp.arange(seq, dtype=jnp.int32))
    cos_s, sin_s = _rope_tables(PAST_LEN + jnp.arange(t_new, dtype=jnp.int32))
    o_p, l_p, o_s, l_s, kv_p, kv_s = [], [], [], [], [], []
    for gi, (window, dil) in enumerate(ATT_GROUPS):
        o, lse, kr = _attn_prompt(zb, gi, dil, seq, cos_p, sin_p, q_gain, k_gain)
        o_p.append(o)
        l_p.append(lse)
        w_keep = min(window, seq)
        v_g = zb[seq - w_keep:seq, 2 * B_QKV_WIDTH + gi * B_GROUP_WIDTH:2 * B_QKV_WIDTH + (gi + 1) * B_GROUP_WIDTH]
        kv_p.append(jnp.stack([kr[seq - w_keep:].reshape(w_keep, HEADS_PER_GROUP, B_HEAD_DIM),
                               v_g.reshape(w_keep, HEADS_PER_GROUP, B_HEAD_DIM)], axis=1)[None, None])
        cache = caches[gi][0]
        o, lse, new_cache = _attn_sample(zb, gi, cache.shape[1], dil, seq // t_new, bs, t_new, cache,
                                         cos_s, sin_s, q_gain, k_gain)
        o_s.append(o)
        l_s.append(lse)
        kv_s.append(new_cache[None])
    yb_p = _combine(o_p, l_p, zc, 0, 1024)
    yb_s = _combine(o_s, l_s, zc, seq // rs, rs)

    ya = jnp.concatenate([ya_p, ya_s], axis=0)
    yb = jnp.concatenate([yb_p, yb_s], axis=0)
    merged = _merge(ya, yb, zc, w_up_a[0].astype(BF16), w_up_b[0].astype(BF16), tm_in, 1024)
    w_out_b = w_out[0].astype(BF16)
    h_p = _outproj(xp, merged, 0, w_out_b, 1024, 1024)
    h_s = _outproj(xs, merged, seq // rs, w_out_b, rs, 1024)
    wg = w_ple_gate[0].astype(BF16)
    wp = w_ple_proj[0].astype(BF16)
    y_p = _ple(h_p, p_prompt[0, 0], wg, wp, 512, 1024)
    y_s = _ple(h_s, p_sample[0].reshape(rs, -1), wg, wp, rs, 1024)

    return (y_p.reshape(bp, seq, d), y_s.reshape(bs, t_new, d),
            shift_p, wkv_p, kv_p[0], kv_p[1], kv_p[2],
            shift_s, wkv_s, kv_s[0], kv_s[1], kv_s[2])
```

```python
import functools

import jax
import jax.numpy as jnp
from jax import lax
from jax.experimental import pallas as pl
from jax.experimental.pallas import tpu as pltpu

F32 = jnp.float32
BF16 = jnp.bfloat16

LANES = 128
SUBLANES = 8
D_MODEL = 2048
A_HEAD_DIM = 64
A_WIDTH = 1024
A_HEADS = 16
QUAD_LANES = 256
A_QUADS = A_WIDTH // QUAD_LANES
CHUNK = A_HEAD_DIM
CHUNKS_PER_STEP = 2
LORA_RANK = 64
A_SHIFT_WIDTH = 3 * A_WIDTH + 2 * LORA_RANK
GN_EPS = 64e-5
NORM_EPS = 1e-6
ATT_GROUPS = ((128, 1), (512, 4), (2048, 16))
N_BACK = 128
HEADS_PER_GROUP = 4
B_HEAD_DIM = 128
B_GROUP_WIDTH = HEADS_PER_GROUP * B_HEAD_DIM
B_QKV_WIDTH = 3 * B_GROUP_WIDTH
ATT_PROMPT_ROWS = 1024
ATT_RESIDUE_GROUP = 4
ROPE_THETA = 10000.0
PAST_LEN = 16384
ZA_WIDTH = A_SHIFT_WIDTH + A_WIDTH
ZB_WIDTH = 3 * B_QKV_WIDTH
ZC_WIDTH = 2 * D_MODEL + B_GROUP_WIDTH
VMEM_LIMIT = 48 * 1024 * 1024


def _cparams(sem):
    return pltpu.CompilerParams(dimension_semantics=sem, vmem_limit_bytes=VMEM_LIMIT)


def _sigmoid(x):
    return 1.0 / (1.0 + jnp.exp(-x))


def _rmsnorm_kernel(xp_ref, xs_ref, g_ref, o_ref, *, n_prompt_tiles):
    def emit(x):
        y = x * lax.rsqrt(jnp.mean(x * x, axis=-1, keepdims=True) + NORM_EPS)
        o_ref[...] = (y * g_ref[...]).astype(BF16)

    i = pl.program_id(0)

    @pl.when(i < n_prompt_tiles)
    def _():
        emit(xp_ref[...])

    @pl.when(i >= n_prompt_tiles)
    def _():
        emit(xs_ref[...])


def _rmsnorm(xp, xs, g):
    rp, d = xp.shape
    rs = xs.shape[0]
    tm = rs
    n_p = rp // tm
    return pl.pallas_call(
        functools.partial(_rmsnorm_kernel, n_prompt_tiles=n_p),
        grid=(n_p + 1,),
        in_specs=[
            pl.BlockSpec((tm, d), lambda i: (jnp.minimum(i, n_p - 1), 0)),
            pl.BlockSpec((tm, d), lambda i: (0, 0)),
            pl.BlockSpec((1, d), lambda i: (0, 0)),
        ],
        out_specs=pl.BlockSpec((tm, d), lambda i: (i, 0)),
        out_shape=jax.ShapeDtypeStruct((rp + rs, d), BF16),
        compiler_params=_cparams(("arbitrary",)),
    )(xp, xs, g)


def _mm_kernel(a_ref, w_ref, o_ref):
    o_ref[...] = jnp.dot(a_ref[...], w_ref[...], preferred_element_type=F32)


def _matmul(a, w, tm, tn):
    m, k = a.shape
    n = w.shape[1]
    return pl.pallas_call(
        _mm_kernel,
        grid=(n // tn, m // tm),
        in_specs=[
            pl.BlockSpec((tm, k), lambda j, i: (i, 0)),
            pl.BlockSpec((k, tn), lambda j, i: (0, j)),
        ],
        out_specs=pl.BlockSpec((tm, tn), lambda j, i: (i, j)),
        out_shape=jax.ShapeDtypeStruct((m, n), F32),
        compiler_params=_cparams(("parallel", "arbitrary")),
    )(a, w)


def _split_dot(x, ones_bf16):
    hi = x.astype(BF16)
    lo = (x - hi.astype(F32)).astype(BF16)
    return (jnp.dot(hi, ones_bf16, preferred_element_type=F32)
            + jnp.dot(lo, ones_bf16, preferred_element_type=F32))


def _rwkv_kernel(z_ref, shift0_ref, s0_ref, mu_ref, w0_ref, w2_ref, a0_ref, a2_ref, kk_ref, ka_ref,
                 rk_ref, gnw_ref, gnb_ref,
                 ya_ref, sT_ref,
                 carry_ref, s_ref, nkk_ref, w_ref, b_ref, k_ref, r_ref, v_ref, y_ref, bonus_ref,
                 *, tc):
    c = pl.program_id(1)
    n_c = pl.num_programs(1)
    s_rows = A_QUADS * A_HEAD_DIM

    @pl.when(c == 0)
    def _():
        carry_ref[...] = shift0_ref[0]
        s_ref[...] = s0_ref[0].reshape(s_rows, QUAD_LANES)

    lane = lax.broadcasted_iota(jnp.int32, (QUAD_LANES, QUAD_LANES), 1)
    row = lax.broadcasted_iota(jnp.int32, (QUAD_LANES, QUAD_LANES), 0)
    head_ones = (lane // A_HEAD_DIM == row // A_HEAD_DIM).astype(BF16)
    diag = (lane % A_HEAD_DIM) == (row % A_HEAD_DIM)

    z = z_ref[:, :A_SHIFT_WIDTH]
    prev = pltpu.roll(z, 1, axis=0)
    first = lax.broadcasted_iota(jnp.int32, (tc, 1), 0) == 0
    prev = jnp.where(first, carry_ref[...], prev)
    carry_ref[...] = z[tc - 1:tc, :]
    zs = z + (prev - z) * mu_ref[...]

    lora = zs[:, 3 * A_WIDTH:]
    w_lo = jnp.tanh(lora[:, :LORA_RANK]).astype(BF16)
    a_lo = lora[:, LORA_RANK:].astype(BF16)
    lw = w0_ref[...] + jnp.dot(w_lo, w2_ref[...].astype(BF16), preferred_element_type=F32)
    nlw = -lw
    log_w = -(jnp.maximum(nlw, 0.0) + jnp.log1p(jnp.exp(-jnp.abs(nlw)))) - 0.5
    w_ref[...] = jnp.exp(-jnp.exp(log_w))
    a = _sigmoid(a0_ref[...] + jnp.dot(a_lo, a2_ref[...].astype(BF16), preferred_element_type=F32))

    for q in range(A_QUADS):
        sl = slice(q * QUAD_LANES, (q + 1) * QUAD_LANES)
        r_q = zs[:, q * QUAD_LANES:(q + 1) * QUAD_LANES]
        k_q = zs[:, A_WIDTH + q * QUAD_LANES:A_WIDTH + (q + 1) * QUAD_LANES]
        v_q = zs[:, 2 * A_WIDTH + q * QUAD_LANES:2 * A_WIDTH + (q + 1) * QUAD_LANES]
        a_q = a[:, sl]
        kk = k_q * kk_ref[:, sl]
        n2 = _split_dot(kk * kk, head_ones)
        kk = kk / jnp.maximum(jnp.sqrt(n2), 1e-12)
        k_mod = k_q * (1.0 + (a_q - 1.0) * ka_ref[:, sl])
        nkk_ref[:, sl] = -kk
        b_ref[:, sl] = kk * a_q
        k_ref[:, sl] = k_mod
        r_ref[:, sl] = r_q
        v_ref[:, sl] = v_q
        bonus_ref[:, sl] = _split_dot(r_q * k_mod * rk_ref[:, sl], head_ones) * v_q

    row8 = lax.broadcasted_iota(jnp.int32, (SUBLANES, QUAD_LANES), 0)

    def expand(tiles, u):
        return jnp.concatenate(
            [jnp.broadcast_to(t[u:u + 1], (A_HEAD_DIM, QUAD_LANES)) for t in tiles], axis=0)

    def group(g, carry):
        rows = pl.ds(pl.multiple_of(g * SUBLANES, SUBLANES), SUBLANES)
        tiles = lambda ref: [ref[rows, q * QUAD_LANES:(q + 1) * QUAD_LANES] for q in range(A_QUADS)]
        nkk8, w8, b8, k8, r8, v8 = (tiles(ref) for ref in (nkk_ref, w_ref, b_ref, k_ref, r_ref, v_ref))
        s = s_ref[...]
        ytiles = [jnp.zeros((SUBLANES, QUAD_LANES), F32) for _ in range(A_QUADS)]
        for u in range(SUBLANES):
            sa = jnp.dot((s * expand(nkk8, u)).astype(BF16), head_ones, preferred_element_type=F32)
            vcol = jnp.dot(jnp.where(diag, expand(v8, u), 0.0).astype(BF16), head_ones,
                           preferred_element_type=F32)
            s = s * expand(w8, u) + sa * expand(b8, u) + vcol * expand(k8, u)
            ycol = jnp.dot((s * expand(r8, u)).astype(BF16), head_ones, preferred_element_type=F32)
            ysel = jnp.where(diag, ycol, 0.0)
            for q in range(A_QUADS):
                yrow = jnp.sum(ysel[q * A_HEAD_DIM:(q + 1) * A_HEAD_DIM], axis=0, keepdims=True)
                ytiles[q] = jnp.where(row8 == u, yrow, ytiles[q])
        s_ref[...] = s
        for q in range(A_QUADS):
            y_ref[rows, q * QUAD_LANES:(q + 1) * QUAD_LANES] = ytiles[q]
        return carry

    lax.fori_loop(0, tc // SUBLANES, group, 0)

    for q in range(A_QUADS):
        sl = slice(q * QUAD_LANES, (q + 1) * QUAD_LANES)
        y = y_ref[:, sl]
        mean = _split_dot(y, head_ones) * (1.0 / A_HEAD_DIM)
        yc = y - mean
        var = _split_dot(yc * yc, head_ones) * (1.0 / A_HEAD_DIM)
        yn = yc * lax.rsqrt(var + GN_EPS) * gnw_ref[:, sl] + gnb_ref[:, sl]
        yn = yn + bonus_ref[:, sl]
        g = z_ref[:, A_SHIFT_WIDTH + q * QUAD_LANES:A_SHIFT_WIDTH + (q + 1) * QUAD_LANES]
        ya_ref[:, sl] = (yn * (g * _sigmoid(g))).astype(BF16)

    @pl.when(c == n_c - 1)
    def _():
        sT_ref[0] = s_ref[...].reshape(A_QUADS, A_HEAD_DIM, QUAD_LANES)


def _rwkv(za, row_block0, n_seq, t_len, tc, shift0, s0, mu, w0, w2, a0, a2, k_k, k_a, r_k, gn_w, gn_b):
    n_c = t_len // tc
    vec = lambda n: pl.BlockSpec((1, n), lambda b, c: (0, 0))
    row_spec = pl.BlockSpec((tc, ZA_WIDTH), lambda b, c: (row_block0 + b * n_c + c, 0))
    state_spec = pl.BlockSpec((1, A_QUADS, A_HEAD_DIM, QUAD_LANES), lambda b, c: (b, 0, 0, 0))
    return pl.pallas_call(
        functools.partial(_rwkv_kernel, tc=tc),
        grid=(n_seq, n_c),
        in_specs=[
            row_spec,
            pl.BlockSpec((1, 1, A_SHIFT_WIDTH), lambda b, c: (b, 0, 0)),
            state_spec,
            vec(A_SHIFT_WIDTH), vec(A_WIDTH),
            pl.BlockSpec((LORA_RANK, A_WIDTH), lambda b, c: (0, 0)),
            vec(A_WIDTH),
            pl.BlockSpec((LORA_RANK, A_WIDTH), lambda b, c: (0, 0)),
            vec(A_WIDTH), vec(A_WIDTH), vec(A_WIDTH), vec(A_WIDTH), vec(A_WIDTH),
        ],
        out_specs=[
            pl.BlockSpec((tc, A_WIDTH), lambda b, c: (b * n_c + c, 0)),
            state_spec,
        ],
        out_shape=[
            jax.ShapeDtypeStruct((n_seq * t_len, A_WIDTH), BF16),
            jax.ShapeDtypeStruct((n_seq, A_QUADS, A_HEAD_DIM, QUAD_LANES), F32),
        ],
        scratch_shapes=[pltpu.VMEM((1, A_SHIFT_WIDTH), F32),
                        pltpu.VMEM((A_QUADS * A_HEAD_DIM, QUAD_LANES), F32)]
        + [pltpu.VMEM((tc, A_WIDTH), F32) for _ in range(8)],
        compiler_params=_cparams(("arbitrary", "arbitrary")),
    )(za, shift0, s0, mu, w0, w2, a0, a2, k_k, k_a, r_k, gn_w, gn_b)


def _state_to_quads(s):
    b = s.shape[0]
    return s.reshape(b, A_QUADS, 4, A_HEAD_DIM, A_HEAD_DIM).transpose(0, 1, 3, 2, 4).reshape(
        b, A_QUADS, A_HEAD_DIM, QUAD_LANES)


def _quads_to_state(s):
    b = s.shape[0]
    return s.reshape(b, A_QUADS, A_HEAD_DIM, 4, A_HEAD_DIM).transpose(0, 1, 3, 2, 4).reshape(
        b, A_HEADS, A_HEAD_DIM, A_HEAD_DIM)


def _head_blocks(y, same_head):
    return jnp.where(same_head, jnp.concatenate([y, y, y, y], axis=0), jnp.zeros((), y.dtype))


def _split2(x):
    hi = x.astype(BF16)
    return hi, (x - hi.astype(F32)).astype(BF16)


def _head_matmul3(x, ys, same_head):
    xh, xl = _split2(x)
    outs = []
    for y in ys:
        yh, yl = _split2(y)
        bh = _head_blocks(yh, same_head)
        out = jnp.dot(xh, bh, preferred_element_type=F32)
        out += jnp.dot(xl, bh, preferred_element_type=F32)
        out += jnp.dot(xh, _head_blocks(yl, same_head), preferred_element_type=F32)
        outs.append(out)
    return outs


def _head_matmul(x, y, same_head):
    return jnp.dot(x.astype(BF16), _head_blocks(y.astype(BF16), same_head), preferred_element_type=F32)


def _rwkv_chunk_kernel(z_ref, shift0_ref, s0_ref, mu_ref, w0_ref, w2_ref, a0_ref, a2_ref, kk_ref, ka_ref,
                       rk_ref, gnw_ref, gnb_ref,
                       ya_ref, sT_ref, carry_ref, st_ref):
    c = pl.program_id(1)
    n_c = pl.num_programs(1)
    tc = CHUNK
    rows_all = tc * CHUNKS_PER_STEP

    @pl.when(c == 0)
    def _():
        carry_ref[...] = shift0_ref[0]
        st_ref[...] = s0_ref[0]

    lane = lax.broadcasted_iota(jnp.int32, (QUAD_LANES, QUAD_LANES), 1)
    row = lax.broadcasted_iota(jnp.int32, (QUAD_LANES, QUAD_LANES), 0)
    same_head = lane // A_HEAD_DIM == row // A_HEAD_DIM
    head_ones = same_head.astype(BF16)
    t_idx = lax.broadcasted_iota(jnp.int32, (tc, QUAD_LANES), 0)
    lane_t = lax.broadcasted_iota(jnp.int32, (tc, QUAD_LANES), 1)
    s_idx = lane_t % A_HEAD_DIM
    strict = s_idx < t_idx
    incl = s_idx <= t_idx
    eye = (s_idx == t_idx).astype(F32)
    lane_head = lane_t // A_HEAD_DIM
    tri = (lax.broadcasted_iota(jnp.int32, (tc, tc), 1) <= lax.broadcasted_iota(jnp.int32, (tc, tc), 0)
           ).astype(BF16)

    z = z_ref[:, :A_SHIFT_WIDTH]
    prev = pltpu.roll(z, 1, axis=0)
    first = lax.broadcasted_iota(jnp.int32, (rows_all, 1), 0) == 0
    prev = jnp.where(first, carry_ref[...], prev)
    carry_ref[...] = z[rows_all - 1:rows_all, :]
    zs = z + (prev - z) * mu_ref[...]

    lora = zs[:, 3 * A_WIDTH:]
    w_lo = jnp.tanh(lora[:, :LORA_RANK]).astype(BF16)
    a_lo = lora[:, LORA_RANK:].astype(BF16)
    lw = w0_ref[...] + jnp.dot(w_lo, w2_ref[...].astype(BF16), preferred_element_type=F32)
    nlw = -lw
    log_w = -(jnp.maximum(nlw, 0.0) + jnp.log1p(jnp.exp(-jnp.abs(nlw)))) - 0.5
    log_decay = -jnp.exp(log_w)
    a = _sigmoid(a0_ref[...] + jnp.dot(a_lo, a2_ref[...].astype(BF16), preferred_element_type=F32))

    quads = range(A_QUADS)
    lanes = [slice(q * QUAD_LANES, (q + 1) * QUAD_LANES) for q in quads]

    def tokens(job):
        sub, q = job
        sl = lanes[q]
        rs = slice(sub * tc, (sub + 1) * tc)
        r_q = zs[rs, q * QUAD_LANES:(q + 1) * QUAD_LANES]
        k_q = zs[rs, A_WIDTH + q * QUAD_LANES:A_WIDTH + (q + 1) * QUAD_LANES]
        v_q = zs[rs, 2 * A_WIDTH + q * QUAD_LANES:2 * A_WIDTH + (q + 1) * QUAD_LANES]
        a_q = a[rs, sl]
        kk = k_q * kk_ref[:, sl]
        n2 = _split_dot(kk * kk, head_ones)
        kk = kk / jnp.maximum(jnp.sqrt(n2), 1e-12)
        k_mod = k_q * (1.0 + (a_q - 1.0) * ka_ref[:, sl])
        beta = kk * a_q
        bonus = _split_dot(r_q * k_mod * rk_ref[:, sl], head_ones) * v_q
        ld = log_decay[rs, sl]
        h1, h2 = _split2(ld)
        h3 = (ld - h1.astype(F32) - h2.astype(F32)).astype(BF16)
        cum = (jnp.dot(tri, h1, preferred_element_type=F32) + jnp.dot(tri, h2, preferred_element_type=F32)
               + jnp.dot(tri, h3, preferred_element_type=F32))
        cum_last = cum[tc - 1:tc, :]
        alpha_p = -kk * jnp.exp(cum - ld)
        r_p = r_q * jnp.exp(cum)
        inv_p = jnp.exp(-cum)
        to_end = jnp.exp(cum_last - cum)
        lhs = jnp.concatenate([alpha_p, r_p], axis=0).astype(BF16)
        g_b = _dot_nt(lhs, _head_blocks((beta * inv_p).astype(BF16), same_head))
        g_k = _dot_nt(lhs, _head_blocks((k_mod * inv_p).astype(BF16), same_head))
        lhs_t = jnp.concatenate([beta * to_end, k_mod * to_end], axis=0).astype(BF16)
        p_col = _split_dot(jnp.where(eye > 0, jnp.exp(cum_last), 0.0), head_ones)
        return dict(v=v_q, bonus=bonus, ar_p=lhs,
                    a_ab=jnp.where(strict, g_b[:tc], 0.0), a_rb=jnp.where(incl, g_b[tc:], 0.0),
                    a_ak=jnp.where(strict, g_k[:tc], 0.0), a_rk=jnp.where(incl, g_k[tc:], 0.0),
                    lhs_t=lhs_t, p_col=p_col)

    jobs = [(sub, q) for sub in range(CHUNKS_PER_STEP) for q in quads]
    n_jobs = range(len(jobs))
    tk = [tokens(job) for job in jobs]

    inv = [eye + t["a_ab"] for t in tk]
    x = [_head_matmul3(t["a_ab"], [t["a_ab"]], same_head)[0] for t in tk]
    for _ in range(4):
        res = [_head_matmul3(x[j], [inv[j], x[j]], same_head) for j in n_jobs]
        inv = [inv[j] + res[j][0] for j in n_jobs]
        x = [res[j][1] for j in n_jobs]
    inv = [inv[j] + _head_matmul3(x[j], [inv[j]], same_head)[0] for j in n_jobs]
    av = [_head_matmul(jnp.concatenate([t["a_ak"], t["a_rk"]], axis=0), t["v"], same_head) for t in tk]
    w1 = [m[:tc] for m in av]
    y_v = [m[tc:] for m in av]

    st = [st_ref[q] for q in quads]
    for sub in range(CHUNKS_PER_STEP):
        js = [sub * A_QUADS + q for q in quads]
        st_blocks = [_head_blocks(s.astype(BF16), same_head) for s in st]
        from_st = [jnp.dot(tk[j]["ar_p"], st_blocks[q], preferred_element_type=F32)
                   for q, j in zip(quads, js)]
        rhs = [w1[j] + from_st[q][:tc] for q, j in zip(quads, js)]
        u = [_head_matmul3(inv[j], [rhs[q]], same_head)[0] for q, j in zip(quads, js)]
        new_st = []
        for q, j in zip(quads, js):
            t = tk[j]
            y = from_st[q][tc:] + _head_matmul(t["a_rb"], u[q], same_head) + y_v[j]
            rhs_t = jnp.concatenate([u[q], t["v"]], axis=0).astype(BF16)
            cross = lax.dot_general(t["lhs_t"], rhs_t, (((0,), (0,)), ((), ())), preferred_element_type=F32)
            new = t["p_col"] * st[q]
            for h in range(QUAD_LANES // A_HEAD_DIM):
                new += jnp.where(lane_head == h, cross[h * A_HEAD_DIM:(h + 1) * A_HEAD_DIM], 0.0)
            new_st.append(new)

            sl = lanes[q]
            rs = slice(sub * tc, (sub + 1) * tc)
            mean = _split_dot(y, head_ones) * (1.0 / A_HEAD_DIM)
            yc = y - mean
            var = _split_dot(yc * yc, head_ones) * (1.0 / A_HEAD_DIM)
            yn = yc * lax.rsqrt(var + GN_EPS) * gnw_ref[:, sl] + gnb_ref[:, sl]
            yn = yn + t["bonus"]
            g = z_ref[rs, A_SHIFT_WIDTH + q * QUAD_LANES:A_SHIFT_WIDTH + (q + 1) * QUAD_LANES]
            ya_ref[rs, sl] = (yn * (g * _sigmoid(g))).astype(BF16)
        st = new_st
    for q in quads:
        st_ref[q] = st[q]

    @pl.when(c == n_c - 1)
    def _():
        sT_ref[0] = st_ref[...]


def _rwkv_chunked(za, t_len, shift0, s0, mu, w0, w2, a0, a2, k_k, k_a, r_k, gn_w, gn_b):
    rows = CHUNK * CHUNKS_PER_STEP
    n_c = t_len // rows
    vec = lambda n: pl.BlockSpec((1, n), lambda b, c: (0, 0))
    state_spec = pl.BlockSpec((1, A_QUADS, A_HEAD_DIM, QUAD_LANES), lambda b, c: (b, 0, 0, 0))
    return pl.pallas_call(
        _rwkv_chunk_kernel,
        grid=(1, n_c),
        in_specs=[
            pl.BlockSpec((rows, ZA_WIDTH), lambda b, c: (c, 0)),
            pl.BlockSpec((1, 1, A_SHIFT_WIDTH), lambda b, c: (b, 0, 0)),
            state_spec,
            vec(A_SHIFT_WIDTH), vec(A_WIDTH),
            pl.BlockSpec((LORA_RANK, A_WIDTH), lambda b, c: (0, 0)),
            vec(A_WIDTH),
            pl.BlockSpec((LORA_RANK, A_WIDTH), lambda b, c: (0, 0)),
            vec(A_WIDTH), vec(A_WIDTH), vec(A_WIDTH), vec(A_WIDTH), vec(A_WIDTH),
        ],
        out_specs=[
            pl.BlockSpec((rows, A_WIDTH), lambda b, c: (c, 0)),
            state_spec,
        ],
        out_shape=[
            jax.ShapeDtypeStruct((t_len, A_WIDTH), BF16),
            jax.ShapeDtypeStruct((1, A_QUADS, A_HEAD_DIM, QUAD_LANES), F32),
        ],
        scratch_shapes=[pltpu.VMEM((1, A_SHIFT_WIDTH), F32),
                        pltpu.VMEM((A_QUADS, A_HEAD_DIM, QUAD_LANES), F32)],
        compiler_params=_cparams(("arbitrary", "arbitrary")),
    )(za, shift0, s0, mu, w0, w2, a0, a2, k_k, k_a, r_k, gn_w, gn_b)


def _keymajor_to_state(s):
    b = s.shape[0]
    return s.reshape(b, A_QUADS, A_HEAD_DIM, 4, A_HEAD_DIM).transpose(0, 1, 3, 4, 2).reshape(
        b, A_HEADS, A_HEAD_DIM, A_HEAD_DIM)


def _norm_rope(x, gain, cos, sin):
    y = x * lax.rsqrt(jnp.mean(x * x, axis=-1, keepdims=True) + NORM_EPS) * gain
    return y * cos + pltpu.roll(y, B_HEAD_DIM // 2, axis=1) * sin


def _dot_nt(a, b):
    return lax.dot_general(a, b, (((1,), (1,)), ((), ())), preferred_element_type=F32)


def _attn_prompt_kernel(q_ref, k_ref, v_ref, cos_ref, sin_ref, qg_ref, kg_ref,
                        o_ref, lse_ref, kr_ref, kprev_ref, vprev_ref, qs_ref, *, dil, n_sub):
    i = pl.program_id(0)
    hd = pl.program_id(1)
    blk = N_BACK

    @pl.when(i == 0)
    def _():
        kprev_ref[hd] = jnp.zeros((dil, blk, B_HEAD_DIM), BF16)
        vprev_ref[hd] = jnp.zeros((dil, blk, B_HEAD_DIM), BF16)

    cos = cos_ref[...]
    sin = sin_ref[...]
    qs_ref[...] = _norm_rope(q_ref[...], qg_ref[...], cos, sin)
    kr_ref[...] = _norm_rope(k_ref[...], kg_ref[...], cos, sin)

    rowi = lax.broadcasted_iota(jnp.int32, (blk, blk), 0)
    coli = lax.broadcasted_iota(jnp.int32, (blk, blk), 1)
    cur_ok = coli <= rowi
    scale = B_HEAD_DIM ** -0.5
    span = lambda sub, rho: pl.ds(sub * blk * dil + rho, blk, stride=dil)
    units = [(sub, rho) for sub in range(n_sub) for rho in range(dil)]
    for u0 in range(0, len(units), ATT_RESIDUE_GROUP):
        group = units[u0:u0 + ATT_RESIDUE_GROUP]
        rows = [span(sub, rho) for sub, rho in group]
        qn = [qs_ref[r, :].astype(BF16) for r in rows]
        kb = [kr_ref[r, :].astype(BF16) for r in rows]
        vb = [v_ref[r, :].astype(BF16) for r in rows]
        kp = [kr_ref[span(sub - 1, rho), :].astype(BF16) if sub > 0 else kprev_ref[hd, rho] for sub, rho in group]
        vp = [v_ref[span(sub - 1, rho), :].astype(BF16) if sub > 0 else vprev_ref[hd, rho] for sub, rho in group]
        prev_ok = [coli >= rowi if sub > 0 else jnp.logical_and(coli >= rowi, i > 0) for sub, _ in group]
        s_cur = [jnp.where(cur_ok, _dot_nt(q, k) * scale, -jnp.inf) for q, k in zip(qn, kb)]
        s_prev = [jnp.where(ok, _dot_nt(q, k) * scale, -jnp.inf) for ok, q, k in zip(prev_ok, qn, kp)]
        m = [jnp.maximum(jnp.max(c, axis=-1, keepdims=True), jnp.max(p, axis=-1, keepdims=True))
             for c, p in zip(s_cur, s_prev)]
        p_cur = [jnp.exp(c - mm) for c, mm in zip(s_cur, m)]
        p_prev = [jnp.exp(p - mm) for p, mm in zip(s_prev, m)]
        den = [jnp.sum(c, axis=-1, keepdims=True) + jnp.sum(p, axis=-1, keepdims=True)
               for c, p in zip(p_cur, p_prev)]
        for n, r in enumerate(rows):
            o = jnp.dot((p_cur[n] / den[n]).astype(BF16), vb[n], preferred_element_type=F32)
            o += jnp.dot((p_prev[n] / den[n]).astype(BF16), vp[n], preferred_element_type=F32)
            o_ref[r, :] = o
            lse_ref[r, :] = jnp.broadcast_to(m[n] + jnp.log(den[n]), (blk, B_HEAD_DIM))
    for rho in range(dil):
        kprev_ref[hd, rho] = kr_ref[span(n_sub - 1, rho), :].astype(BF16)
        vprev_ref[hd, rho] = v_ref[span(n_sub - 1, rho), :].astype(BF16)


def _attn_prompt(zb, gi, dil, seq, cos, sin, q_gain, k_gain):
    n_sub = max(1, ATT_PROMPT_ROWS // (N_BACK * dil))
    tr = N_BACK * dil * n_sub
    heads_qkv = B_QKV_WIDTH // B_HEAD_DIM
    blk = lambda which: pl.BlockSpec(
        (tr, B_HEAD_DIM), lambda i, hd: (i, which * heads_qkv + gi * HEADS_PER_GROUP + hd))
    tab = pl.BlockSpec((tr, B_HEAD_DIM), lambda i, hd: (i, 0))
    gain = pl.BlockSpec((1, B_HEAD_DIM), lambda i, hd: (0, 0))
    out = pl.BlockSpec((tr, B_HEAD_DIM), lambda i, hd: (i, hd))
    shp = jax.ShapeDtypeStruct((seq, B_GROUP_WIDTH), F32)
    prev = pltpu.VMEM((HEADS_PER_GROUP, dil, N_BACK, B_HEAD_DIM), BF16)
    return pl.pallas_call(
        functools.partial(_attn_prompt_kernel, dil=dil, n_sub=n_sub),
        grid=(seq // tr, HEADS_PER_GROUP),
        in_specs=[blk(0), blk(1), blk(2), tab, tab, gain, gain],
        out_specs=[out, out, out],
        out_shape=[shp, shp, shp],
        scratch_shapes=[prev, prev, pltpu.VMEM((tr, B_HEAD_DIM), F32)],
        compiler_params=_cparams(("arbitrary", "arbitrary")),
    )(zb, zb, zb, cos, sin, q_gain, k_gain)


def _attn_sample_kernel(q_ref, k_ref, v_ref, cache_ref, cos_ref, sin_ref, qg_ref, kg_ref,
                        o_ref, lse_ref, new_ref, *, window, dil, t_new):
    per_pos = 2 * HEADS_PER_GROUP
    new_ref[0, 0:(window - t_new) * per_pos, :] = cache_ref[0, t_new * per_pos:window * per_pos, :]
    cos = cos_ref[...]
    sin = sin_ref[...]
    scale = B_HEAD_DIM ** -0.5
    tq = lax.broadcasted_iota(jnp.int32, (t_new, window), 0)
    cc = lax.broadcasted_iota(jnp.int32, (t_new, window), 1)
    cache_ok = jnp.logical_and(((cc - tq + dil * N_BACK) & (dil - 1)) == 0, cc >= tq)
    tn = lax.broadcasted_iota(jnp.int32, (t_new, t_new), 0)
    un = lax.broadcasted_iota(jnp.int32, (t_new, t_new), 1)
    new_ok = jnp.logical_and(((tn - un + dil * N_BACK) & (dil - 1)) == 0, un <= tn)
    for h in range(HEADS_PER_GROUP):
        sl = slice(h * B_HEAD_DIM, (h + 1) * B_HEAD_DIM)
        qn = _norm_rope(q_ref[:, sl], qg_ref[...], cos, sin).astype(BF16)
        kr = _norm_rope(k_ref[:, sl], kg_ref[...], cos, sin)
        v_new = v_ref[:, sl]
        new_ref[0, pl.ds((window - t_new) * per_pos + h, t_new, stride=per_pos), :] = kr
        new_ref[0, pl.ds((window - t_new) * per_pos + HEADS_PER_GROUP + h, t_new, stride=per_pos), :] = v_new
        kc = cache_ref[0, pl.ds(h, window, stride=per_pos), :].astype(BF16)
        vc = cache_ref[0, pl.ds(HEADS_PER_GROUP + h, window, stride=per_pos), :].astype(BF16)
        s_c = jnp.where(cache_ok, _dot_nt(qn, kc) * scale, -jnp.inf)
        s_n = jnp.where(new_ok, _dot_nt(qn, kr.astype(BF16)) * scale, -jnp.inf)
        m = jnp.maximum(jnp.max(s_c, axis=-1, keepdims=True), jnp.max(s_n, axis=-1, keepdims=True))
        p_c = jnp.exp(s_c - m)
        p_n = jnp.exp(s_n - m)
        den = jnp.sum(p_c, axis=-1, keepdims=True) + jnp.sum(p_n, axis=-1, keepdims=True)
        o = jnp.dot((p_c / den).astype(BF16), vc, preferred_element_type=F32)
        o += jnp.dot((p_n / den).astype(BF16), v_new.astype(BF16), preferred_element_type=F32)
        o_ref[:, sl] = o
        lse_ref[:, sl] = jnp.broadcast_to(m + jnp.log(den), (t_new, B_HEAD_DIM))


def _attn_sample(zb, gi, window, dil, row_block0, n_seq, t_new, cache, cos, sin, q_gain, k_gain):
    rows = window * 2 * HEADS_PER_GROUP
    cache2 = cache.reshape(n_seq, rows, B_HEAD_DIM)
    blk = lambda off: pl.BlockSpec((t_new, B_GROUP_WIDTH), lambda b: (row_block0 + b, off + gi))
    tab = pl.BlockSpec((t_new, B_HEAD_DIM), lambda b: (0, 0))
    gain = pl.BlockSpec((1, B_HEAD_DIM), lambda b: (0, 0))
    out = pl.BlockSpec((t_new, B_GROUP_WIDTH), lambda b: (b, 0))
    win = pl.BlockSpec((1, rows, B_HEAD_DIM), lambda b: (b, 0, 0))
    shp = jax.ShapeDtypeStruct((n_seq * t_new, B_GROUP_WIDTH), F32)
    o, lse, new = pl.pallas_call(
        functools.partial(_attn_sample_kernel, window=window, dil=dil, t_new=t_new),
        grid=(n_seq,),
        in_specs=[blk(0), blk(3), blk(6), win, tab, tab, gain, gain],
        out_specs=[out, out, win],
        out_shape=[shp, shp, jax.ShapeDtypeStruct((n_seq, rows, B_HEAD_DIM), F32)],
        compiler_params=_cparams(("arbitrary",)),
    )(zb, zb, zb, cache2, cos, sin, q_gain, k_gain)
    return o, lse, new.reshape(cache.shape)


def _combine_kernel(o0, o1, o2, l0, l1, l2, g_ref, y_ref):
    a, b, c = l0[...], l1[...], l2[...]
    m = jnp.maximum(jnp.maximum(a, b), c)
    ea, eb, ec = jnp.exp(a - m), jnp.exp(b - m), jnp.exp(c - m)
    o = (ea * o0[...] + eb * o1[...] + ec * o2[...]) / (ea + eb + ec)
    g = g_ref[...]
    y_ref[...] = (o * (g * _sigmoid(g))).astype(BF16)


def _combine(outs, lses, zc, row_block0, tm):
    rows = outs[0].shape[0]
    spec = pl.BlockSpec((tm, B_GROUP_WIDTH), lambda i: (i, 0))
    gate_block = 2 * D_MODEL // B_GROUP_WIDTH
    return pl.pallas_call(
        _combine_kernel,
        grid=(rows // tm,),
        in_specs=[spec] * 6 + [pl.BlockSpec((tm, B_GROUP_WIDTH), lambda i: (row_block0 + i, gate_block))],
        out_specs=spec,
        out_shape=jax.ShapeDtypeStruct((rows, B_GROUP_WIDTH), BF16),
        compiler_params=_cparams(("parallel",)),
    )(*outs, *lses, zc)


def _merge_kernel(ya_ref, yb_ref, ma_ref, mb_ref, wa_ref, wb_ref, o_ref):
    ua = jnp.dot(ya_ref[...], wa_ref[...], preferred_element_type=F32)
    ub = jnp.dot(yb_ref[...], wb_ref[...], preferred_element_type=F32)
    o_ref[...] = (_sigmoid(ma_ref[...]) * ua + _sigmoid(mb_ref[...]) * ub).astype(BF16)


def _merge(ya, yb, zc, row_block0, wa, wb, tm, tn):
    m = ya.shape[0]
    nb = D_MODEL // tn
    return pl.pallas_call(
        _merge_kernel,
        grid=(nb, m // tm),
        in_specs=[
            pl.BlockSpec((tm, A_WIDTH), lambda j, i: (i, 0)),
            pl.BlockSpec((tm, B_GROUP_WIDTH), lambda j, i: (i, 0)),
            pl.BlockSpec((tm, tn), lambda j, i: (row_block0 + i, j)),
            pl.BlockSpec((tm, tn), lambda j, i: (row_block0 + i, nb + j)),
            pl.BlockSpec((A_WIDTH, tn), lambda j, i: (0, j)),
            pl.BlockSpec((B_GROUP_WIDTH, tn), lambda j, i: (0, j)),
        ],
        out_specs=pl.BlockSpec((tm, tn), lambda j, i: (i, j)),
        out_shape=jax.ShapeDtypeStruct((m, D_MODEL), BF16),
        compiler_params=_cparams(("parallel", "arbitrary")),
    )(ya, yb, zc, zc, wa, wb)


def _outproj_kernel(x_ref, a_ref, w_ref, o_ref):
    o_ref[...] = x_ref[...] + jnp.dot(a_ref[...], w_ref[...], preferred_element_type=F32)


def _outproj(x, merged, w, tm, tn):
    m = x.shape[0]
    return pl.pallas_call(
        _outproj_kernel,
        grid=(D_MODEL // tn, m // tm),
        in_specs=[
            pl.BlockSpec((tm, tn), lambda j, i: (i, j)),
            pl.BlockSpec((tm, D_MODEL), lambda j, i: (i, 0)),
            pl.BlockSpec((D_MODEL, tn), lambda j, i: (0, j)),
        ],
        out_specs=pl.BlockSpec((tm, tn), lambda j, i: (i, j)),
        out_shape=jax.ShapeDtypeStruct((m, D_MODEL), F32),
        compiler_params=_cparams(("parallel", "arbitrary")),
    )(x, merged, w)


def _ple_kernel(hrow_ref, h_ref, p_ref, wg_ref, wp_ref, o_ref):
    gate = _sigmoid(jnp.dot(hrow_ref[...].astype(BF16), wg_ref[...], preferred_element_type=F32))
    proj = jnp.dot(p_ref[...].astype(BF16), wp_ref[...], preferred_element_type=F32)
    o_ref[...] = h_ref[...] + gate * proj


def _ple(h, p, wg, wp, tm, tn):
    m = h.shape[0]
    pd = p.shape[1]
    return pl.pallas_call(
        _ple_kernel,
        grid=(D_MODEL // tn, m // tm),
        in_specs=[
            pl.BlockSpec((tm, D_MODEL), lambda j, i: (i, 0)),
            pl.BlockSpec((tm, tn), lambda j, i: (i, j)),
            pl.BlockSpec((tm, pd), lambda j, i: (i, 0)),
            pl.BlockSpec((D_MODEL, tn), lambda j, i: (0, j)),
            pl.BlockSpec((pd, tn), lambda j, i: (0, j)),
        ],
        out_specs=pl.BlockSpec((tm, tn), lambda j, i: (i, j)),
        out_shape=jax.ShapeDtypeStruct((m, D_MODEL), F32),
        compiler_params=_cparams(("parallel", "arbitrary")),
    )(h, h, p, wg, wp)


def _rope_tables(pos):
    half = B_HEAD_DIM // 2
    inv = ROPE_THETA ** (-jnp.arange(half, dtype=F32) / half)
    ang = pos.astype(F32)[:, None] * inv[None, :]
    cos, sin = jnp.cos(ang), jnp.sin(ang)
    return jnp.concatenate([cos, cos], axis=1), jnp.concatenate([-sin, sin], axis=1)


def kernel(x_prompt, x_sample, state_shift, state_wkv, cache_kv_w128, cache_kv_w512, cache_kv_w2048,
           p_prompt, p_sample, ln_g, w_in, mu, w0, w2, a0, a2, k_k, k_a, r_k, gn_w, gn_b,
           q_gain, k_gain, w_up_a, w_up_b, w_out, w_ple_gate, w_ple_proj):
    depth = ln_g.shape[0]
    assert depth == 1
    bp, seq, d = x_prompt.shape
    bs, t_new, _ = x_sample.shape
    assert bp == 1 and d == D_MODEL
    rs = bs * t_new
    caches = (cache_kv_w128, cache_kv_w512, cache_kv_w2048)

    xp = x_prompt.reshape(seq, d)
    xs = x_sample.reshape(rs, d)
    hn = _rmsnorm(xp, xs, ln_g)

    w_in0 = w_in[0]
    off_b = ZA_WIDTH
    off_c = ZA_WIDTH + ZB_WIDTH
    wa_in = w_in0[:, :off_b].astype(BF16)
    wb_in = w_in0[:, off_b:off_c].astype(BF16)
    wc_in = jnp.concatenate([w_in0[:, off_c + B_GROUP_WIDTH:], w_in0[:, off_c:off_c + B_GROUP_WIDTH]],
                            axis=1).astype(BF16)
    tm_in = 768
    za = _matmul(hn, wa_in, tm_in, ZA_WIDTH // 3)
    zb = _matmul(hn, wb_in, tm_in, ZB_WIDTH // 3)
    zc = _matmul(hn, wc_in, tm_in, ZC_WIDTH // 3)

    flat = lambda v: v.reshape(1, -1)
    rw = (mu, flat(w0[0]), w2[0], flat(a0[0]), a2[0], flat(k_k[0]), flat(k_a[0]), flat(r_k[0]),
          flat(gn_w[0]), flat(gn_b[0]))
    ya_p, s_p = _rwkv_chunked(za, seq, jnp.zeros((1, 1, A_SHIFT_WIDTH), F32),
                              jnp.zeros((1, A_QUADS, A_HEAD_DIM, QUAD_LANES), F32), *rw)
    ya_s, s_s = _rwkv(za, seq // t_new, bs, t_new, t_new, state_shift[0][:, None, :],
                      _state_to_quads(state_wkv[0]), *rw)
    shift_p = za[seq - 1, :A_SHIFT_WIDTH].reshape(1, 1, A_SHIFT_WIDTH)
    shift_s = za[seq:, :A_SHIFT_WIDTH].reshape(bs, t_new, A_SHIFT_WIDTH)[:, -1][None]
    wkv_p = _keymajor_to_state(s_p)[None]
    wkv_s = _quads_to_state(s_s)[None]

    cos_p, sin_p = _rope_tables(jnp.arange(seq, dtype=jnp.int32))
    cos_s, sin_s = _rope_tables(PAST_LEN + jnp.arange(t_new, dtype=jnp.int32))
    o_p, l_p, o_s, l_s, kv_p, kv_s = [], [], [], [], [], []
    for gi, (window, dil) in enumerate(ATT_GROUPS):
        o, lse, kr = _attn_prompt(zb, gi, dil, seq, cos_p, sin_p, q_gain, k_gain)
        o_p.append(o)
        l_p.append(lse)
        w_keep = min(window, seq)
        v_g = zb[seq - w_keep:seq, 2 * B_QKV_WIDTH + gi * B_GROUP_WIDTH:2 * B_QKV_WIDTH + (gi + 1) * B_GROUP_WIDTH]
        kv_p.append(jnp.stack([kr[seq - w_keep:].reshape(w_keep, HEADS_PER_GROUP, B_HEAD_DIM),
                               v_g.reshape(w_keep, HEADS_PER_GROUP, B_HEAD_DIM)], axis=1)[None, None])
        cache = caches[gi][0]
        o, lse, new_cache = _attn_sample(zb, gi, cache.shape[1], dil, seq // t_new, bs, t_new, cache,
                                         cos_s, sin_s, q_gain, k_gain)
        o_s.append(o)
        l_s.append(lse)
        kv_s.append(new_cache[None])
    yb_p = _combine(o_p, l_p, zc, 0, 1024)
    yb_s = _combine(o_s, l_s, zc, seq // rs, rs)

    wa_up = w_up_a[0].astype(BF16)
    wb_up = w_up_b[0].astype(BF16)
    merged_p = _merge(ya_p, yb_p, zc, 0, wa_up, wb_up, 1024, 1024)
    merged_s = _merge(ya_s, yb_s, zc, seq // rs, wa_up, wb_up, rs, 1024)
    w_out_b = w_out[0].astype(BF16)
    h_p = _outproj(xp, merged_p, w_out_b, 1024, 1024)
    h_s = _outproj(xs, merged_s, w_out_b, rs, 1024)
    wg = w_ple_gate[0].astype(BF16)
    wp = w_ple_proj[0].astype(BF16)
    y_p = _ple(h_p, p_prompt[0, 0], wg, wp, 512, 1024)
    y_s = _ple(h_s, p_sample[0].reshape(rs, -1), wg, wp, rs, 1024)

    return (y_p.reshape(bp, seq, d), y_s.reshape(bs, t_new, d),
            shift_p, wkv_p, kv_p[0], kv_p[1], kv_p[2],
            shift_s, wkv_s, kv_s[0], kv_s[1], kv_s[2])
```

```python
import functools

import jax
import jax.numpy as jnp
from jax import lax
from jax.experimental import pallas as pl
from jax.experimental.pallas import tpu as pltpu

F32 = jnp.float32
BF16 = jnp.bfloat16

LANES = 128
SUBLANES = 8
D_MODEL = 2048
A_HEAD_DIM = 64
A_WIDTH = 1024
A_HEADS = 16
QUAD_LANES = 256
A_QUADS = A_WIDTH // QUAD_LANES
CHUNK = A_HEAD_DIM
CHUNKS_PER_STEP = 2
LORA_RANK = 64
A_SHIFT_WIDTH = 3 * A_WIDTH + 2 * LORA_RANK
GN_EPS = 64e-5
NORM_EPS = 1e-6
ATT_GROUPS = ((128, 1), (512, 4), (2048, 16))
N_BACK = 128
HEADS_PER_GROUP = 4
B_HEAD_DIM = 128
B_GROUP_WIDTH = HEADS_PER_GROUP * B_HEAD_DIM
B_QKV_WIDTH = 3 * B_GROUP_WIDTH
ATT_PROMPT_ROWS = 1024
ATT_RESIDUE_GROUP = 4
ROPE_THETA = 10000.0
PAST_LEN = 16384
ZA_WIDTH = A_SHIFT_WIDTH + A_WIDTH
ZB_WIDTH = 3 * B_QKV_WIDTH
ZC_WIDTH = 2 * D_MODEL + B_GROUP_WIDTH
VMEM_LIMIT = 48 * 1024 * 1024


def _cparams(sem):
    return pltpu.CompilerParams(dimension_semantics=sem, vmem_limit_bytes=VMEM_LIMIT)


def _sigmoid(x):
    return 1.0 / (1.0 + jnp.exp(-x))


def _rmsnorm_kernel(xp_ref, xs_ref, g_ref, o_ref, *, n_prompt_tiles):
    def emit(x):
        y = x * lax.rsqrt(jnp.mean(x * x, axis=-1, keepdims=True) + NORM_EPS)
        o_ref[...] = (y * g_ref[...]).astype(BF16)

    i = pl.program_id(0)

    @pl.when(i < n_prompt_tiles)
    def _():
        emit(xp_ref[...])

    @pl.when(i >= n_prompt_tiles)
    def _():
        emit(xs_ref[...])


def _rmsnorm(xp, xs, g):
    rp, d = xp.shape
    rs = xs.shape[0]
    tm = rs
    n_p = rp // tm
    return pl.pallas_call(
        functools.partial(_rmsnorm_kernel, n_prompt_tiles=n_p),
        grid=(n_p + 1,),
        in_specs=[
            pl.BlockSpec((tm, d), lambda i: (jnp.minimum(i, n_p - 1), 0)),
            pl.BlockSpec((tm, d), lambda i: (0, 0)),
            pl.BlockSpec((1, d), lambda i: (0, 0)),
        ],
        out_specs=pl.BlockSpec((tm, d), lambda i: (i, 0)),
        out_shape=jax.ShapeDtypeStruct((rp + rs, d), BF16),
        compiler_params=_cparams(("arbitrary",)),
    )(xp, xs, g)


def _mm_kernel(a_ref, w_ref, o_ref):
    o_ref[...] = jnp.dot(a_ref[...], w_ref[...], preferred_element_type=F32)


def _matmul(a, w, tm, tn):
    m, k = a.shape
    n = w.shape[1]
    return pl.pallas_call(
        _mm_kernel,
        grid=(n // tn, m // tm),
        in_specs=[
            pl.BlockSpec((tm, k), lambda j, i: (i, 0)),
            pl.BlockSpec((k, tn), lambda j, i: (0, j)),
        ],
        out_specs=pl.BlockSpec((tm, tn), lambda j, i: (i, j)),
        out_shape=jax.ShapeDtypeStruct((m, n), F32),
        compiler_params=_cparams(("parallel", "arbitrary")),
    )(a, w)


def _split_dot(x, ones_bf16):
    hi = x.astype(BF16)
    lo = (x - hi.astype(F32)).astype(BF16)
    return (jnp.dot(hi, ones_bf16, preferred_element_type=F32)
            + jnp.dot(lo, ones_bf16, preferred_element_type=F32))


def _rwkv_kernel(z_ref, shift0_ref, s0_ref, mu_ref, w0_ref, w2_ref, a0_ref, a2_ref, kk_ref, ka_ref,
                 rk_ref, gnw_ref, gnb_ref,
                 ya_ref, sT_ref,
                 carry_ref, s_ref, nkk_ref, w_ref, b_ref, k_ref, r_ref, v_ref, y_ref, bonus_ref,
                 *, tc):
    c = pl.program_id(1)
    n_c = pl.num_programs(1)
    s_rows = A_QUADS * A_HEAD_DIM

    @pl.when(c == 0)
    def _():
        carry_ref[...] = shift0_ref[0]
        heads_per_quad = QUAD_LANES // A_HEAD_DIM
        for q in range(A_QUADS):
            s_ref[q * A_HEAD_DIM:(q + 1) * A_HEAD_DIM, :] = jnp.concatenate(
                [s0_ref[0, q * heads_per_quad + h] for h in range(heads_per_quad)], axis=1)

    lane = lax.broadcasted_iota(jnp.int32, (QUAD_LANES, QUAD_LANES), 1)
    row = lax.broadcasted_iota(jnp.int32, (QUAD_LANES, QUAD_LANES), 0)
    head_ones = (lane // A_HEAD_DIM == row // A_HEAD_DIM).astype(BF16)
    diag = (lane % A_HEAD_DIM) == (row % A_HEAD_DIM)

    z = z_ref[:, :A_SHIFT_WIDTH]
    prev = pltpu.roll(z, 1, axis=0)
    first = lax.broadcasted_iota(jnp.int32, (tc, 1), 0) == 0
    prev = jnp.where(first, carry_ref[...], prev)
    carry_ref[...] = z[tc - 1:tc, :]
    zs = z + (prev - z) * mu_ref[...]

    lora = zs[:, 3 * A_WIDTH:]
    w_lo = jnp.tanh(lora[:, :LORA_RANK]).astype(BF16)
    a_lo = lora[:, LORA_RANK:].astype(BF16)
    lw = w0_ref[...] + jnp.dot(w_lo, w2_ref[...].astype(BF16), preferred_element_type=F32)
    nlw = -lw
    log_w = -(jnp.maximum(nlw, 0.0) + jnp.log1p(jnp.exp(-jnp.abs(nlw)))) - 0.5
    w_ref[...] = jnp.exp(-jnp.exp(log_w))
    a = _sigmoid(a0_ref[...] + jnp.dot(a_lo, a2_ref[...].astype(BF16), preferred_element_type=F32))

    for q in range(A_QUADS):
        sl = slice(q * QUAD_LANES, (q + 1) * QUAD_LANES)
        r_q = zs[:, q * QUAD_LANES:(q + 1) * QUAD_LANES]
        k_q = zs[:, A_WIDTH + q * QUAD_LANES:A_WIDTH + (q + 1) * QUAD_LANES]
        v_q = zs[:, 2 * A_WIDTH + q * QUAD_LANES:2 * A_WIDTH + (q + 1) * QUAD_LANES]
        a_q = a[:, sl]
        kk = k_q * kk_ref[:, sl]
        n2 = _split_dot(kk * kk, head_ones)
        kk = kk / jnp.maximum(jnp.sqrt(n2), 1e-12)
        k_mod = k_q * (1.0 + (a_q - 1.0) * ka_ref[:, sl])
        nkk_ref[:, sl] = -kk
        b_ref[:, sl] = kk * a_q
        k_ref[:, sl] = k_mod
        r_ref[:, sl] = r_q
        v_ref[:, sl] = v_q
        bonus_ref[:, sl] = _split_dot(r_q * k_mod * rk_ref[:, sl], head_ones) * v_q

    row8 = lax.broadcasted_iota(jnp.int32, (SUBLANES, QUAD_LANES), 0)

    def expand(tiles, u):
        return jnp.concatenate(
            [jnp.broadcast_to(t[u:u + 1], (A_HEAD_DIM, QUAD_LANES)) for t in tiles], axis=0)

    def group(g, carry):
        rows = pl.ds(pl.multiple_of(g * SUBLANES, SUBLANES), SUBLANES)
        tiles = lambda ref: [ref[rows, q * QUAD_LANES:(q + 1) * QUAD_LANES] for q in range(A_QUADS)]
        nkk8, w8, b8, k8, r8, v8 = (tiles(ref) for ref in (nkk_ref, w_ref, b_ref, k_ref, r_ref, v_ref))
        s = s_ref[...]
        ytiles = [jnp.zeros((SUBLANES, QUAD_LANES), F32) for _ in range(A_QUADS)]
        for u in range(SUBLANES):
            sa = jnp.dot((s * expand(nkk8, u)).astype(BF16), head_ones, preferred_element_type=F32)
            vcol = jnp.dot(jnp.where(diag, expand(v8, u), 0.0).astype(BF16), head_ones,
                           preferred_element_type=F32)
            s = s * expand(w8, u) + sa * expand(b8, u) + vcol * expand(k8, u)
            ycol = jnp.dot((s * expand(r8, u)).astype(BF16), head_ones, preferred_element_type=F32)
            ysel = jnp.where(diag, ycol, 0.0)
            for q in range(A_QUADS):
                yrow = jnp.sum(ysel[q * A_HEAD_DIM:(q + 1) * A_HEAD_DIM], axis=0, keepdims=True)
                ytiles[q] = jnp.where(row8 == u, yrow, ytiles[q])
        s_ref[...] = s
        for q in range(A_QUADS):
            y_ref[rows, q * QUAD_LANES:(q + 1) * QUAD_LANES] = ytiles[q]
        return carry

    lax.fori_loop(0, tc // SUBLANES, group, 0)

    for q in range(A_QUADS):
        sl = slice(q * QUAD_LANES, (q + 1) * QUAD_LANES)
        y = y_ref[:, sl]
        mean = _split_dot(y, head_ones) * (1.0 / A_HEAD_DIM)
        yc = y - mean
        var = _split_dot(yc * yc, head_ones) * (1.0 / A_HEAD_DIM)
        yn = yc * lax.rsqrt(var + GN_EPS) * gnw_ref[:, sl] + gnb_ref[:, sl]
        yn = yn + bonus_ref[:, sl]
        g = z_ref[:, A_SHIFT_WIDTH + q * QUAD_LANES:A_SHIFT_WIDTH + (q + 1) * QUAD_LANES]
        ya_ref[:, sl] = (yn * (g * _sigmoid(g))).astype(BF16)

    @pl.when(c == n_c - 1)
    def _():
        heads_per_quad = QUAD_LANES // A_HEAD_DIM
        for hd in range(A_HEADS):
            q, h = divmod(hd, heads_per_quad)
            sT_ref[0, hd] = s_ref[q * A_HEAD_DIM:(q + 1) * A_HEAD_DIM, h * A_HEAD_DIM:(h + 1) * A_HEAD_DIM]


def _rwkv(za, row_block0, n_seq, t_len, tc, shift0, s0, mu, w0, w2, a0, a2, k_k, k_a, r_k, gn_w, gn_b):
    n_c = t_len // tc
    vec = lambda n: pl.BlockSpec((1, n), lambda b, c: (0, 0))
    row_spec = pl.BlockSpec((tc, ZA_WIDTH), lambda b, c: (row_block0 + b * n_c + c, 0))
    state_spec = pl.BlockSpec((1, A_HEADS, A_HEAD_DIM, A_HEAD_DIM), lambda b, c: (b, 0, 0, 0))
    return pl.pallas_call(
        functools.partial(_rwkv_kernel, tc=tc),
        grid=(n_seq, n_c),
        in_specs=[
            row_spec,
            pl.BlockSpec((1, 1, A_SHIFT_WIDTH), lambda b, c: (b, 0, 0)),
            state_spec,
            vec(A_SHIFT_WIDTH), vec(A_WIDTH),
            pl.BlockSpec((LORA_RANK, A_WIDTH), lambda b, c: (0, 0)),
            vec(A_WIDTH),
            pl.BlockSpec((LORA_RANK, A_WIDTH), lambda b, c: (0, 0)),
            vec(A_WIDTH), vec(A_WIDTH), vec(A_WIDTH), vec(A_WIDTH), vec(A_WIDTH),
        ],
        out_specs=[
            pl.BlockSpec((tc, A_WIDTH), lambda b, c: (b * n_c + c, 0)),
            state_spec,
        ],
        out_shape=[
            jax.ShapeDtypeStruct((n_seq * t_len, A_WIDTH), BF16),
            jax.ShapeDtypeStruct((n_seq, A_HEADS, A_HEAD_DIM, A_HEAD_DIM), F32),
        ],
        scratch_shapes=[pltpu.VMEM((1, A_SHIFT_WIDTH), F32),
                        pltpu.VMEM((A_QUADS * A_HEAD_DIM, QUAD_LANES), F32)]
        + [pltpu.VMEM((tc, A_WIDTH), F32) for _ in range(8)],
        compiler_params=_cparams(("arbitrary", "arbitrary")),
    )(za, shift0, s0, mu, w0, w2, a0, a2, k_k, k_a, r_k, gn_w, gn_b)


def _head_blocks(y, same_head):
    return jnp.where(same_head, jnp.concatenate([y, y, y, y], axis=0), jnp.zeros((), y.dtype))


def _off_block(t, s, b):
    return jnp.logical_and(t // (2 * b) == s // (2 * b), jnp.logical_and(t % (2 * b) >= b, s % (2 * b) < b))


def _split2(x):
    hi = x.astype(BF16)
    return hi, (x - hi.astype(F32)).astype(BF16)


def _head_matmul3(x, ys, same_head):
    xh, xl = _split2(x)
    outs = []
    for y in ys:
        yh, yl = _split2(y)
        bh = _head_blocks(yh, same_head)
        out = jnp.dot(xh, bh, preferred_element_type=F32)
        out += jnp.dot(xl, bh, preferred_element_type=F32)
        out += jnp.dot(xh, _head_blocks(yl, same_head), preferred_element_type=F32)
        outs.append(out)
    return outs


def _head_matmul(x, y, same_head):
    return jnp.dot(x.astype(BF16), _head_blocks(y.astype(BF16), same_head), preferred_element_type=F32)


def _rwkv_chunk_kernel(z_ref, shift0_ref, s0_ref, mu_ref, w0_ref, w2_ref, a0_ref, a2_ref, kk_ref, ka_ref,
                       rk_ref, gnw_ref, gnb_ref,
                       ya_ref, sT_ref, carry_ref, st_ref):
    c = pl.program_id(1)
    n_c = pl.num_programs(1)
    tc = CHUNK
    rows_all = tc * CHUNKS_PER_STEP

    @pl.when(c == 0)
    def _():
        carry_ref[...] = shift0_ref[0]
        st_ref[...] = s0_ref[0]

    lane = lax.broadcasted_iota(jnp.int32, (QUAD_LANES, QUAD_LANES), 1)
    row = lax.broadcasted_iota(jnp.int32, (QUAD_LANES, QUAD_LANES), 0)
    same_head = lane // A_HEAD_DIM == row // A_HEAD_DIM
    head_ones = same_head.astype(BF16)
    t_idx = lax.broadcasted_iota(jnp.int32, (tc, QUAD_LANES), 0)
    lane_t = lax.broadcasted_iota(jnp.int32, (tc, QUAD_LANES), 1)
    s_idx = lane_t % A_HEAD_DIM
    strict = s_idx < t_idx
    incl = s_idx <= t_idx
    eye = (s_idx == t_idx).astype(F32)
    lane_head = lane_t // A_HEAD_DIM
    tri = (lax.broadcasted_iota(jnp.int32, (tc, tc), 1) <= lax.broadcasted_iota(jnp.int32, (tc, tc), 0)
           ).astype(BF16)

    z = z_ref[:, :A_SHIFT_WIDTH]
    prev = pltpu.roll(z, 1, axis=0)
    first = lax.broadcasted_iota(jnp.int32, (rows_all, 1), 0) == 0
    prev = jnp.where(first, carry_ref[...], prev)
    carry_ref[...] = z[rows_all - 1:rows_all, :]
    zs = z + (prev - z) * mu_ref[...]

    lora = zs[:, 3 * A_WIDTH:]
    w_lo = jnp.tanh(lora[:, :LORA_RANK]).astype(BF16)
    a_lo = lora[:, LORA_RANK:].astype(BF16)
    lw = w0_ref[...] + jnp.dot(w_lo, w2_ref[...].astype(BF16), preferred_element_type=F32)
    nlw = -lw
    log_w = -(jnp.maximum(nlw, 0.0) + jnp.log1p(jnp.exp(-jnp.abs(nlw)))) - 0.5
    log_decay = -jnp.exp(log_w)
    a = _sigmoid(a0_ref[...] + jnp.dot(a_lo, a2_ref[...].astype(BF16), preferred_element_type=F32))

    quads = range(A_QUADS)
    lanes = [slice(q * QUAD_LANES, (q + 1) * QUAD_LANES) for q in quads]

    def tokens(job):
        sub, q = job
        sl = lanes[q]
        rs = slice(sub * tc, (sub + 1) * tc)
        r_q = zs[rs, q * QUAD_LANES:(q + 1) * QUAD_LANES]
        k_q = zs[rs, A_WIDTH + q * QUAD_LANES:A_WIDTH + (q + 1) * QUAD_LANES]
        v_q = zs[rs, 2 * A_WIDTH + q * QUAD_LANES:2 * A_WIDTH + (q + 1) * QUAD_LANES]
        a_q = a[rs, sl]
        kk = k_q * kk_ref[:, sl]
        n2 = _split_dot(kk * kk, head_ones)
        kk = kk / jnp.maximum(jnp.sqrt(n2), 1e-12)
        k_mod = k_q * (1.0 + (a_q - 1.0) * ka_ref[:, sl])
        beta = kk * a_q
        bonus = _split_dot(r_q * k_mod * rk_ref[:, sl], head_ones) * v_q
        ld = log_decay[rs, sl]
        h1, h2 = _split2(ld)
        h3 = (ld - h1.astype(F32) - h2.astype(F32)).astype(BF16)
        cum = (jnp.dot(tri, h1, preferred_element_type=F32) + jnp.dot(tri, h2, preferred_element_type=F32)
               + jnp.dot(tri, h3, preferred_element_type=F32))
        cum_last = cum[tc - 1:tc, :]
        alpha_p = -kk * jnp.exp(cum - ld)
        r_p = r_q * jnp.exp(cum)
        inv_p = jnp.exp(-cum)
        to_end = jnp.exp(cum_last - cum)
        lhs = jnp.concatenate([alpha_p, r_p], axis=0).astype(BF16)
        g_b = _dot_nt(lhs, _head_blocks((beta * inv_p).astype(BF16), same_head))
        g_k = _dot_nt(lhs, _head_blocks((k_mod * inv_p).astype(BF16), same_head))
        lhs_t = jnp.concatenate([beta * to_end, k_mod * to_end], axis=0).astype(BF16)
        p_col = _split_dot(jnp.where(eye > 0, jnp.exp(cum_last), 0.0), head_ones)
        return dict(v=v_q, bonus=bonus, ar_p=lhs,
                    a_ab=jnp.where(strict, g_b[:tc], 0.0), a_rb=jnp.where(incl, g_b[tc:], 0.0),
                    a_ak=jnp.where(strict, g_k[:tc], 0.0), a_rk=jnp.where(incl, g_k[tc:], 0.0),
                    lhs_t=lhs_t, p_col=p_col)

    jobs = [(sub, q) for sub in range(CHUNKS_PER_STEP) for q in quads]
    n_jobs = range(len(jobs))
    tk = [tokens(job) for job in jobs]

    inv = [eye + jnp.where(_off_block(t_idx, s_idx, 1), t["a_ab"], 0.0) for t in tk]
    b = 2
    while b < tc:
        off = _off_block(t_idx, s_idx, b)
        mid = [_head_matmul3(jnp.where(off, tk[j]["a_ab"], 0.0), [inv[j]], same_head)[0] for j in n_jobs]
        inv = [inv[j] + _head_matmul3(inv[j], [mid[j]], same_head)[0] for j in n_jobs]
        b *= 2
    av = [_head_matmul(jnp.concatenate([t["a_ak"], t["a_rk"]], axis=0), t["v"], same_head) for t in tk]
    w1 = [m[:tc] for m in av]
    y_v = [m[tc:] for m in av]

    st = [st_ref[q] for q in quads]
    for sub in range(CHUNKS_PER_STEP):
        js = [sub * A_QUADS + q for q in quads]
        st_blocks = [_head_blocks(s.astype(BF16), same_head) for s in st]
        from_st = [jnp.dot(tk[j]["ar_p"], st_blocks[q], preferred_element_type=F32)
                   for q, j in zip(quads, js)]
        rhs = [w1[j] + from_st[q][:tc] for q, j in zip(quads, js)]
        u = [_head_matmul3(inv[j], [rhs[q]], same_head)[0] for q, j in zip(quads, js)]
        new_st = []
        for q, j in zip(quads, js):
            t = tk[j]
            y = from_st[q][tc:] + _head_matmul(t["a_rb"], u[q], same_head) + y_v[j]
            rhs_t = jnp.concatenate([u[q], t["v"]], axis=0).astype(BF16)
            cross = lax.dot_general(t["lhs_t"], rhs_t, (((0,), (0,)), ((), ())), preferred_element_type=F32)
            new = t["p_col"] * st[q]
            for h in range(QUAD_LANES // A_HEAD_DIM):
                new += jnp.where(lane_head == h, cross[h * A_HEAD_DIM:(h + 1) * A_HEAD_DIM], 0.0)
            new_st.append(new)

            sl = lanes[q]
            rs = slice(sub * tc, (sub + 1) * tc)
            mean = _split_dot(y, head_ones) * (1.0 / A_HEAD_DIM)
            yc = y - mean
            var = _split_dot(yc * yc, head_ones) * (1.0 / A_HEAD_DIM)
            yn = yc * lax.rsqrt(var + GN_EPS) * gnw_ref[:, sl] + gnb_ref[:, sl]
            yn = yn + t["bonus"]
            g = z_ref[rs, A_SHIFT_WIDTH + q * QUAD_LANES:A_SHIFT_WIDTH + (q + 1) * QUAD_LANES]
            ya_ref[rs, sl] = (yn * (g * _sigmoid(g))).astype(BF16)
        st = new_st
    for q in quads:
        st_ref[q] = st[q]

    @pl.when(c == n_c - 1)
    def _():
        sT_ref[0] = st_ref[...]


def _rwkv_chunked(za, t_len, shift0, s0, mu, w0, w2, a0, a2, k_k, k_a, r_k, gn_w, gn_b):
    rows = CHUNK * CHUNKS_PER_STEP
    n_c = t_len // rows
    vec = lambda n: pl.BlockSpec((1, n), lambda b, c: (0, 0))
    state_spec = pl.BlockSpec((1, A_QUADS, A_HEAD_DIM, QUAD_LANES), lambda b, c: (b, 0, 0, 0))
    return pl.pallas_call(
        _rwkv_chunk_kernel,
        grid=(1, n_c),
        in_specs=[
            pl.BlockSpec((rows, ZA_WIDTH), lambda b, c: (c, 0)),
            pl.BlockSpec((1, 1, A_SHIFT_WIDTH), lambda b, c: (b, 0, 0)),
            state_spec,
            vec(A_SHIFT_WIDTH), vec(A_WIDTH),
            pl.BlockSpec((LORA_RANK, A_WIDTH), lambda b, c: (0, 0)),
            vec(A_WIDTH),
            pl.BlockSpec((LORA_RANK, A_WIDTH), lambda b, c: (0, 0)),
            vec(A_WIDTH), vec(A_WIDTH), vec(A_WIDTH), vec(A_WIDTH), vec(A_WIDTH),
        ],
        out_specs=[
            pl.BlockSpec((rows, A_WIDTH), lambda b, c: (c, 0)),
            state_spec,
        ],
        out_shape=[
            jax.ShapeDtypeStruct((t_len, A_WIDTH), BF16),
            jax.ShapeDtypeStruct((1, A_QUADS, A_HEAD_DIM, QUAD_LANES), F32),
        ],
        scratch_shapes=[pltpu.VMEM((1, A_SHIFT_WIDTH), F32),
                        pltpu.VMEM((A_QUADS, A_HEAD_DIM, QUAD_LANES), F32)],
        compiler_params=_cparams(("arbitrary", "arbitrary")),
    )(za, shift0, s0, mu, w0, w2, a0, a2, k_k, k_a, r_k, gn_w, gn_b)


def _keymajor_to_state(s):
    b = s.shape[0]
    return s.reshape(b, A_QUADS, A_HEAD_DIM, 4, A_HEAD_DIM).transpose(0, 1, 3, 4, 2).reshape(
        b, A_HEADS, A_HEAD_DIM, A_HEAD_DIM)


def _norm_rope(x, gain, cos, sin):
    y = x * lax.rsqrt(jnp.mean(x * x, axis=-1, keepdims=True) + NORM_EPS) * gain
    return y * cos + pltpu.roll(y, B_HEAD_DIM // 2, axis=1) * sin


def _dot_nt(a, b):
    return lax.dot_general(a, b, (((1,), (1,)), ((), ())), preferred_element_type=F32)


def _attn_prompt_kernel(q_ref, k_ref, v_ref, cos_ref, sin_ref, qg_ref, kg_ref,
                        o_ref, lse_ref, kr_ref, kprev_ref, vprev_ref, qs_ref, *, dil, n_sub):
    i = pl.program_id(0)
    hd = pl.program_id(1)
    blk = N_BACK

    @pl.when(i == 0)
    def _():
        kprev_ref[hd] = jnp.zeros((dil, blk, B_HEAD_DIM), BF16)
        vprev_ref[hd] = jnp.zeros((dil, blk, B_HEAD_DIM), BF16)

    cos = cos_ref[...]
    sin = sin_ref[...]
    qs_ref[...] = _norm_rope(q_ref[...], qg_ref[...], cos, sin)
    kr_ref[...] = _norm_rope(k_ref[...], kg_ref[...], cos, sin)

    rowi = lax.broadcasted_iota(jnp.int32, (blk, blk), 0)
    coli = lax.broadcasted_iota(jnp.int32, (blk, blk), 1)
    cur_ok = coli <= rowi
    scale = B_HEAD_DIM ** -0.5
    span = lambda sub, rho: pl.ds(sub * blk * dil + rho, blk, stride=dil)
    units = [(sub, rho) for sub in range(n_sub) for rho in range(dil)]
    for u0 in range(0, len(units), ATT_RESIDUE_GROUP):
        group = units[u0:u0 + ATT_RESIDUE_GROUP]
        rows = [span(sub, rho) for sub, rho in group]
        qn = [qs_ref[r, :].astype(BF16) for r in rows]
        kb = [kr_ref[r, :].astype(BF16) for r in rows]
        vb = [v_ref[r, :].astype(BF16) for r in rows]
        kp = [kr_ref[span(sub - 1, rho), :].astype(BF16) if sub > 0 else kprev_ref[hd, rho] for sub, rho in group]
        vp = [v_ref[span(sub - 1, rho), :].astype(BF16) if sub > 0 else vprev_ref[hd, rho] for sub, rho in group]
        prev_ok = [coli >= rowi if sub > 0 else jnp.logical_and(coli >= rowi, i > 0) for sub, _ in group]
        s_cur = [jnp.where(cur_ok, _dot_nt(q, k) * scale, -jnp.inf) for q, k in zip(qn, kb)]
        s_prev = [jnp.where(ok, _dot_nt(q, k) * scale, -jnp.inf) for ok, q, k in zip(prev_ok, qn, kp)]
        m = [jnp.maximum(jnp.max(c, axis=-1, keepdims=True), jnp.max(p, axis=-1, keepdims=True))
             for c, p in zip(s_cur, s_prev)]
        p_cur = [jnp.exp(c - mm) for c, mm in zip(s_cur, m)]
        p_prev = [jnp.exp(p - mm) for p, mm in zip(s_prev, m)]
        den = [jnp.sum(c, axis=-1, keepdims=True) + jnp.sum(p, axis=-1, keepdims=True)
               for c, p in zip(p_cur, p_prev)]
        for n, r in enumerate(rows):
            o = jnp.dot((p_cur[n] / den[n]).astype(BF16), vb[n], preferred_element_type=F32)
            o += jnp.dot((p_prev[n] / den[n]).astype(BF16), vp[n], preferred_element_type=F32)
            o_ref[r, :] = o
            lse_ref[r, :] = jnp.broadcast_to(m[n] + jnp.log(den[n]), (blk, B_HEAD_DIM))
    for rho in range(dil):
        kprev_ref[hd, rho] = kr_ref[span(n_sub - 1, rho), :].astype(BF16)
        vprev_ref[hd, rho] = v_ref[span(n_sub - 1, rho), :].astype(BF16)


def _attn_prompt(zb, gi, dil, seq, cos, sin, q_gain, k_gain):
    n_sub = max(1, ATT_PROMPT_ROWS // (N_BACK * dil))
    tr = N_BACK * dil * n_sub
    heads_qkv = B_QKV_WIDTH // B_HEAD_DIM
    blk = lambda which: pl.BlockSpec(
        (tr, B_HEAD_DIM), lambda i, hd: (i, which * heads_qkv + gi * HEADS_PER_GROUP + hd))
    tab = pl.BlockSpec((tr, B_HEAD_DIM), lambda i, hd: (i, 0))
    gain = pl.BlockSpec((1, B_HEAD_DIM), lambda i, hd: (0, 0))
    out = pl.BlockSpec((tr, B_HEAD_DIM), lambda i, hd: (i, hd))
    shp = jax.ShapeDtypeStruct((seq, B_GROUP_WIDTH), F32)
    prev = pltpu.VMEM((HEADS_PER_GROUP, dil, N_BACK, B_HEAD_DIM), BF16)
    return pl.pallas_call(
        functools.partial(_attn_prompt_kernel, dil=dil, n_sub=n_sub),
        grid=(seq // tr, HEADS_PER_GROUP),
        in_specs=[blk(0), blk(1), blk(2), tab, tab, gain, gain],
        out_specs=[out, out, out],
        out_shape=[shp, shp, shp],
        scratch_shapes=[prev, prev, pltpu.VMEM((tr, B_HEAD_DIM), F32)],
        compiler_params=_cparams(("arbitrary", "arbitrary")),
    )(zb, zb, zb, cos, sin, q_gain, k_gain)


def _attn_sample_kernel(q_ref, k_ref, v_ref, cache_ref, cos_ref, sin_ref, qg_ref, kg_ref,
                        o_ref, lse_ref, new_ref, *, window, dil, t_new):
    per_pos = 2 * HEADS_PER_GROUP
    new_ref[0, 0:(window - t_new) * per_pos, :] = cache_ref[0, t_new * per_pos:window * per_pos, :]
    cos = cos_ref[...]
    sin = sin_ref[...]
    scale = B_HEAD_DIM ** -0.5
    tq = lax.broadcasted_iota(jnp.int32, (t_new, window), 0)
    cc = lax.broadcasted_iota(jnp.int32, (t_new, window), 1)
    cache_ok = jnp.logical_and(((cc - tq + dil * N_BACK) & (dil - 1)) == 0, cc >= tq)
    tn = lax.broadcasted_iota(jnp.int32, (t_new, t_new), 0)
    un = lax.broadcasted_iota(jnp.int32, (t_new, t_new), 1)
    new_ok = jnp.logical_and(((tn - un + dil * N_BACK) & (dil - 1)) == 0, un <= tn)
    for h in range(HEADS_PER_GROUP):
        sl = slice(h * B_HEAD_DIM, (h + 1) * B_HEAD_DIM)
        qn = _norm_rope(q_ref[:, sl], qg_ref[...], cos, sin).astype(BF16)
        kr = _norm_rope(k_ref[:, sl], kg_ref[...], cos, sin)
        v_new = v_ref[:, sl]
        new_ref[0, pl.ds((window - t_new) * per_pos + h, t_new, stride=per_pos), :] = kr
        new_ref[0, pl.ds((window - t_new) * per_pos + HEADS_PER_GROUP + h, t_new, stride=per_pos), :] = v_new
        kc = cache_ref[0, pl.ds(h, window, stride=per_pos), :].astype(BF16)
        vc = cache_ref[0, pl.ds(HEADS_PER_GROUP + h, window, stride=per_pos), :].astype(BF16)
        s_c = jnp.where(cache_ok, _dot_nt(qn, kc) * scale, -jnp.inf)
        s_n = jnp.where(new_ok, _dot_nt(qn, kr.astype(BF16)) * scale, -jnp.inf)
        m = jnp.maximum(jnp.max(s_c, axis=-1, keepdims=True), jnp.max(s_n, axis=-1, keepdims=True))
        p_c = jnp.exp(s_c - m)
        p_n = jnp.exp(s_n - m)
        den = jnp.sum(p_c, axis=-1, keepdims=True) + jnp.sum(p_n, axis=-1, keepdims=True)
        o = jnp.dot((p_c / den).astype(BF16), vc, preferred_element_type=F32)
        o += jnp.dot((p_n / den).astype(BF16), v_new.astype(BF16), preferred_element_type=F32)
        o_ref[:, sl] = o
        lse_ref[:, sl] = jnp.broadcast_to(m + jnp.log(den), (t_new, B_HEAD_DIM))


def _attn_sample(zb, gi, window, dil, row_block0, n_seq, t_new, cache, cos, sin, q_gain, k_gain):
    rows = window * 2 * HEADS_PER_GROUP
    cache2 = cache.reshape(n_seq, rows, B_HEAD_DIM)
    blk = lambda off: pl.BlockSpec((t_new, B_GROUP_WIDTH), lambda b: (row_block0 + b, off + gi))
    tab = pl.BlockSpec((t_new, B_HEAD_DIM), lambda b: (0, 0))
    gain = pl.BlockSpec((1, B_HEAD_DIM), lambda b: (0, 0))
    out = pl.BlockSpec((t_new, B_GROUP_WIDTH), lambda b: (b, 0))
    win = pl.BlockSpec((1, rows, B_HEAD_DIM), lambda b: (b, 0, 0))
    shp = jax.ShapeDtypeStruct((n_seq * t_new, B_GROUP_WIDTH), F32)
    o, lse, new = pl.pallas_call(
        functools.partial(_attn_sample_kernel, window=window, dil=dil, t_new=t_new),
        grid=(n_seq,),
        in_specs=[blk(0), blk(3), blk(6), win, tab, tab, gain, gain],
        out_specs=[out, out, win],
        out_shape=[shp, shp, jax.ShapeDtypeStruct((n_seq, rows, B_HEAD_DIM), F32)],
        compiler_params=_cparams(("arbitrary",)),
    )(zb, zb, zb, cache2, cos, sin, q_gain, k_gain)
    return o, lse, new.reshape(cache.shape)


def _combine_kernel(o0, o1, o2, l0, l1, l2, g_ref, y_ref):
    a, b, c = l0[...], l1[...], l2[...]
    m = jnp.maximum(jnp.maximum(a, b), c)
    ea, eb, ec = jnp.exp(a - m), jnp.exp(b - m), jnp.exp(c - m)
    o = (ea * o0[...] + eb * o1[...] + ec * o2[...]) / (ea + eb + ec)
    g = g_ref[...]
    y_ref[...] = (o * (g * _sigmoid(g))).astype(BF16)


def _combine(outs, lses, zc, row_block0, tm):
    rows = outs[0].shape[0]
    spec = pl.BlockSpec((tm, B_GROUP_WIDTH), lambda i: (i, 0))
    return pl.pallas_call(
        _combine_kernel,
        grid=(rows // tm,),
        in_specs=[spec] * 6 + [pl.BlockSpec((tm, B_GROUP_WIDTH), lambda i: (row_block0 + i, 0))],
        out_specs=spec,
        out_shape=jax.ShapeDtypeStruct((rows, B_GROUP_WIDTH), BF16),
        compiler_params=_cparams(("parallel",)),
    )(*outs, *lses, zc)


def _merge_kernel(ya_ref, yb_ref, ma_ref, mb_ref, wa_ref, wb_ref, o_ref):
    ua = jnp.dot(ya_ref[...], wa_ref[...], preferred_element_type=F32)
    ub = jnp.dot(yb_ref[...], wb_ref[...], preferred_element_type=F32)
    o_ref[...] = (_sigmoid(ma_ref[...]) * ua + _sigmoid(mb_ref[...]) * ub).astype(BF16)


def _merge(ya, yb, zc, row_block0, wa, wb, tm, tn):
    m = ya.shape[0]
    nb = D_MODEL // tn
    off_a = B_GROUP_WIDTH // tn
    off_b = (B_GROUP_WIDTH + D_MODEL) // tn
    assert off_a * tn == B_GROUP_WIDTH
    return pl.pallas_call(
        _merge_kernel,
        grid=(m // tm, nb),
        in_specs=[
            pl.BlockSpec((tm, A_WIDTH), lambda i, j: (i, 0)),
            pl.BlockSpec((tm, B_GROUP_WIDTH), lambda i, j: (i, 0)),
            pl.BlockSpec((tm, tn), lambda i, j: (row_block0 + i, off_a + j)),
            pl.BlockSpec((tm, tn), lambda i, j: (row_block0 + i, off_b + j)),
            pl.BlockSpec((A_WIDTH, tn), lambda i, j: (0, j)),
            pl.BlockSpec((B_GROUP_WIDTH, tn), lambda i, j: (0, j)),
        ],
        out_specs=pl.BlockSpec((tm, tn), lambda i, j: (i, j)),
        out_shape=jax.ShapeDtypeStruct((m, D_MODEL), BF16),
        compiler_params=_cparams(("parallel", "arbitrary")),
    )(ya, yb, zc, zc, wa, wb)


def _outproj_kernel(x_ref, a_ref, w_ref, o_ref):
    o_ref[...] = x_ref[...] + jnp.dot(a_ref[...], w_ref[...], preferred_element_type=F32)


def _outproj(x, merged, w, tm, tn):
    m = x.shape[0]
    return pl.pallas_call(
        _outproj_kernel,
        grid=(D_MODEL // tn, m // tm),
        in_specs=[
            pl.BlockSpec((tm, tn), lambda j, i: (i, j)),
            pl.BlockSpec((tm, D_MODEL), lambda j, i: (i, 0)),
            pl.BlockSpec((D_MODEL, tn), lambda j, i: (0, j)),
        ],
        out_specs=pl.BlockSpec((tm, tn), lambda j, i: (i, j)),
        out_shape=jax.ShapeDtypeStruct((m, D_MODEL), F32),
        compiler_params=_cparams(("parallel", "arbitrary")),
    )(x, merged, w)


def _ple_kernel(h_ref, p_ref, wg_ref, wp_ref, o_ref):
    h = h_ref[...]
    gate = _sigmoid(jnp.dot(h.astype(BF16), wg_ref[...], preferred_element_type=F32))
    proj = jnp.dot(p_ref[...].astype(BF16), wp_ref[...], preferred_element_type=F32)
    o_ref[...] = h + gate * proj


def _ple(h, p, wg, wp, tm):
    m = h.shape[0]
    pd = p.shape[1]
    return pl.pallas_call(
        _ple_kernel,
        grid=(m // tm,),
        in_specs=[
            pl.BlockSpec((tm, D_MODEL), lambda i: (i, 0)),
            pl.BlockSpec((tm, pd), lambda i: (i, 0)),
            pl.BlockSpec((D_MODEL, D_MODEL), lambda i: (0, 0)),
            pl.BlockSpec((pd, D_MODEL), lambda i: (0, 0)),
        ],
        out_specs=pl.BlockSpec((tm, D_MODEL), lambda i: (i, 0)),
        out_shape=jax.ShapeDtypeStruct((m, D_MODEL), F32),
        compiler_params=_cparams(("parallel",)),
    )(h, p, wg, wp)


def _rope_tables(pos):
    half = B_HEAD_DIM // 2
    inv = ROPE_THETA ** (-jnp.arange(half, dtype=F32) / half)
    ang = pos.astype(F32)[:, None] * inv[None, :]
    cos, sin = jnp.cos(ang), jnp.sin(ang)
    return jnp.concatenate([cos, cos], axis=1), jnp.concatenate([-sin, sin], axis=1)


def kernel(x_prompt, x_sample, state_shift, state_wkv, cache_kv_w128, cache_kv_w512, cache_kv_w2048,
           p_prompt, p_sample, ln_g, w_in, mu, w0, w2, a0, a2, k_k, k_a, r_k, gn_w, gn_b,
           q_gain, k_gain, w_up_a, w_up_b, w_out, w_ple_gate, w_ple_proj):
    depth = ln_g.shape[0]
    assert depth == 1
    bp, seq, d = x_prompt.shape
    bs, t_new, _ = x_sample.shape
    assert bp == 1 and d == D_MODEL
    rs = bs * t_new
    caches = (cache_kv_w128, cache_kv_w512, cache_kv_w2048)

    xp = x_prompt.reshape(seq, d)
    xs = x_sample.reshape(rs, d)
    hn = _rmsnorm(xp, xs, ln_g)

    w_in0 = w_in[0]
    off_b = ZA_WIDTH
    off_c = ZA_WIDTH + ZB_WIDTH
    wa_in = w_in0[:, :off_b].astype(BF16)
    wb_in = w_in0[:, off_b:off_c].astype(BF16)
    wc_in = w_in0[:, off_c:].astype(BF16)
    tm_in = 768
    za = _matmul(hn, wa_in, tm_in, ZA_WIDTH // 3)
    zb = _matmul(hn, wb_in, tm_in, ZB_WIDTH // 3)
    zc = _matmul(hn, wc_in, tm_in, ZC_WIDTH // 3)

    flat = lambda v: v.reshape(1, -1)
    rw = (mu, flat(w0[0]), w2[0], flat(a0[0]), a2[0], flat(k_k[0]), flat(k_a[0]), flat(r_k[0]),
          flat(gn_w[0]), flat(gn_b[0]))
    ya_p, s_p = _rwkv_chunked(za, seq, jnp.zeros((1, 1, A_SHIFT_WIDTH), F32),
                              jnp.zeros((1, A_QUADS, A_HEAD_DIM, QUAD_LANES), F32), *rw)
    ya_s, s_s = _rwkv(za, seq // t_new, bs, t_new, t_new, state_shift[0][:, None, :],
                      state_wkv[0], *rw)
    shift_p = za[seq - 1, :A_SHIFT_WIDTH].reshape(1, 1, A_SHIFT_WIDTH)
    shift_s = za[seq:, :A_SHIFT_WIDTH].reshape(bs, t_new, A_SHIFT_WIDTH)[:, -1][None]
    wkv_p = _keymajor_to_state(s_p)[None]
    wkv_s = s_s[None]

    cos_p, sin_p = _rope_tables(jnp.arange(seq, dtype=jnp.int32))
    cos_s, sin_s = _rope_tables(PAST_LEN + jnp.arange(t_new, dtype=jnp.int32))
    o_p, l_p, o_s, l_s, kv_p, kv_s = [], [], [], [], [], []
    for gi, (window, dil) in enumerate(ATT_GROUPS):
        o, lse, kr = _attn_prompt(zb, gi, dil, seq, cos_p, sin_p, q_gain, k_gain)
        o_p.append(o)
        l_p.append(lse)
        w_keep = min(window, seq)
        v_g = zb[seq - w_keep:seq, 2 * B_QKV_WIDTH + gi * B_GROUP_WIDTH:2 * B_QKV_WIDTH + (gi + 1) * B_GROUP_WIDTH]
        kv_p.append(jnp.stack([kr[seq - w_keep:].reshape(w_keep, HEADS_PER_GROUP, B_HEAD_DIM),
                               v_g.reshape(w_keep, HEADS_PER_GROUP, B_HEAD_DIM)], axis=1)[None, None])
        cache = caches[gi][0]
        o, lse, new_cache = _attn_sample(zb, gi, cache.shape[1], dil, seq // t_new, bs, t_new, cache,
                                         cos_s, sin_s, q_gain, k_gain)
        o_s.append(o)
        l_s.append(lse)
        kv_s.append(new_cache[None])
    yb_p = _combine(o_p, l_p, zc, 0, 1024)
    yb_s = _combine(o_s, l_s, zc, seq // rs, rs)

    wa_up = w_up_a[0].astype(BF16)
    wb_up = w_up_b[0].astype(BF16)
    merged_p = _merge(ya_p, yb_p, zc, 0, wa_up, wb_up, 2048, B_GROUP_WIDTH)
    merged_s = _merge(ya_s, yb_s, zc, seq // rs, wa_up, wb_up, rs, B_GROUP_WIDTH)
    w_out_b = w_out[0].astype(BF16)
    h_p = _outproj(xp, merged_p, w_out_b, 512, D_MODEL)
    h_s = _outproj(xs, merged_s, w_out_b, rs, D_MODEL)
    wg = w_ple_gate[0].astype(BF16)
    wp = w_ple_proj[0].astype(BF16)
    y_p = _ple(h_p, p_prompt[0, 0], wg, wp, 512)
    y_s = _ple(h_s, p_sample[0].reshape(rs, -1), wg, wp, rs)

    return (y_p.reshape(bp, seq, d), y_s.reshape(bs, t_new, d),
            shift_p, wkv_p, kv_p[0], kv_p[1], kv_p[2],
            shift_s, wkv_s, kv_s[0], kv_s[1], kv_s[2])
```

```python
import functools

import jax
import jax.numpy as jnp
from jax import lax
from jax.experimental import pallas as pl
from jax.experimental.pallas import tpu as pltpu

F32 = jnp.float32
BF16 = jnp.bfloat16

LANES = 128
SUBLANES = 8
D_MODEL = 2048
A_HEAD_DIM = 64
A_WIDTH = 1024
A_HEADS = 16
QUAD_LANES = 256
A_QUADS = A_WIDTH // QUAD_LANES
CHUNK = A_HEAD_DIM
RWKV_SEQS_PER_STEP = 4
CHUNKS_PER_STEP = 2
LORA_RANK = 64
A_SHIFT_WIDTH = 3 * A_WIDTH + 2 * LORA_RANK
GN_EPS = 64e-5
NORM_EPS = 1e-6
ATT_GROUPS = ((128, 1), (512, 4), (2048, 16))
N_BACK = 128
HEADS_PER_GROUP = 4
B_HEAD_DIM = 128
B_GROUP_WIDTH = HEADS_PER_GROUP * B_HEAD_DIM
B_QKV_WIDTH = 3 * B_GROUP_WIDTH
ATT_PROMPT_ROWS = 1024
ATT_SAMPLE_WINDOW_ROWS = 2048
ATT_RESIDUE_GROUP = 4
ROPE_THETA = 10000.0
PAST_LEN = 16384
ZA_WIDTH = A_SHIFT_WIDTH + A_WIDTH
ZB_WIDTH = 3 * B_QKV_WIDTH
ZC_WIDTH = 2 * D_MODEL + B_GROUP_WIDTH
VMEM_LIMIT = 48 * 1024 * 1024


def _cparams(sem):
    return pltpu.CompilerParams(dimension_semantics=sem, vmem_limit_bytes=VMEM_LIMIT)


def _sigmoid(x):
    return 1.0 / (1.0 + jnp.exp(-x))


def _rmsnorm_kernel(xp_ref, xs_ref, g_ref, o_ref, *, n_prompt_tiles):
    def emit(x):
        y = x * lax.rsqrt(jnp.mean(x * x, axis=-1, keepdims=True) + NORM_EPS)
        o_ref[...] = (y * g_ref[...]).astype(BF16)

    i = pl.program_id(0)

    @pl.when(i < n_prompt_tiles)
    def _():
        emit(xp_ref[...])

    @pl.when(i >= n_prompt_tiles)
    def _():
        emit(xs_ref[...])


def _rmsnorm(xp, xs, g):
    rp, d = xp.shape
    rs = xs.shape[0]
    tm = rs
    n_p = rp // tm
    return pl.pallas_call(
        functools.partial(_rmsnorm_kernel, n_prompt_tiles=n_p),
        grid=(n_p + 1,),
        in_specs=[
            pl.BlockSpec((tm, d), lambda i: (jnp.minimum(i, n_p - 1), 0)),
            pl.BlockSpec((tm, d), lambda i: (0, 0)),
            pl.BlockSpec((1, d), lambda i: (0, 0)),
        ],
        out_specs=pl.BlockSpec((tm, d), lambda i: (i, 0)),
        out_shape=jax.ShapeDtypeStruct((rp + rs, d), BF16),
        compiler_params=_cparams(("arbitrary",)),
    )(xp, xs, g)


def _mm_kernel(a_ref, w_ref, o_ref):
    o_ref[...] = jnp.dot(a_ref[...], w_ref[...], preferred_element_type=F32)


def _matmul(a, w, tm, tn):
    m, k = a.shape
    n = w.shape[1]
    return pl.pallas_call(
        _mm_kernel,
        grid=(n // tn, m // tm),
        in_specs=[
            pl.BlockSpec((tm, k), lambda j, i: (i, 0)),
            pl.BlockSpec((k, tn), lambda j, i: (0, j)),
        ],
        out_specs=pl.BlockSpec((tm, tn), lambda j, i: (i, j)),
        out_shape=jax.ShapeDtypeStruct((m, n), F32),
        compiler_params=_cparams(("parallel", "arbitrary")),
    )(a, w)


def _split_dot(x, ones_bf16):
    hi = x.astype(BF16)
    lo = (x - hi.astype(F32)).astype(BF16)
    return (jnp.dot(hi, ones_bf16, preferred_element_type=F32)
            + jnp.dot(lo, ones_bf16, preferred_element_type=F32))


def _rwkv_kernel(z_ref, shift0_ref, s0_ref, mu_ref, w0_ref, w2_ref, a0_ref, a2_ref, kk_ref, ka_ref,
                 rk_ref, gnw_ref, gnb_ref,
                 ya_ref, sT_ref,
                 nkk_ref, w_ref, b_ref, k_ref, r_ref, v_ref, y_ref, bonus_ref,
                 *, n_seq):
    t_len = SUBLANES
    tc = n_seq * t_len
    heads_per_quad = QUAD_LANES // A_HEAD_DIM

    lane = lax.broadcasted_iota(jnp.int32, (QUAD_LANES, QUAD_LANES), 1)
    row = lax.broadcasted_iota(jnp.int32, (QUAD_LANES, QUAD_LANES), 0)
    head_ones = (lane // A_HEAD_DIM == row // A_HEAD_DIM).astype(BF16)
    diag = (lane % A_HEAD_DIM) == (row % A_HEAD_DIM)

    z = z_ref[:, :A_SHIFT_WIDTH]
    prev = pltpu.roll(z, 1, axis=0)
    row_id = lax.broadcasted_iota(jnp.int32, (tc, 1), 0)
    for g in range(n_seq):
        prev = jnp.where(row_id == g * t_len, shift0_ref[g], prev)
    zs = z + (prev - z) * mu_ref[...]

    lora = zs[:, 3 * A_WIDTH:]
    w_lo = jnp.tanh(lora[:, :LORA_RANK]).astype(BF16)
    a_lo = lora[:, LORA_RANK:].astype(BF16)
    lw = w0_ref[...] + jnp.dot(w_lo, w2_ref[...].astype(BF16), preferred_element_type=F32)
    nlw = -lw
    log_w = -(jnp.maximum(nlw, 0.0) + jnp.log1p(jnp.exp(-jnp.abs(nlw)))) - 0.5
    w_ref[...] = jnp.exp(-jnp.exp(log_w))
    a = _sigmoid(a0_ref[...] + jnp.dot(a_lo, a2_ref[...].astype(BF16), preferred_element_type=F32))

    for q in range(A_QUADS):
        sl = slice(q * QUAD_LANES, (q + 1) * QUAD_LANES)
        r_q = zs[:, q * QUAD_LANES:(q + 1) * QUAD_LANES]
        k_q = zs[:, A_WIDTH + q * QUAD_LANES:A_WIDTH + (q + 1) * QUAD_LANES]
        v_q = zs[:, 2 * A_WIDTH + q * QUAD_LANES:2 * A_WIDTH + (q + 1) * QUAD_LANES]
        a_q = a[:, sl]
        kk = k_q * kk_ref[:, sl]
        n2 = _split_dot(kk * kk, head_ones)
        kk = kk / jnp.maximum(jnp.sqrt(n2), 1e-12)
        k_mod = k_q * (1.0 + (a_q - 1.0) * ka_ref[:, sl])
        nkk_ref[:, sl] = -kk
        b_ref[:, sl] = kk * a_q
        k_ref[:, sl] = k_mod
        r_ref[:, sl] = r_q
        v_ref[:, sl] = v_q
        bonus_ref[:, sl] = _split_dot(r_q * k_mod * rk_ref[:, sl], head_ones) * v_q

    row8 = lax.broadcasted_iota(jnp.int32, (SUBLANES, QUAD_LANES), 0)

    def expand(tiles, u):
        return jnp.concatenate(
            [jnp.broadcast_to(t[u:u + 1], (A_HEAD_DIM, QUAD_LANES)) for t in tiles], axis=0)

    seqs = range(n_seq)
    tiles = lambda ref, g: [ref[g * t_len:(g + 1) * t_len, q * QUAD_LANES:(q + 1) * QUAD_LANES]
                            for q in range(A_QUADS)]
    vecs = [[tiles(ref, g) for ref in (nkk_ref, w_ref, b_ref, k_ref, r_ref, v_ref)] for g in seqs]
    s = [jnp.concatenate([jnp.concatenate([s0_ref[g, q * heads_per_quad + h] for h in range(heads_per_quad)],
                                          axis=1) for q in range(A_QUADS)], axis=0) for g in seqs]
    ytiles = [[jnp.zeros((SUBLANES, QUAD_LANES), F32) for _ in range(A_QUADS)] for _ in seqs]
    for u in range(t_len):
        for g in seqs:
            nkk8, w8, b8, k8, r8, v8 = vecs[g]
            sa = jnp.dot((s[g] * expand(nkk8, u)).astype(BF16), head_ones, preferred_element_type=F32)
            vcol = jnp.dot(jnp.where(diag, expand(v8, u), 0.0).astype(BF16), head_ones,
                           preferred_element_type=F32)
            s[g] = s[g] * expand(w8, u) + sa * expand(b8, u) + vcol * expand(k8, u)
            ycol = jnp.dot((s[g] * expand(r8, u)).astype(BF16), head_ones, preferred_element_type=F32)
            ysel = jnp.where(diag, ycol, 0.0)
            for q in range(A_QUADS):
                yrow = jnp.sum(ysel[q * A_HEAD_DIM:(q + 1) * A_HEAD_DIM], axis=0, keepdims=True)
                ytiles[g][q] = jnp.where(row8 == u, yrow, ytiles[g][q])
    for g in seqs:
        for q in range(A_QUADS):
            y_ref[g * t_len:(g + 1) * t_len, q * QUAD_LANES:(q + 1) * QUAD_LANES] = ytiles[g][q]
        for hd in range(A_HEADS):
            q, h = divmod(hd, heads_per_quad)
            sT_ref[g, hd] = s[g][q * A_HEAD_DIM:(q + 1) * A_HEAD_DIM, h * A_HEAD_DIM:(h + 1) * A_HEAD_DIM]

    for q in range(A_QUADS):
        sl = slice(q * QUAD_LANES, (q + 1) * QUAD_LANES)
        y = y_ref[:, sl]
        mean = _split_dot(y, head_ones) * (1.0 / A_HEAD_DIM)
        yc = y - mean
        var = _split_dot(yc * yc, head_ones) * (1.0 / A_HEAD_DIM)
        yn = yc * lax.rsqrt(var + GN_EPS) * gnw_ref[:, sl] + gnb_ref[:, sl]
        yn = yn + bonus_ref[:, sl]
        g = z_ref[:, A_SHIFT_WIDTH + q * QUAD_LANES:A_SHIFT_WIDTH + (q + 1) * QUAD_LANES]
        ya_ref[:, sl] = (yn * (g * _sigmoid(g))).astype(BF16)


def _rwkv_short(za, row0, n_seq, shift0, s0, mu, w0, w2, a0, a2, k_k, k_a, r_k, gn_w, gn_b):
    per_step = RWKV_SEQS_PER_STEP
    tr = per_step * SUBLANES
    vec = lambda n: pl.BlockSpec((1, n), lambda b: (0, 0))
    state_spec = pl.BlockSpec((per_step, A_HEADS, A_HEAD_DIM, A_HEAD_DIM), lambda b: (b, 0, 0, 0))
    return pl.pallas_call(
        functools.partial(_rwkv_kernel, n_seq=per_step),
        grid=(n_seq // per_step,),
        in_specs=[
            pl.BlockSpec((tr, ZA_WIDTH), lambda b: (row0 // tr + b, 0)),
            pl.BlockSpec((per_step, 1, A_SHIFT_WIDTH), lambda b: (b, 0, 0)),
            state_spec,
            vec(A_SHIFT_WIDTH), vec(A_WIDTH),
            pl.BlockSpec((LORA_RANK, A_WIDTH), lambda b: (0, 0)),
            vec(A_WIDTH),
            pl.BlockSpec((LORA_RANK, A_WIDTH), lambda b: (0, 0)),
            vec(A_WIDTH), vec(A_WIDTH), vec(A_WIDTH), vec(A_WIDTH), vec(A_WIDTH),
        ],
        out_specs=[
            pl.BlockSpec((tr, A_WIDTH), lambda b: (b, 0)),
            state_spec,
        ],
        out_shape=[
            jax.ShapeDtypeStruct((n_seq * SUBLANES, A_WIDTH), BF16),
            jax.ShapeDtypeStruct((n_seq, A_HEADS, A_HEAD_DIM, A_HEAD_DIM), F32),
        ],
        scratch_shapes=[pltpu.VMEM((tr, A_WIDTH), F32) for _ in range(8)],
        compiler_params=_cparams(("arbitrary",)),
    )(za, shift0, s0, mu, w0, w2, a0, a2, k_k, k_a, r_k, gn_w, gn_b)


def _head_blocks(y, same_head):
    return jnp.where(same_head, jnp.concatenate([y, y, y, y], axis=0), jnp.zeros((), y.dtype))


def _off_block(t, s, b):
    return jnp.logical_and(t // (2 * b) == s // (2 * b), jnp.logical_and(t % (2 * b) >= b, s % (2 * b) < b))


def _split2(x):
    hi = x.astype(BF16)
    return hi, (x - hi.astype(F32)).astype(BF16)


def _head_matmul3(x, ys, same_head):
    xh, xl = _split2(x)
    outs = []
    for y in ys:
        yh, yl = _split2(y)
        bh = _head_blocks(yh, same_head)
        out = jnp.dot(xh, bh, preferred_element_type=F32)
        out += jnp.dot(xl, bh, preferred_element_type=F32)
        out += jnp.dot(xh, _head_blocks(yl, same_head), preferred_element_type=F32)
        outs.append(out)
    return outs


def _head_matmul(x, y, same_head):
    return jnp.dot(x.astype(BF16), _head_blocks(y.astype(BF16), same_head), preferred_element_type=F32)


def _rwkv_chunk_kernel(z_ref, shift0_ref, s0_ref, mu_ref, w0_ref, w2_ref, a0_ref, a2_ref, kk_ref, ka_ref,
                       rk_ref, gnw_ref, gnb_ref,
                       ya_ref, sT_ref, carry_ref, st_ref):
    c = pl.program_id(1)
    n_c = pl.num_programs(1)
    tc = CHUNK
    rows_all = tc * CHUNKS_PER_STEP

    @pl.when(c == 0)
    def _():
        carry_ref[...] = shift0_ref[0]
        st_ref[...] = s0_ref[0]

    lane = lax.broadcasted_iota(jnp.int32, (QUAD_LANES, QUAD_LANES), 1)
    row = lax.broadcasted_iota(jnp.int32, (QUAD_LANES, QUAD_LANES), 0)
    same_head = lane // A_HEAD_DIM == row // A_HEAD_DIM
    head_ones = same_head.astype(BF16)
    t_idx = lax.broadcasted_iota(jnp.int32, (tc, QUAD_LANES), 0)
    lane_t = lax.broadcasted_iota(jnp.int32, (tc, QUAD_LANES), 1)
    s_idx = lane_t % A_HEAD_DIM
    strict = s_idx < t_idx
    incl = s_idx <= t_idx
    eye = (s_idx == t_idx).astype(F32)
    lane_head = lane_t // A_HEAD_DIM
    tri = (lax.broadcasted_iota(jnp.int32, (tc, tc), 1) <= lax.broadcasted_iota(jnp.int32, (tc, tc), 0)
           ).astype(BF16)

    z = z_ref[:, :A_SHIFT_WIDTH]
    prev = pltpu.roll(z, 1, axis=0)
    first = lax.broadcasted_iota(jnp.int32, (rows_all, 1), 0) == 0
    prev = jnp.where(first, carry_ref[...], prev)
    carry_ref[...] = z[rows_all - 1:rows_all, :]
    zs = z + (prev - z) * mu_ref[...]

    lora = zs[:, 3 * A_WIDTH:]
    w_lo = jnp.tanh(lora[:, :LORA_RANK]).astype(BF16)
    a_lo = lora[:, LORA_RANK:].astype(BF16)
    lw = w0_ref[...] + jnp.dot(w_lo, w2_ref[...].astype(BF16), preferred_element_type=F32)
    nlw = -lw
    log_w = -(jnp.maximum(nlw, 0.0) + jnp.log1p(jnp.exp(-jnp.abs(nlw)))) - 0.5
    log_decay = -jnp.exp(log_w)
    a = _sigmoid(a0_ref[...] + jnp.dot(a_lo, a2_ref[...].astype(BF16), preferred_element_type=F32))

    quads = range(A_QUADS)
    lanes = [slice(q * QUAD_LANES, (q + 1) * QUAD_LANES) for q in quads]

    def tokens(job):
        sub, q = job
        sl = lanes[q]
        rs = slice(sub * tc, (sub + 1) * tc)
        r_q = zs[rs, q * QUAD_LANES:(q + 1) * QUAD_LANES]
        k_q = zs[rs, A_WIDTH + q * QUAD_LANES:A_WIDTH + (q + 1) * QUAD_LANES]
        v_q = zs[rs, 2 * A_WIDTH + q * QUAD_LANES:2 * A_WIDTH + (q + 1) * QUAD_LANES]
        a_q = a[rs, sl]
        kk = k_q * kk_ref[:, sl]
        n2 = _split_dot(kk * kk, head_ones)
        kk = kk / jnp.maximum(jnp.sqrt(n2), 1e-12)
        k_mod = k_q * (1.0 + (a_q - 1.0) * ka_ref[:, sl])
        beta = kk * a_q
        bonus = _split_dot(r_q * k_mod * rk_ref[:, sl], head_ones) * v_q
        ld = log_decay[rs, sl]
        h1, h2 = _split2(ld)
        h3 = (ld - h1.astype(F32) - h2.astype(F32)).astype(BF16)
        cum = (jnp.dot(tri, h1, preferred_element_type=F32) + jnp.dot(tri, h2, preferred_element_type=F32)
               + jnp.dot(tri, h3, preferred_element_type=F32))
        cum_last = cum[tc - 1:tc, :]
        alpha_p = -kk * jnp.exp(cum - ld)
        r_p = r_q * jnp.exp(cum)
        inv_p = jnp.exp(-cum)
        to_end = jnp.exp(cum_last - cum)
        lhs = jnp.concatenate([alpha_p, r_p], axis=0).astype(BF16)
        g_b = _dot_nt(lhs, _head_blocks((beta * inv_p).astype(BF16), same_head))
        g_k = _dot_nt(lhs, _head_blocks((k_mod * inv_p).astype(BF16), same_head))
        lhs_t = jnp.concatenate([beta * to_end, k_mod * to_end], axis=0).astype(BF16)
        p_col = _split_dot(jnp.where(eye > 0, jnp.exp(cum_last), 0.0), head_ones)
        return dict(v=v_q, bonus=bonus, ar_p=lhs,
                    a_ab=jnp.where(strict, g_b[:tc], 0.0), a_rb=jnp.where(incl, g_b[tc:], 0.0),
                    a_ak=jnp.where(strict, g_k[:tc], 0.0), a_rk=jnp.where(incl, g_k[tc:], 0.0),
                    lhs_t=lhs_t, p_col=p_col)

    jobs = [(sub, q) for sub in range(CHUNKS_PER_STEP) for q in quads]
    n_jobs = range(len(jobs))
    tk = [tokens(job) for job in jobs]

    inv = [eye + jnp.where(_off_block(t_idx, s_idx, 1), t["a_ab"], 0.0) for t in tk]
    b = 2
    while b < tc:
        off = _off_block(t_idx, s_idx, b)
        mid = [_head_matmul3(jnp.where(off, tk[j]["a_ab"], 0.0), [inv[j]], same_head)[0] for j in n_jobs]
        inv = [inv[j] + _head_matmul3(inv[j], [mid[j]], same_head)[0] for j in n_jobs]
        b *= 2
    av = [_head_matmul(jnp.concatenate([t["a_ak"], t["a_rk"]], axis=0), t["v"], same_head) for t in tk]
    w1 = [m[:tc] for m in av]
    y_v = [m[tc:] for m in av]

    st = [st_ref[q] for q in quads]
    for sub in range(CHUNKS_PER_STEP):
        js = [sub * A_QUADS + q for q in quads]
        st_blocks = [_head_blocks(s.astype(BF16), same_head) for s in st]
        from_st = [jnp.dot(tk[j]["ar_p"], st_blocks[q], preferred_element_type=F32)
                   for q, j in zip(quads, js)]
        rhs = [w1[j] + from_st[q][:tc] for q, j in zip(quads, js)]
        u = [_head_matmul3(inv[j], [rhs[q]], same_head)[0] for q, j in zip(quads, js)]
        new_st = []
        for q, j in zip(quads, js):
            t = tk[j]
            y = from_st[q][tc:] + _head_matmul(t["a_rb"], u[q], same_head) + y_v[j]
            rhs_t = jnp.concatenate([u[q], t["v"]], axis=0).astype(BF16)
            cross = lax.dot_general(t["lhs_t"], rhs_t, (((0,), (0,)), ((), ())), preferred_element_type=F32)
            new = t["p_col"] * st[q]
            for h in range(QUAD_LANES // A_HEAD_DIM):
                new += jnp.where(lane_head == h, cross[h * A_HEAD_DIM:(h + 1) * A_HEAD_DIM], 0.0)
            new_st.append(new)

            sl = lanes[q]
            rs = slice(sub * tc, (sub + 1) * tc)
            mean = _split_dot(y, head_ones) * (1.0 / A_HEAD_DIM)
            yc = y - mean
            var = _split_dot(yc * yc, head_ones) * (1.0 / A_HEAD_DIM)
            yn = yc * lax.rsqrt(var + GN_EPS) * gnw_ref[:, sl] + gnb_ref[:, sl]
            yn = yn + t["bonus"]
            g = z_ref[rs, A_SHIFT_WIDTH + q * QUAD_LANES:A_SHIFT_WIDTH + (q + 1) * QUAD_LANES]
            ya_ref[rs, sl] = (yn * (g * _sigmoid(g))).astype(BF16)
        st = new_st
    for q in quads:
        st_ref[q] = st[q]

    @pl.when(c == n_c - 1)
    def _():
        sT_ref[0] = st_ref[...]


def _rwkv_chunked(za, t_len, shift0, s0, mu, w0, w2, a0, a2, k_k, k_a, r_k, gn_w, gn_b):
    rows = CHUNK * CHUNKS_PER_STEP
    n_c = t_len // rows
    vec = lambda n: pl.BlockSpec((1, n), lambda b, c: (0, 0))
    state_spec = pl.BlockSpec((1, A_QUADS, A_HEAD_DIM, QUAD_LANES), lambda b, c: (b, 0, 0, 0))
    return pl.pallas_call(
        _rwkv_chunk_kernel,
        grid=(1, n_c),
        in_specs=[
            pl.BlockSpec((rows, ZA_WIDTH), lambda b, c: (c, 0)),
            pl.BlockSpec((1, 1, A_SHIFT_WIDTH), lambda b, c: (b, 0, 0)),
            state_spec,
            vec(A_SHIFT_WIDTH), vec(A_WIDTH),
            pl.BlockSpec((LORA_RANK, A_WIDTH), lambda b, c: (0, 0)),
            vec(A_WIDTH),
            pl.BlockSpec((LORA_RANK, A_WIDTH), lambda b, c: (0, 0)),
            vec(A_WIDTH), vec(A_WIDTH), vec(A_WIDTH), vec(A_WIDTH), vec(A_WIDTH),
        ],
        out_specs=[
            pl.BlockSpec((rows, A_WIDTH), lambda b, c: (c, 0)),
            state_spec,
        ],
        out_shape=[
            jax.ShapeDtypeStruct((t_len, A_WIDTH), BF16),
            jax.ShapeDtypeStruct((1, A_QUADS, A_HEAD_DIM, QUAD_LANES), F32),
        ],
        scratch_shapes=[pltpu.VMEM((1, A_SHIFT_WIDTH), F32),
                        pltpu.VMEM((A_QUADS, A_HEAD_DIM, QUAD_LANES), F32)],
        compiler_params=_cparams(("arbitrary", "arbitrary")),
    )(za, shift0, s0, mu, w0, w2, a0, a2, k_k, k_a, r_k, gn_w, gn_b)


def _keymajor_to_state(s):
    b = s.shape[0]
    return s.reshape(b, A_QUADS, A_HEAD_DIM, 4, A_HEAD_DIM).transpose(0, 1, 3, 4, 2).reshape(
        b, A_HEADS, A_HEAD_DIM, A_HEAD_DIM)


def _norm_rope(x, gain, cos, sin):
    y = x * lax.rsqrt(jnp.mean(x * x, axis=-1, keepdims=True) + NORM_EPS) * gain
    return y * cos + pltpu.roll(y, B_HEAD_DIM // 2, axis=1) * sin


def _dot_nt(a, b):
    return lax.dot_general(a, b, (((1,), (1,)), ((), ())), preferred_element_type=F32)


def _attn_prompt_kernel(q_ref, k_ref, v_ref, cos_ref, sin_ref, qg_ref, kg_ref,
                        o_ref, lse_ref, kr_ref, kprev_ref, vprev_ref, qs_ref, *, dil, n_sub):
    i = pl.program_id(0)
    hd = pl.program_id(1)
    blk = N_BACK

    @pl.when(i == 0)
    def _():
        kprev_ref[hd] = jnp.zeros((dil, blk, B_HEAD_DIM), BF16)
        vprev_ref[hd] = jnp.zeros((dil, blk, B_HEAD_DIM), BF16)

    cos = cos_ref[...]
    sin = sin_ref[...]
    qs_ref[...] = _norm_rope(q_ref[...], qg_ref[...], cos, sin)
    kr_ref[...] = _norm_rope(k_ref[...], kg_ref[...], cos, sin)

    rowi = lax.broadcasted_iota(jnp.int32, (blk, blk), 0)
    coli = lax.broadcasted_iota(jnp.int32, (blk, blk), 1)
    cur_ok = coli <= rowi
    scale = B_HEAD_DIM ** -0.5
    span = lambda sub, rho: pl.ds(sub * blk * dil + rho, blk, stride=dil)
    units = [(sub, rho) for sub in range(n_sub) for rho in range(dil)]
    for u0 in range(0, len(units), ATT_RESIDUE_GROUP):
        group = units[u0:u0 + ATT_RESIDUE_GROUP]
        rows = [span(sub, rho) for sub, rho in group]
        qn = [qs_ref[r, :].astype(BF16) for r in rows]
        kb = [kr_ref[r, :].astype(BF16) for r in rows]
        vb = [v_ref[r, :].astype(BF16) for r in rows]
        kp = [kr_ref[span(sub - 1, rho), :].astype(BF16) if sub > 0 else kprev_ref[hd, rho] for sub, rho in group]
        vp = [v_ref[span(sub - 1, rho), :].astype(BF16) if sub > 0 else vprev_ref[hd, rho] for sub, rho in group]
        prev_ok = [coli >= rowi if sub > 0 else jnp.logical_and(coli >= rowi, i > 0) for sub, _ in group]
        s_cur = [jnp.where(cur_ok, _dot_nt(q, k) * scale, -jnp.inf) for q, k in zip(qn, kb)]
        s_prev = [jnp.where(ok, _dot_nt(q, k) * scale, -jnp.inf) for ok, q, k in zip(prev_ok, qn, kp)]
        m = [jnp.maximum(jnp.max(c, axis=-1, keepdims=True), jnp.max(p, axis=-1, keepdims=True))
             for c, p in zip(s_cur, s_prev)]
        p_cur = [jnp.exp(c - mm) for c, mm in zip(s_cur, m)]
        p_prev = [jnp.exp(p - mm) for p, mm in zip(s_prev, m)]
        den = [jnp.sum(c, axis=-1, keepdims=True) + jnp.sum(p, axis=-1, keepdims=True)
               for c, p in zip(p_cur, p_prev)]
        for n, r in enumerate(rows):
            o = jnp.dot((p_cur[n] / den[n]).astype(BF16), vb[n], preferred_element_type=F32)
            o += jnp.dot((p_prev[n] / den[n]).astype(BF16), vp[n], preferred_element_type=F32)
            o_ref[r, :] = o
            lse_ref[r, :] = jnp.broadcast_to(m[n] + jnp.log(den[n]), (blk, B_HEAD_DIM))
    for rho in range(dil):
        kprev_ref[hd, rho] = kr_ref[span(n_sub - 1, rho), :].astype(BF16)
        vprev_ref[hd, rho] = v_ref[span(n_sub - 1, rho), :].astype(BF16)


def _attn_prompt(zb, gi, dil, seq, cos, sin, q_gain, k_gain):
    n_sub = max(1, ATT_PROMPT_ROWS // (N_BACK * dil))
    tr = N_BACK * dil * n_sub
    heads_qkv = B_QKV_WIDTH // B_HEAD_DIM
    blk = lambda which: pl.BlockSpec(
        (tr, B_HEAD_DIM), lambda i, hd: (i, which * heads_qkv + gi * HEADS_PER_GROUP + hd))
    tab = pl.BlockSpec((tr, B_HEAD_DIM), lambda i, hd: (i, 0))
    gain = pl.BlockSpec((1, B_HEAD_DIM), lambda i, hd: (0, 0))
    out = pl.BlockSpec((tr, B_HEAD_DIM), lambda i, hd: (i, hd))
    shp = jax.ShapeDtypeStruct((seq, B_GROUP_WIDTH), F32)
    prev = pltpu.VMEM((HEADS_PER_GROUP, dil, N_BACK, B_HEAD_DIM), BF16)
    return pl.pallas_call(
        functools.partial(_attn_prompt_kernel, dil=dil, n_sub=n_sub),
        grid=(seq // tr, HEADS_PER_GROUP),
        in_specs=[blk(0), blk(1), blk(2), tab, tab, gain, gain],
        out_specs=[out, out, out],
        out_shape=[shp, shp, shp],
        scratch_shapes=[prev, prev, pltpu.VMEM((tr, B_HEAD_DIM), F32)],
        compiler_params=_cparams(("arbitrary", "arbitrary")),
    )(zb, zb, zb, cos, sin, q_gain, k_gain)


def _attn_sample_kernel(q_ref, k_ref, v_ref, cache_ref, cos_ref, sin_ref, qg_ref, kg_ref,
                        o_ref, lse_ref, new_ref, *, window, dil, t_new, n_seq):
    per_pos = 2 * HEADS_PER_GROUP
    cos = cos_ref[...]
    sin = sin_ref[...]
    scale = B_HEAD_DIM ** -0.5
    tq = lax.broadcasted_iota(jnp.int32, (t_new, window), 0)
    cc = lax.broadcasted_iota(jnp.int32, (t_new, window), 1)
    cache_ok = jnp.logical_and(((cc - tq + dil * N_BACK) & (dil - 1)) == 0, cc >= tq)
    tn = lax.broadcasted_iota(jnp.int32, (t_new, t_new), 0)
    un = lax.broadcasted_iota(jnp.int32, (t_new, t_new), 1)
    new_ok = jnp.logical_and(((tn - un + dil * N_BACK) & (dil - 1)) == 0, un <= tn)
    for g in range(n_seq):
        new_ref[g, 0:(window - t_new) * per_pos, :] = cache_ref[g, t_new * per_pos:window * per_pos, :]
    units = [(g, h) for g in range(n_seq) for h in range(HEADS_PER_GROUP)]
    for u0 in range(0, len(units), HEADS_PER_GROUP):
        group = units[u0:u0 + HEADS_PER_GROUP]
        rows = [slice(g * t_new, (g + 1) * t_new) for g, _ in group]
        cols = [slice(h * B_HEAD_DIM, (h + 1) * B_HEAD_DIM) for _, h in group]
        qn = [_norm_rope(q_ref[r, c], qg_ref[...], cos, sin).astype(BF16) for r, c in zip(rows, cols)]
        kr = [_norm_rope(k_ref[r, c], kg_ref[...], cos, sin) for r, c in zip(rows, cols)]
        v_new = [v_ref[r, c] for r, c in zip(rows, cols)]
        kc = [cache_ref[g, pl.ds(h, window, stride=per_pos), :].astype(BF16) for g, h in group]
        vc = [cache_ref[g, pl.ds(HEADS_PER_GROUP + h, window, stride=per_pos), :].astype(BF16) for g, h in group]
        s_c = [jnp.where(cache_ok, _dot_nt(q, k) * scale, -jnp.inf) for q, k in zip(qn, kc)]
        s_n = [jnp.where(new_ok, _dot_nt(q, k.astype(BF16)) * scale, -jnp.inf) for q, k in zip(qn, kr)]
        m = [jnp.maximum(jnp.max(c, axis=-1, keepdims=True), jnp.max(n, axis=-1, keepdims=True))
             for c, n in zip(s_c, s_n)]
        p_c = [jnp.exp(c - mm) for c, mm in zip(s_c, m)]
        p_n = [jnp.exp(n - mm) for n, mm in zip(s_n, m)]
        den = [jnp.sum(c, axis=-1, keepdims=True) + jnp.sum(n, axis=-1, keepdims=True) for c, n in zip(p_c, p_n)]
        for i, (g, h) in enumerate(group):
            o = jnp.dot((p_c[i] / den[i]).astype(BF16), vc[i], preferred_element_type=F32)
            o += jnp.dot((p_n[i] / den[i]).astype(BF16), v_new[i].astype(BF16), preferred_element_type=F32)
            o_ref[rows[i], cols[i]] = o
            lse_ref[rows[i], cols[i]] = jnp.broadcast_to(m[i] + jnp.log(den[i]), (t_new, B_HEAD_DIM))
            base = (window - t_new) * per_pos
            new_ref[g, pl.ds(base + h, t_new, stride=per_pos), :] = kr[i]
            new_ref[g, pl.ds(base + HEADS_PER_GROUP + h, t_new, stride=per_pos), :] = v_new[i]


def _attn_sample(zb, gi, window, dil, row0, n_seq, t_new, cache, cos, sin, q_gain, k_gain):
    rows = window * 2 * HEADS_PER_GROUP
    per_step = max(1, min(n_seq, ATT_SAMPLE_WINDOW_ROWS // window))
    tr = per_step * t_new
    cache2 = cache.reshape(n_seq, rows, B_HEAD_DIM)
    blk = lambda off: pl.BlockSpec((tr, B_GROUP_WIDTH), lambda b: (row0 // tr + b, off + gi))
    tab = pl.BlockSpec((t_new, B_HEAD_DIM), lambda b: (0, 0))
    gain = pl.BlockSpec((1, B_HEAD_DIM), lambda b: (0, 0))
    out = pl.BlockSpec((tr, B_GROUP_WIDTH), lambda b: (b, 0))
    win = pl.BlockSpec((per_step, rows, B_HEAD_DIM), lambda b: (b, 0, 0))
    shp = jax.ShapeDtypeStruct((n_seq * t_new, B_GROUP_WIDTH), F32)
    o, lse, new = pl.pallas_call(
        functools.partial(_attn_sample_kernel, window=window, dil=dil, t_new=t_new, n_seq=per_step),
        grid=(n_seq // per_step,),
        in_specs=[blk(0), blk(3), blk(6), win, tab, tab, gain, gain],
        out_specs=[out, out, win],
        out_shape=[shp, shp, jax.ShapeDtypeStruct((n_seq, rows, B_HEAD_DIM), F32)],
        compiler_params=_cparams(("arbitrary",)),
    )(zb, zb, zb, cache2, cos, sin, q_gain, k_gain)
    return o, lse, new.reshape(cache.shape)


def _combine_kernel(o0, o1, o2, l0, l1, l2, g_ref, y_ref):
    a, b, c = l0[...], l1[...], l2[...]
    m = jnp.maximum(jnp.maximum(a, b), c)
    ea, eb, ec = jnp.exp(a - m), jnp.exp(b - m), jnp.exp(c - m)
    o = (ea * o0[...] + eb * o1[...] + ec * o2[...]) / (ea + eb + ec)
    g = g_ref[...]
    y_ref[...] = (o * (g * _sigmoid(g))).astype(BF16)


def _combine(outs, lses, zc, row_block0, tm):
    rows = outs[0].shape[0]
    spec = pl.BlockSpec((tm, B_GROUP_WIDTH), lambda i: (i, 0))
    return pl.pallas_call(
        _combine_kernel,
        grid=(rows // tm,),
        in_specs=[spec] * 6 + [pl.BlockSpec((tm, B_GROUP_WIDTH), lambda i: (row_block0 + i, 0))],
        out_specs=spec,
        out_shape=jax.ShapeDtypeStruct((rows, B_GROUP_WIDTH), BF16),
        compiler_params=_cparams(("parallel",)),
    )(*outs, *lses, zc)


def _merge_kernel(ya_ref, yb_ref, ma_ref, mb_ref, wa_ref, wb_ref, o_ref):
    ua = jnp.dot(ya_ref[...], wa_ref[...], preferred_element_type=F32)
    ub = jnp.dot(yb_ref[...], wb_ref[...], preferred_element_type=F32)
    o_ref[...] = (_sigmoid(ma_ref[...]) * ua + _sigmoid(mb_ref[...]) * ub).astype(BF16)


def _merge(ya, yb, zc, row_block0, wa, wb, tm, tn):
    m = ya.shape[0]
    nb = D_MODEL // tn
    off_a = B_GROUP_WIDTH // tn
    off_b = (B_GROUP_WIDTH + D_MODEL) // tn
    assert off_a * tn == B_GROUP_WIDTH
    return pl.pallas_call(
        _merge_kernel,
        grid=(m // tm, nb),
        in_specs=[
            pl.BlockSpec((tm, A_WIDTH), lambda i, j: (i, 0)),
            pl.BlockSpec((tm, B_GROUP_WIDTH), lambda i, j: (i, 0)),
            pl.BlockSpec((tm, tn), lambda i, j: (row_block0 + i, off_a + j)),
            pl.BlockSpec((tm, tn), lambda i, j: (row_block0 + i, off_b + j)),
            pl.BlockSpec((A_WIDTH, tn), lambda i, j: (0, j)),
            pl.BlockSpec((B_GROUP_WIDTH, tn), lambda i, j: (0, j)),
        ],
        out_specs=pl.BlockSpec((tm, tn), lambda i, j: (i, j)),
        out_shape=jax.ShapeDtypeStruct((m, D_MODEL), BF16),
        compiler_params=_cparams(("parallel", "arbitrary")),
    )(ya, yb, zc, zc, wa, wb)


def _outproj_kernel(x_ref, a_ref, w_ref, o_ref):
    o_ref[...] = x_ref[...] + jnp.dot(a_ref[...], w_ref[...], preferred_element_type=F32)


def _outproj(x, merged, w, tm, tn):
    m = x.shape[0]
    return pl.pallas_call(
        _outproj_kernel,
        grid=(D_MODEL // tn, m // tm),
        in_specs=[
            pl.BlockSpec((tm, tn), lambda j, i: (i, j)),
            pl.BlockSpec((tm, D_MODEL), lambda j, i: (i, 0)),
            pl.BlockSpec((D_MODEL, tn), lambda j, i: (0, j)),
        ],
        out_specs=pl.BlockSpec((tm, tn), lambda j, i: (i, j)),
        out_shape=jax.ShapeDtypeStruct((m, D_MODEL), F32),
        compiler_params=_cparams(("parallel", "arbitrary")),
    )(x, merged, w)


def _ple_kernel(h_ref, p_ref, wg_ref, wp_ref, o_ref):
    h = h_ref[...]
    gate = _sigmoid(jnp.dot(h.astype(BF16), wg_ref[...], preferred_element_type=F32))
    proj = jnp.dot(p_ref[...].astype(BF16), wp_ref[...], preferred_element_type=F32)
    o_ref[...] = h + gate * proj


def _ple(h, p, wg, wp, tm):
    m = h.shape[0]
    pd = p.shape[1]
    return pl.pallas_call(
        _ple_kernel,
        grid=(m // tm,),
        in_specs=[
            pl.BlockSpec((tm, D_MODEL), lambda i: (i, 0)),
            pl.BlockSpec((tm, pd), lambda i: (i, 0)),
            pl.BlockSpec((D_MODEL, D_MODEL), lambda i: (0, 0)),
            pl.BlockSpec((pd, D_MODEL), lambda i: (0, 0)),
        ],
        out_specs=pl.BlockSpec((tm, D_MODEL), lambda i: (i, 0)),
        out_shape=jax.ShapeDtypeStruct((m, D_MODEL), F32),
        compiler_params=_cparams(("parallel",)),
    )(h, p, wg, wp)


def _rope_tables(pos):
    half = B_HEAD_DIM // 2
    inv = ROPE_THETA ** (-jnp.arange(half, dtype=F32) / half)
    ang = pos.astype(F32)[:, None] * inv[None, :]
    cos, sin = jnp.cos(ang), jnp.sin(ang)
    return jnp.concatenate([cos, cos], axis=1), jnp.concatenate([-sin, sin], axis=1)


def kernel(x_prompt, x_sample, state_shift, state_wkv, cache_kv_w128, cache_kv_w512, cache_kv_w2048,
           p_prompt, p_sample, ln_g, w_in, mu, w0, w2, a0, a2, k_k, k_a, r_k, gn_w, gn_b,
           q_gain, k_gain, w_up_a, w_up_b, w_out, w_ple_gate, w_ple_proj):
    depth = ln_g.shape[0]
    assert depth == 1
    bp, seq, d = x_prompt.shape
    bs, t_new, _ = x_sample.shape
    assert bp == 1 and d == D_MODEL
    rs = bs * t_new
    caches = (cache_kv_w128, cache_kv_w512, cache_kv_w2048)

    xp = x_prompt.reshape(seq, d)
    xs = x_sample.reshape(rs, d)
    hn = _rmsnorm(xp, xs, ln_g)

    w_in0 = w_in[0]
    off_b = ZA_WIDTH
    off_c = ZA_WIDTH + ZB_WIDTH
    wa_in = w_in0[:, :off_b].astype(BF16)
    wb_in = w_in0[:, off_b:off_c].astype(BF16)
    wc_in = w_in0[:, off_c:].astype(BF16)
    tm_in = 768
    za = _matmul(hn, wa_in, tm_in, ZA_WIDTH // 3)
    zb = _matmul(hn, wb_in, tm_in, ZB_WIDTH // 3)
    zc = _matmul(hn, wc_in, tm_in, ZC_WIDTH // 3)

    flat = lambda v: v.reshape(1, -1)
    rw = (mu, flat(w0[0]), w2[0], flat(a0[0]), a2[0], flat(k_k[0]), flat(k_a[0]), flat(r_k[0]),
          flat(gn_w[0]), flat(gn_b[0]))
    ya_p, s_p = _rwkv_chunked(za, seq, jnp.zeros((1, 1, A_SHIFT_WIDTH), F32),
                              jnp.zeros((1, A_QUADS, A_HEAD_DIM, QUAD_LANES), F32), *rw)
    assert t_new == SUBLANES
    ya_s, s_s = _rwkv_short(za, seq, bs, state_shift[0][:, None, :], state_wkv[0], *rw)
    shift_p = za[seq - 1, :A_SHIFT_WIDTH].reshape(1, 1, A_SHIFT_WIDTH)
    shift_s = za[seq:, :A_SHIFT_WIDTH].reshape(bs, t_new, A_SHIFT_WIDTH)[:, -1][None]
    wkv_p = _keymajor_to_state(s_p)[None]
    wkv_s = s_s[None]

    cos_p, sin_p = _rope_tables(jnp.arange(seq, dtype=jnp.int32))
    cos_s, sin_s = _rope_tables(PAST_LEN + jnp.arange(t_new, dtype=jnp.int32))
    o_p, l_p, o_s, l_s, kv_p, kv_s = [], [], [], [], [], []
    for gi, (window, dil) in enumerate(ATT_GROUPS):
        o, lse, kr = _attn_prompt(zb, gi, dil, seq, cos_p, sin_p, q_gain, k_gain)
        o_p.append(o)
        l_p.append(lse)
        w_keep = min(window, seq)
        v_g = zb[seq - w_keep:seq, 2 * B_QKV_WIDTH + gi * B_GROUP_WIDTH:2 * B_QKV_WIDTH + (gi + 1) * B_GROUP_WIDTH]
        kv_p.append(jnp.stack([kr[seq - w_keep:].reshape(w_keep, HEADS_PER_GROUP, B_HEAD_DIM),
                               v_g.reshape(w_keep, HEADS_PER_GROUP, B_HEAD_DIM)], axis=1)[None, None])
        cache = caches[gi][0]
        o, lse, new_cache = _attn_sample(zb, gi, cache.shape[1], dil, seq, bs, t_new, cache,
                                         cos_s, sin_s, q_gain, k_gain)
        o_s.append(o)
        l_s.append(lse)
        kv_s.append(new_cache[None])
    yb_p = _combine(o_p, l_p, zc, 0, 1024)
    yb_s = _combine(o_s, l_s, zc, seq // rs, rs)

    wa_up = w_up_a[0].astype(BF16)
    wb_up = w_up_b[0].astype(BF16)
    merged_p = _merge(ya_p, yb_p, zc, 0, wa_up, wb_up, 2048, B_GROUP_WIDTH)
    merged_s = _merge(ya_s, yb_s, zc, seq // rs, wa_up, wb_up, rs, B_GROUP_WIDTH)
    w_out_b = w_out[0].astype(BF16)
    h_p = _outproj(xp, merged_p, w_out_b, 512, D_MODEL)
    h_s = _outproj(xs, merged_s, w_out_b, rs, D_MODEL)
    wg = w_ple_gate[0].astype(BF16)
    wp = w_ple_proj[0].astype(BF16)
    y_p = _ple(h_p, p_prompt[0, 0], wg, wp, 512)
    y_s = _ple(h_s, p_sample[0].reshape(rs, -1), wg, wp, rs)

    return (y_p.reshape(bp, seq, d), y_s.reshape(bs, t_new, d),
            shift_p, wkv_p, kv_p[0], kv_p[1], kv_p[2],
            shift_s, wkv_s, kv_s[0], kv_s[1], kv_s[2])
```

```python
import functools

import jax
import jax.numpy as jnp
from jax import lax
from jax.experimental import pallas as pl
from jax.experimental.pallas import tpu as pltpu

F32 = jnp.float32
BF16 = jnp.bfloat16

LANES = 128
SUBLANES = 8
D_MODEL = 2048
A_HEAD_DIM = 64
A_WIDTH = 1024
A_HEADS = 16
QUAD_LANES = 256
A_QUADS = A_WIDTH // QUAD_LANES
CHUNK = A_HEAD_DIM
RWKV_SEQS_PER_STEP = 4
CHUNKS_PER_STEP = 2
LORA_RANK = 64
A_SHIFT_WIDTH = 3 * A_WIDTH + 2 * LORA_RANK
GN_EPS = 64e-5
NORM_EPS = 1e-6
ATT_GROUPS = ((128, 1), (512, 4), (2048, 16))
N_BACK = 128
HEADS_PER_GROUP = 4
B_HEAD_DIM = 128
B_GROUP_WIDTH = HEADS_PER_GROUP * B_HEAD_DIM
B_QKV_WIDTH = 3 * B_GROUP_WIDTH
ATT_PROMPT_ROWS = 1024
ATT_SAMPLE_WINDOW_ROWS = 2048
ATT_RESIDUE_GROUP = 4
ROPE_THETA = 10000.0
PAST_LEN = 16384
ZA_WIDTH = A_SHIFT_WIDTH + A_WIDTH
ZB_WIDTH = 3 * B_QKV_WIDTH
ZC_WIDTH = 2 * D_MODEL + B_GROUP_WIDTH
VMEM_LIMIT = 48 * 1024 * 1024


def _cparams(sem):
    return pltpu.CompilerParams(dimension_semantics=sem, vmem_limit_bytes=VMEM_LIMIT)


def _sigmoid(x):
    return 1.0 / (1.0 + jnp.exp(-x))


def _rmsnorm_kernel(xp_ref, xs_ref, g_ref, o_ref, *, n_prompt_tiles):
    def emit(x):
        y = x * lax.rsqrt(jnp.mean(x * x, axis=-1, keepdims=True) + NORM_EPS)
        o_ref[...] = (y * g_ref[...]).astype(BF16)

    i = pl.program_id(0)

    @pl.when(i < n_prompt_tiles)
    def _():
        emit(xp_ref[...])

    @pl.when(i >= n_prompt_tiles)
    def _():
        emit(xs_ref[...])


def _rmsnorm(xp, xs, g):
    rp, d = xp.shape
    rs = xs.shape[0]
    tm = rs
    n_p = rp // tm
    return pl.pallas_call(
        functools.partial(_rmsnorm_kernel, n_prompt_tiles=n_p),
        grid=(n_p + 1,),
        in_specs=[
            pl.BlockSpec((tm, d), lambda i: (jnp.minimum(i, n_p - 1), 0)),
            pl.BlockSpec((tm, d), lambda i: (0, 0)),
            pl.BlockSpec((1, d), lambda i: (0, 0)),
        ],
        out_specs=pl.BlockSpec((tm, d), lambda i: (i, 0)),
        out_shape=jax.ShapeDtypeStruct((rp + rs, d), BF16),
        compiler_params=_cparams(("arbitrary",)),
    )(xp, xs, g)


def _mm_kernel(a_ref, w_ref, o_ref):
    o_ref[...] = jnp.dot(a_ref[...], w_ref[...], preferred_element_type=F32)


def _matmul(a, w, tm, tn):
    m, k = a.shape
    n = w.shape[1]
    return pl.pallas_call(
        _mm_kernel,
        grid=(n // tn, m // tm),
        in_specs=[
            pl.BlockSpec((tm, k), lambda j, i: (i, 0)),
            pl.BlockSpec((k, tn), lambda j, i: (0, j)),
        ],
        out_specs=pl.BlockSpec((tm, tn), lambda j, i: (i, j)),
        out_shape=jax.ShapeDtypeStruct((m, n), F32),
        compiler_params=_cparams(("parallel", "arbitrary")),
    )(a, w)


def _split_dot(x, ones_bf16):
    hi = x.astype(BF16)
    lo = (x - hi.astype(F32)).astype(BF16)
    return (jnp.dot(hi, ones_bf16, preferred_element_type=F32)
            + jnp.dot(lo, ones_bf16, preferred_element_type=F32))


def _rwkv_kernel(z_ref, shift0_ref, s0_ref, mu_ref, w0_ref, w2_ref, a0_ref, a2_ref, kk_ref, ka_ref,
                 rk_ref, gnw_ref, gnb_ref,
                 ya_ref, sT_ref,
                 nkk_ref, w_ref, b_ref, k_ref, r_ref, v_ref, y_ref, bonus_ref,
                 *, n_seq):
    t_len = SUBLANES
    tc = n_seq * t_len
    heads_per_quad = QUAD_LANES // A_HEAD_DIM

    lane = lax.broadcasted_iota(jnp.int32, (QUAD_LANES, QUAD_LANES), 1)
    row = lax.broadcasted_iota(jnp.int32, (QUAD_LANES, QUAD_LANES), 0)
    head_ones = (lane // A_HEAD_DIM == row // A_HEAD_DIM).astype(BF16)
    diag = (lane % A_HEAD_DIM) == (row % A_HEAD_DIM)

    z = z_ref[:, :A_SHIFT_WIDTH]
    prev = pltpu.roll(z, 1, axis=0)
    row_id = lax.broadcasted_iota(jnp.int32, (tc, 1), 0)
    for g in range(n_seq):
        prev = jnp.where(row_id == g * t_len, shift0_ref[g], prev)
    zs = z + (prev - z) * mu_ref[...]

    lora = zs[:, 3 * A_WIDTH:]
    w_lo = jnp.tanh(lora[:, :LORA_RANK]).astype(BF16)
    a_lo = lora[:, LORA_RANK:].astype(BF16)
    lw = w0_ref[...] + jnp.dot(w_lo, w2_ref[...].astype(BF16), preferred_element_type=F32)
    nlw = -lw
    log_w = -(jnp.maximum(nlw, 0.0) + jnp.log1p(jnp.exp(-jnp.abs(nlw)))) - 0.5
    w_ref[...] = jnp.exp(-jnp.exp(log_w))
    a = _sigmoid(a0_ref[...] + jnp.dot(a_lo, a2_ref[...].astype(BF16), preferred_element_type=F32))

    for q in range(A_QUADS):
        sl = slice(q * QUAD_LANES, (q + 1) * QUAD_LANES)
        r_q = zs[:, q * QUAD_LANES:(q + 1) * QUAD_LANES]
        k_q = zs[:, A_WIDTH + q * QUAD_LANES:A_WIDTH + (q + 1) * QUAD_LANES]
        v_q = zs[:, 2 * A_WIDTH + q * QUAD_LANES:2 * A_WIDTH + (q + 1) * QUAD_LANES]
        a_q = a[:, sl]
        kk = k_q * kk_ref[:, sl]
        n2 = _split_dot(kk * kk, head_ones)
        kk = kk / jnp.maximum(jnp.sqrt(n2), 1e-12)
        k_mod = k_q * (1.0 + (a_q - 1.0) * ka_ref[:, sl])
        nkk_ref[:, sl] = -kk
        b_ref[:, sl] = kk * a_q
        k_ref[:, sl] = k_mod
        r_ref[:, sl] = r_q
        v_ref[:, sl] = v_q
        bonus_ref[:, sl] = _split_dot(r_q * k_mod * rk_ref[:, sl], head_ones) * v_q

    row8 = lax.broadcasted_iota(jnp.int32, (SUBLANES, QUAD_LANES), 0)

    def expand(tiles, u):
        return jnp.concatenate(
            [jnp.broadcast_to(t[u:u + 1], (A_HEAD_DIM, QUAD_LANES)) for t in tiles], axis=0)

    seqs = range(n_seq)
    tiles = lambda ref, g: [ref[g * t_len:(g + 1) * t_len, q * QUAD_LANES:(q + 1) * QUAD_LANES]
                            for q in range(A_QUADS)]
    vecs = [[tiles(ref, g) for ref in (nkk_ref, w_ref, b_ref, k_ref, r_ref, v_ref)] for g in seqs]
    s = [jnp.concatenate([jnp.concatenate([s0_ref[g, q * heads_per_quad + h] for h in range(heads_per_quad)],
                                          axis=1) for q in range(A_QUADS)], axis=0) for g in seqs]
    ytiles = [[jnp.zeros((SUBLANES, QUAD_LANES), F32) for _ in range(A_QUADS)] for _ in seqs]
    for u in range(t_len):
        for g in seqs:
            nkk8, w8, b8, k8, r8, v8 = vecs[g]
            sa = jnp.dot((s[g] * expand(nkk8, u)).astype(BF16), head_ones, preferred_element_type=F32)
            vcol = jnp.dot(jnp.where(diag, expand(v8, u), 0.0).astype(BF16), head_ones,
                           preferred_element_type=F32)
            s[g] = s[g] * expand(w8, u) + sa * expand(b8, u) + vcol * expand(k8, u)
            ycol = jnp.dot((s[g] * expand(r8, u)).astype(BF16), head_ones, preferred_element_type=F32)
            ysel = jnp.where(diag, ycol, 0.0)
            for q in range(A_QUADS):
                yrow = jnp.sum(ysel[q * A_HEAD_DIM:(q + 1) * A_HEAD_DIM], axis=0, keepdims=True)
                ytiles[g][q] = jnp.where(row8 == u, yrow, ytiles[g][q])
    for g in seqs:
        for q in range(A_QUADS):
            y_ref[g * t_len:(g + 1) * t_len, q * QUAD_LANES:(q + 1) * QUAD_LANES] = ytiles[g][q]
        for hd in range(A_HEADS):
            q, h = divmod(hd, heads_per_quad)
            sT_ref[g, hd] = s[g][q * A_HEAD_DIM:(q + 1) * A_HEAD_DIM, h * A_HEAD_DIM:(h + 1) * A_HEAD_DIM]

    for q in range(A_QUADS):
        sl = slice(q * QUAD_LANES, (q + 1) * QUAD_LANES)
        y = y_ref[:, sl]
        mean = _split_dot(y, head_ones) * (1.0 / A_HEAD_DIM)
        yc = y - mean
        var = _split_dot(yc * yc, head_ones) * (1.0 / A_HEAD_DIM)
        yn = yc * lax.rsqrt(var + GN_EPS) * gnw_ref[:, sl] + gnb_ref[:, sl]
        yn = yn + bonus_ref[:, sl]
        g = z_ref[:, A_SHIFT_WIDTH + q * QUAD_LANES:A_SHIFT_WIDTH + (q + 1) * QUAD_LANES]
        ya_ref[:, sl] = (yn * (g * _sigmoid(g))).astype(BF16)


def _rwkv_short(za, row0, n_seq, shift0, s0, mu, w0, w2, a0, a2, k_k, k_a, r_k, gn_w, gn_b):
    per_step = RWKV_SEQS_PER_STEP
    tr = per_step * SUBLANES
    vec = lambda n: pl.BlockSpec((1, n), lambda b: (0, 0))
    state_spec = pl.BlockSpec((per_step, A_HEADS, A_HEAD_DIM, A_HEAD_DIM), lambda b: (b, 0, 0, 0))
    return pl.pallas_call(
        functools.partial(_rwkv_kernel, n_seq=per_step),
        grid=(n_seq // per_step,),
        in_specs=[
            pl.BlockSpec((tr, ZA_WIDTH), lambda b: (row0 // tr + b, 0)),
            pl.BlockSpec((per_step, 1, A_SHIFT_WIDTH), lambda b: (b, 0, 0)),
            state_spec,
            vec(A_SHIFT_WIDTH), vec(A_WIDTH),
            pl.BlockSpec((LORA_RANK, A_WIDTH), lambda b: (0, 0)),
            vec(A_WIDTH),
            pl.BlockSpec((LORA_RANK, A_WIDTH), lambda b: (0, 0)),
            vec(A_WIDTH), vec(A_WIDTH), vec(A_WIDTH), vec(A_WIDTH), vec(A_WIDTH),
        ],
        out_specs=[
            pl.BlockSpec((tr, A_WIDTH), lambda b: (b, 0)),
            state_spec,
        ],
        out_shape=[
            jax.ShapeDtypeStruct((n_seq * SUBLANES, A_WIDTH), BF16),
            jax.ShapeDtypeStruct((n_seq, A_HEADS, A_HEAD_DIM, A_HEAD_DIM), F32),
        ],
        scratch_shapes=[pltpu.VMEM((tr, A_WIDTH), F32) for _ in range(8)],
        compiler_params=_cparams(("arbitrary",)),
    )(za, shift0, s0, mu, w0, w2, a0, a2, k_k, k_a, r_k, gn_w, gn_b)


def _head_blocks(y, same_head):
    return jnp.where(same_head, jnp.concatenate([y, y, y, y], axis=0), jnp.zeros((), y.dtype))


def _off_block(t, s, b):
    return jnp.logical_and(t // (2 * b) == s // (2 * b), jnp.logical_and(t % (2 * b) >= b, s % (2 * b) < b))


def _split2(x):
    hi = x.astype(BF16)
    return hi, (x - hi.astype(F32)).astype(BF16)


def _head_matmul3(x, ys, same_head):
    xh, xl = _split2(x)
    outs = []
    for y in ys:
        yh, yl = _split2(y)
        bh = _head_blocks(yh, same_head)
        out = jnp.dot(xh, bh, preferred_element_type=F32)
        out += jnp.dot(xl, bh, preferred_element_type=F32)
        out += jnp.dot(xh, _head_blocks(yl, same_head), preferred_element_type=F32)
        outs.append(out)
    return outs


def _head_matmul(x, y, same_head):
    return jnp.dot(x.astype(BF16), _head_blocks(y.astype(BF16), same_head), preferred_element_type=F32)


def _rwkv_chunk_kernel(z_ref, shift0_ref, s0_ref, mu_ref, w0_ref, w2_ref, a0_ref, a2_ref, kk_ref, ka_ref,
                       rk_ref, gnw_ref, gnb_ref,
                       ya_ref, sT_ref, carry_ref, st_ref):
    c = pl.program_id(1)
    n_c = pl.num_programs(1)
    tc = CHUNK
    rows_all = tc * CHUNKS_PER_STEP

    @pl.when(c == 0)
    def _():
        carry_ref[...] = shift0_ref[0]
        st_ref[...] = s0_ref[0]

    lane = lax.broadcasted_iota(jnp.int32, (QUAD_LANES, QUAD_LANES), 1)
    row = lax.broadcasted_iota(jnp.int32, (QUAD_LANES, QUAD_LANES), 0)
    same_head = lane // A_HEAD_DIM == row // A_HEAD_DIM
    head_ones = same_head.astype(BF16)
    t_idx = lax.broadcasted_iota(jnp.int32, (tc, QUAD_LANES), 0)
    lane_t = lax.broadcasted_iota(jnp.int32, (tc, QUAD_LANES), 1)
    s_idx = lane_t % A_HEAD_DIM
    strict = s_idx < t_idx
    incl = s_idx <= t_idx
    eye = (s_idx == t_idx).astype(F32)
    lane_head = lane_t // A_HEAD_DIM
    tri = (lax.broadcasted_iota(jnp.int32, (tc, tc), 1) <= lax.broadcasted_iota(jnp.int32, (tc, tc), 0)
           ).astype(BF16)

    z = z_ref[:, :A_SHIFT_WIDTH]
    prev = pltpu.roll(z, 1, axis=0)
    first = lax.broadcasted_iota(jnp.int32, (rows_all, 1), 0) == 0
    prev = jnp.where(first, carry_ref[...], prev)
    carry_ref[...] = z[rows_all - 1:rows_all, :]
    zs = z + (prev - z) * mu_ref[...]

    lora = zs[:, 3 * A_WIDTH:]
    w_lo = jnp.tanh(lora[:, :LORA_RANK]).astype(BF16)
    a_lo = lora[:, LORA_RANK:].astype(BF16)
    lw = w0_ref[...] + jnp.dot(w_lo, w2_ref[...].astype(BF16), preferred_element_type=F32)
    nlw = -lw
    log_w = -(jnp.maximum(nlw, 0.0) + jnp.log1p(jnp.exp(-jnp.abs(nlw)))) - 0.5
    log_decay = -jnp.exp(log_w)
    a = _sigmoid(a0_ref[...] + jnp.dot(a_lo, a2_ref[...].astype(BF16), preferred_element_type=F32))

    quads = range(A_QUADS)
    lanes = [slice(q * QUAD_LANES, (q + 1) * QUAD_LANES) for q in quads]

    jobs = [(sub, q) for sub in range(CHUNKS_PER_STEP) for q in quads]
    n_jobs = range(len(jobs))
    job_rows = lambda stacked, j: stacked[j * tc:(j + 1) * tc]

    def raw(job):
        sub, q = job
        sl = lanes[q]
        rs = slice(sub * tc, (sub + 1) * tc)
        r_q = zs[rs, q * QUAD_LANES:(q + 1) * QUAD_LANES]
        k_q = zs[rs, A_WIDTH + q * QUAD_LANES:A_WIDTH + (q + 1) * QUAD_LANES]
        v_q = zs[rs, 2 * A_WIDTH + q * QUAD_LANES:2 * A_WIDTH + (q + 1) * QUAD_LANES]
        a_q = a[rs, sl]
        kk = k_q * kk_ref[:, sl]
        k_mod = k_q * (1.0 + (a_q - 1.0) * ka_ref[:, sl])
        return dict(r=r_q, v=v_q, a=a_q, kk=kk, k_mod=k_mod, rkk=r_q * k_mod * rk_ref[:, sl])

    rw = [raw(job) for job in jobs]
    sums = _split_dot(jnp.concatenate([t["kk"] * t["kk"] for t in rw] + [t["rkk"] for t in rw], axis=0), head_ones)
    cums = []
    for sub in range(CHUNKS_PER_STEP):
        ld = log_decay[sub * tc:(sub + 1) * tc, :]
        h1, h2 = _split2(ld)
        h3 = (ld - h1.astype(F32) - h2.astype(F32)).astype(BF16)
        cums.append(jnp.dot(tri, h1, preferred_element_type=F32) + jnp.dot(tri, h2, preferred_element_type=F32)
                    + jnp.dot(tri, h3, preferred_element_type=F32))
    cum_of = lambda job: cums[job[0]][:, lanes[job[1]]]
    p_cols = _split_dot(jnp.concatenate(
        [jnp.where(eye > 0, jnp.exp(cum_of(job)[tc - 1:tc, :]), 0.0) for job in jobs], axis=0), head_ones)

    def tokens(j):
        job, t = jobs[j], rw[j]
        sub, q = job
        kk = t["kk"] / jnp.maximum(jnp.sqrt(job_rows(sums, j)), 1e-12)
        k_mod = t["k_mod"]
        beta = kk * t["a"]
        bonus = job_rows(sums, len(jobs) + j) * t["v"]
        cum = cum_of(job)
        ld = log_decay[sub * tc:(sub + 1) * tc, lanes[q]]
        cum_last = cum[tc - 1:tc, :]
        alpha_p = -kk * jnp.exp(cum - ld)
        r_p = t["r"] * jnp.exp(cum)
        inv_p = jnp.exp(-cum)
        to_end = jnp.exp(cum_last - cum)
        lhs = jnp.concatenate([alpha_p, r_p], axis=0).astype(BF16)
        g_b = _dot_nt(lhs, _head_blocks((beta * inv_p).astype(BF16), same_head))
        g_k = _dot_nt(lhs, _head_blocks((k_mod * inv_p).astype(BF16), same_head))
        lhs_t = jnp.concatenate([beta * to_end, k_mod * to_end], axis=0).astype(BF16)
        return dict(v=t["v"], bonus=bonus, ar_p=lhs,
                    a_ab=jnp.where(strict, g_b[:tc], 0.0), a_rb=jnp.where(incl, g_b[tc:], 0.0),
                    a_ak=jnp.where(strict, g_k[:tc], 0.0), a_rk=jnp.where(incl, g_k[tc:], 0.0),
                    lhs_t=lhs_t, p_col=job_rows(p_cols, j))

    tk = [tokens(j) for j in n_jobs]

    inv = [eye + jnp.where(_off_block(t_idx, s_idx, 1), t["a_ab"], 0.0) for t in tk]
    b = 2
    while b < tc:
        off = _off_block(t_idx, s_idx, b)
        mid = [_head_matmul3(jnp.where(off, tk[j]["a_ab"], 0.0), [inv[j]], same_head)[0] for j in n_jobs]
        inv = [inv[j] + _head_matmul3(inv[j], [mid[j]], same_head)[0] for j in n_jobs]
        b *= 2
    av = [_head_matmul(jnp.concatenate([t["a_ak"], t["a_rk"]], axis=0), t["v"], same_head) for t in tk]
    w1 = [m[:tc] for m in av]
    y_v = [m[tc:] for m in av]

    st = [st_ref[q] for q in quads]
    for sub in range(CHUNKS_PER_STEP):
        js = [sub * A_QUADS + q for q in quads]
        st_blocks = [_head_blocks(s.astype(BF16), same_head) for s in st]
        from_st = [jnp.dot(tk[j]["ar_p"], st_blocks[q], preferred_element_type=F32)
                   for q, j in zip(quads, js)]
        rhs = [w1[j] + from_st[q][:tc] for q, j in zip(quads, js)]
        u = [_head_matmul3(inv[j], [rhs[q]], same_head)[0] for q, j in zip(quads, js)]
        new_st, ys = [], []
        for q, j in zip(quads, js):
            t = tk[j]
            ys.append(from_st[q][tc:] + _head_matmul(t["a_rb"], u[q], same_head) + y_v[j])
            rhs_t = jnp.concatenate([u[q], t["v"]], axis=0).astype(BF16)
            cross = lax.dot_general(t["lhs_t"], rhs_t, (((0,), (0,)), ((), ())), preferred_element_type=F32)
            new = t["p_col"] * st[q]
            for h in range(QUAD_LANES // A_HEAD_DIM):
                new += jnp.where(lane_head == h, cross[h * A_HEAD_DIM:(h + 1) * A_HEAD_DIM], 0.0)
            new_st.append(new)
        st = new_st

        y_all = jnp.concatenate(ys, axis=0)
        yc_all = y_all - _split_dot(y_all, head_ones) * (1.0 / A_HEAD_DIM)
        var_all = _split_dot(yc_all * yc_all, head_ones) * (1.0 / A_HEAD_DIM)
        rs = slice(sub * tc, (sub + 1) * tc)
        for q, j in zip(quads, js):
            sl = lanes[q]
            yn = job_rows(yc_all, q) * lax.rsqrt(job_rows(var_all, q) + GN_EPS) * gnw_ref[:, sl] + gnb_ref[:, sl]
            yn = yn + tk[j]["bonus"]
            g = z_ref[rs, A_SHIFT_WIDTH + q * QUAD_LANES:A_SHIFT_WIDTH + (q + 1) * QUAD_LANES]
            ya_ref[rs, sl] = (yn * (g * _sigmoid(g))).astype(BF16)
    for q in quads:
        st_ref[q] = st[q]

    @pl.when(c == n_c - 1)
    def _():
        sT_ref[0] = st_ref[...]


def _rwkv_chunked(za, t_len, shift0, s0, mu, w0, w2, a0, a2, k_k, k_a, r_k, gn_w, gn_b):
    rows = CHUNK * CHUNKS_PER_STEP
    n_c = t_len // rows
    vec = lambda n: pl.BlockSpec((1, n), lambda b, c: (0, 0))
    state_spec = pl.BlockSpec((1, A_QUADS, A_HEAD_DIM, QUAD_LANES), lambda b, c: (b, 0, 0, 0))
    return pl.pallas_call(
        _rwkv_chunk_kernel,
        grid=(1, n_c),
        in_specs=[
            pl.BlockSpec((rows, ZA_WIDTH), lambda b, c: (c, 0)),
            pl.BlockSpec((1, 1, A_SHIFT_WIDTH), lambda b, c: (b, 0, 0)),
            state_spec,
            vec(A_SHIFT_WIDTH), vec(A_WIDTH),
            pl.BlockSpec((LORA_RANK, A_WIDTH), lambda b, c: (0, 0)),
            vec(A_WIDTH),
            pl.BlockSpec((LORA_RANK, A_WIDTH), lambda b, c: (0, 0)),
            vec(A_WIDTH), vec(A_WIDTH), vec(A_WIDTH), vec(A_WIDTH), vec(A_WIDTH),
        ],
        out_specs=[
            pl.BlockSpec((rows, A_WIDTH), lambda b, c: (c, 0)),
            state_spec,
        ],
        out_shape=[
            jax.ShapeDtypeStruct((t_len, A_WIDTH), BF16),
            jax.ShapeDtypeStruct((1, A_QUADS, A_HEAD_DIM, QUAD_LANES), F32),
        ],
        scratch_shapes=[pltpu.VMEM((1, A_SHIFT_WIDTH), F32),
                        pltpu.VMEM((A_QUADS, A_HEAD_DIM, QUAD_LANES), F32)],
        compiler_params=_cparams(("arbitrary", "arbitrary")),
    )(za, shift0, s0, mu, w0, w2, a0, a2, k_k, k_a, r_k, gn_w, gn_b)


def _keymajor_to_state(s):
    b = s.shape[0]
    return s.reshape(b, A_QUADS, A_HEAD_DIM, 4, A_HEAD_DIM).transpose(0, 1, 3, 4, 2).reshape(
        b, A_HEADS, A_HEAD_DIM, A_HEAD_DIM)


def _norm_rope(x, gain, cos, sin):
    y = x * lax.rsqrt(jnp.mean(x * x, axis=-1, keepdims=True) + NORM_EPS) * gain
    return y * cos + pltpu.roll(y, B_HEAD_DIM // 2, axis=1) * sin


def _dot_nt(a, b):
    return lax.dot_general(a, b, (((1,), (1,)), ((), ())), preferred_element_type=F32)


def _attn_prompt_kernel(q_ref, k_ref, v_ref, cos_ref, sin_ref, qg_ref, kg_ref,
                        o_ref, lse_ref, kr_ref, kprev_ref, vprev_ref, qs_ref, *, dil, n_sub):
    i = pl.program_id(0)
    hd = pl.program_id(1)
    blk = N_BACK

    @pl.when(i == 0)
    def _():
        kprev_ref[hd] = jnp.zeros((dil, blk, B_HEAD_DIM), BF16)
        vprev_ref[hd] = jnp.zeros((dil, blk, B_HEAD_DIM), BF16)

    cos = cos_ref[...]
    sin = sin_ref[...]
    qs_ref[...] = _norm_rope(q_ref[...], qg_ref[...], cos, sin)
    kr_ref[...] = _norm_rope(k_ref[...], kg_ref[...], cos, sin)

    rowi = lax.broadcasted_iota(jnp.int32, (blk, blk), 0)
    coli = lax.broadcasted_iota(jnp.int32, (blk, blk), 1)
    cur_ok = coli <= rowi
    scale = B_HEAD_DIM ** -0.5
    span = lambda sub, rho: pl.ds(sub * blk * dil + rho, blk, stride=dil)
    units = [(sub, rho) for sub in range(n_sub) for rho in range(dil)]
    for u0 in range(0, len(units), ATT_RESIDUE_GROUP):
        group = units[u0:u0 + ATT_RESIDUE_GROUP]
        rows = [span(sub, rho) for sub, rho in group]
        qn = [qs_ref[r, :].astype(BF16) for r in rows]
        kb = [kr_ref[r, :].astype(BF16) for r in rows]
        vb = [v_ref[r, :].astype(BF16) for r in rows]
        kp = [kr_ref[span(sub - 1, rho), :].astype(BF16) if sub > 0 else kprev_ref[hd, rho] for sub, rho in group]
        vp = [v_ref[span(sub - 1, rho), :].astype(BF16) if sub > 0 else vprev_ref[hd, rho] for sub, rho in group]
        prev_ok = [coli >= rowi if sub > 0 else jnp.logical_and(coli >= rowi, i > 0) for sub, _ in group]
        s_cur = [jnp.where(cur_ok, _dot_nt(q, k) * scale, -jnp.inf) for q, k in zip(qn, kb)]
        s_prev = [jnp.where(ok, _dot_nt(q, k) * scale, -jnp.inf) for ok, q, k in zip(prev_ok, qn, kp)]
        m = [jnp.max(jnp.maximum(c, p), axis=-1, keepdims=True) for c, p in zip(s_cur, s_prev)]
        p_cur = [jnp.exp(c - mm) for c, mm in zip(s_cur, m)]
        p_prev = [jnp.exp(p - mm) for p, mm in zip(s_prev, m)]
        den = [jnp.sum(c + p, axis=-1, keepdims=True) for c, p in zip(p_cur, p_prev)]
        for n, r in enumerate(rows):
            o = jnp.dot((p_cur[n] / den[n]).astype(BF16), vb[n], preferred_element_type=F32)
            o += jnp.dot((p_prev[n] / den[n]).astype(BF16), vp[n], preferred_element_type=F32)
            o_ref[r, :] = o
            lse_ref[r, :] = jnp.broadcast_to(m[n] + jnp.log(den[n]), (blk, B_HEAD_DIM))
    for rho in range(dil):
        kprev_ref[hd, rho] = kr_ref[span(n_sub - 1, rho), :].astype(BF16)
        vprev_ref[hd, rho] = v_ref[span(n_sub - 1, rho), :].astype(BF16)


def _attn_prompt(zb, gi, dil, seq, cos, sin, q_gain, k_gain):
    n_sub = max(1, ATT_PROMPT_ROWS // (N_BACK * dil))
    tr = N_BACK * dil * n_sub
    heads_qkv = B_QKV_WIDTH // B_HEAD_DIM
    blk = lambda which: pl.BlockSpec(
        (tr, B_HEAD_DIM), lambda i, hd: (i, which * heads_qkv + gi * HEADS_PER_GROUP + hd))
    tab = pl.BlockSpec((tr, B_HEAD_DIM), lambda i, hd: (i, 0))
    gain = pl.BlockSpec((1, B_HEAD_DIM), lambda i, hd: (0, 0))
    out = pl.BlockSpec((tr, B_HEAD_DIM), lambda i, hd: (i, hd))
    shp = jax.ShapeDtypeStruct((seq, B_GROUP_WIDTH), F32)
    prev = pltpu.VMEM((HEADS_PER_GROUP, dil, N_BACK, B_HEAD_DIM), BF16)
    return pl.pallas_call(
        functools.partial(_attn_prompt_kernel, dil=dil, n_sub=n_sub),
        grid=(seq // tr, HEADS_PER_GROUP),
        in_specs=[blk(0), blk(1), blk(2), tab, tab, gain, gain],
        out_specs=[out, out, out],
        out_shape=[shp, shp, shp],
        scratch_shapes=[prev, prev, pltpu.VMEM((tr, B_HEAD_DIM), F32)],
        compiler_params=_cparams(("arbitrary", "arbitrary")),
    )(zb, zb, zb, cos, sin, q_gain, k_gain)


def _attn_sample_kernel(q_ref, k_ref, v_ref, cache_ref, cos_ref, sin_ref, qg_ref, kg_ref,
                        o_ref, lse_ref, new_ref, *, window, dil, t_new, n_seq):
    per_pos = 2 * HEADS_PER_GROUP
    cos = cos_ref[...]
    sin = sin_ref[...]
    scale = B_HEAD_DIM ** -0.5
    tq = lax.broadcasted_iota(jnp.int32, (t_new, window), 0)
    cc = lax.broadcasted_iota(jnp.int32, (t_new, window), 1)
    cache_ok = jnp.logical_and(((cc - tq + dil * N_BACK) & (dil - 1)) == 0, cc >= tq)
    tn = lax.broadcasted_iota(jnp.int32, (t_new, t_new), 0)
    un = lax.broadcasted_iota(jnp.int32, (t_new, t_new), 1)
    new_ok = jnp.logical_and(((tn - un + dil * N_BACK) & (dil - 1)) == 0, un <= tn)
    for g in range(n_seq):
        new_ref[g, 0:(window - t_new) * per_pos, :] = cache_ref[g, t_new * per_pos:window * per_pos, :]
    units = [(g, h) for g in range(n_seq) for h in range(HEADS_PER_GROUP)]
    for u0 in range(0, len(units), HEADS_PER_GROUP):
        group = units[u0:u0 + HEADS_PER_GROUP]
        rows = [slice(g * t_new, (g + 1) * t_new) for g, _ in group]
        cols = [slice(h * B_HEAD_DIM, (h + 1) * B_HEAD_DIM) for _, h in group]
        qn = [_norm_rope(q_ref[r, c], qg_ref[...], cos, sin).astype(BF16) for r, c in zip(rows, cols)]
        kr = [_norm_rope(k_ref[r, c], kg_ref[...], cos, sin) for r, c in zip(rows, cols)]
        v_new = [v_ref[r, c] for r, c in zip(rows, cols)]
        kc = [cache_ref[g, pl.ds(h, window, stride=per_pos), :].astype(BF16) for g, h in group]
        vc = [cache_ref[g, pl.ds(HEADS_PER_GROUP + h, window, stride=per_pos), :].astype(BF16) for g, h in group]
        s_c = [jnp.where(cache_ok, _dot_nt(q, k) * scale, -jnp.inf) for q, k in zip(qn, kc)]
        s_n = [jnp.where(new_ok, _dot_nt(q, k.astype(BF16)) * scale, -jnp.inf) for q, k in zip(qn, kr)]
        m = [jnp.maximum(jnp.max(c, axis=-1, keepdims=True), jnp.max(n, axis=-1, keepdims=True))
             for c, n in zip(s_c, s_n)]
        p_c = [jnp.exp(c - mm) for c, mm in zip(s_c, m)]
        p_n = [jnp.exp(n - mm) for n, mm in zip(s_n, m)]
        den = [jnp.sum(c, axis=-1, keepdims=True) + jnp.sum(n, axis=-1, keepdims=True) for c, n in zip(p_c, p_n)]
        for i, (g, h) in enumerate(group):
            o = jnp.dot((p_c[i] / den[i]).astype(BF16), vc[i], preferred_element_type=F32)
            o += jnp.dot((p_n[i] / den[i]).astype(BF16), v_new[i].astype(BF16), preferred_element_type=F32)
            o_ref[rows[i], cols[i]] = o
            lse_ref[rows[i], cols[i]] = jnp.broadcast_to(m[i] + jnp.log(den[i]), (t_new, B_HEAD_DIM))
            base = (window - t_new) * per_pos
            new_ref[g, pl.ds(base + h, t_new, stride=per_pos), :] = kr[i]
            new_ref[g, pl.ds(base + HEADS_PER_GROUP + h, t_new, stride=per_pos), :] = v_new[i]


def _attn_sample(zb, gi, window, dil, row0, n_seq, t_new, cache, cos, sin, q_gain, k_gain):
    rows = window * 2 * HEADS_PER_GROUP
    per_step = max(1, min(n_seq, ATT_SAMPLE_WINDOW_ROWS // window))
    tr = per_step * t_new
    cache2 = cache.reshape(n_seq, rows, B_HEAD_DIM)
    blk = lambda off: pl.BlockSpec((tr, B_GROUP_WIDTH), lambda b: (row0 // tr + b, off + gi))
    tab = pl.BlockSpec((t_new, B_HEAD_DIM), lambda b: (0, 0))
    gain = pl.BlockSpec((1, B_HEAD_DIM), lambda b: (0, 0))
    out = pl.BlockSpec((tr, B_GROUP_WIDTH), lambda b: (b, 0))
    win = pl.BlockSpec((per_step, rows, B_HEAD_DIM), lambda b: (b, 0, 0))
    shp = jax.ShapeDtypeStruct((n_seq * t_new, B_GROUP_WIDTH), F32)
    o, lse, new = pl.pallas_call(
        functools.partial(_attn_sample_kernel, window=window, dil=dil, t_new=t_new, n_seq=per_step),
        grid=(n_seq // per_step,),
        in_specs=[blk(0), blk(3), blk(6), win, tab, tab, gain, gain],
        out_specs=[out, out, win],
        out_shape=[shp, shp, jax.ShapeDtypeStruct((n_seq, rows, B_HEAD_DIM), F32)],
        compiler_params=_cparams(("arbitrary",)),
    )(zb, zb, zb, cache2, cos, sin, q_gain, k_gain)
    return o, lse, new.reshape(cache.shape)


def _combine_kernel(o0, o1, o2, l0, l1, l2, g_ref, y_ref):
    a, b, c = l0[...], l1[...], l2[...]
    m = jnp.maximum(jnp.maximum(a, b), c)
    ea, eb, ec = jnp.exp(a - m), jnp.exp(b - m), jnp.exp(c - m)
    o = (ea * o0[...] + eb * o1[...] + ec * o2[...]) / (ea + eb + ec)
    g = g_ref[...]
    y_ref[...] = (o * (g * _sigmoid(g))).astype(BF16)


def _combine(outs, lses, zc, row_block0, tm):
    rows = outs[0].shape[0]
    spec = pl.BlockSpec((tm, B_GROUP_WIDTH), lambda i: (i, 0))
    return pl.pallas_call(
        _combine_kernel,
        grid=(rows // tm,),
        in_specs=[spec] * 6 + [pl.BlockSpec((tm, B_GROUP_WIDTH), lambda i: (row_block0 + i, 0))],
        out_specs=spec,
        out_shape=jax.ShapeDtypeStruct((rows, B_GROUP_WIDTH), BF16),
        compiler_params=_cparams(("parallel",)),
    )(*outs, *lses, zc)


def _merge_kernel(ya_ref, yb_ref, ma_ref, mb_ref, wa_ref, wb_ref, o_ref):
    ua = jnp.dot(ya_ref[...], wa_ref[...], preferred_element_type=F32)
    ub = jnp.dot(yb_ref[...], wb_ref[...], preferred_element_type=F32)
    o_ref[...] = (_sigmoid(ma_ref[...]) * ua + _sigmoid(mb_ref[...]) * ub).astype(BF16)


def _merge(ya, yb, zc, row_block0, wa, wb, tm, tn):
    m = ya.shape[0]
    nb = D_MODEL // tn
    off_a = B_GROUP_WIDTH // tn
    off_b = (B_GROUP_WIDTH + D_MODEL) // tn
    assert off_a * tn == B_GROUP_WIDTH
    return pl.pallas_call(
        _merge_kernel,
        grid=(m // tm, nb),
        in_specs=[
            pl.BlockSpec((tm, A_WIDTH), lambda i, j: (i, 0)),
            pl.BlockSpec((tm, B_GROUP_WIDTH), lambda i, j: (i, 0)),
            pl.BlockSpec((tm, tn), lambda i, j: (row_block0 + i, off_a + j)),
            pl.BlockSpec((tm, tn), lambda i, j: (row_block0 + i, off_b + j)),
            pl.BlockSpec((A_WIDTH, tn), lambda i, j: (0, j)),
            pl.BlockSpec((B_GROUP_WIDTH, tn), lambda i, j: (0, j)),
        ],
        out_specs=pl.BlockSpec((tm, tn), lambda i, j: (i, j)),
        out_shape=jax.ShapeDtypeStruct((m, D_MODEL), BF16),
        compiler_params=_cparams(("parallel", "arbitrary")),
    )(ya, yb, zc, zc, wa, wb)


def _outproj_kernel(x_ref, a_ref, w_ref, o_ref):
    o_ref[...] = x_ref[...] + jnp.dot(a_ref[...], w_ref[...], preferred_element_type=F32)


def _outproj(x, merged, w, tm, tn):
    m = x.shape[0]
    return pl.pallas_call(
        _outproj_kernel,
        grid=(D_MODEL // tn, m // tm),
        in_specs=[
            pl.BlockSpec((tm, tn), lambda j, i: (i, j)),
            pl.BlockSpec((tm, D_MODEL), lambda j, i: (i, 0)),
            pl.BlockSpec((D_MODEL, tn), lambda j, i: (0, j)),
        ],
        out_specs=pl.BlockSpec((tm, tn), lambda j, i: (i, j)),
        out_shape=jax.ShapeDtypeStruct((m, D_MODEL), F32),
        compiler_params=_cparams(("parallel", "arbitrary")),
    )(x, merged, w)


def _ple_kernel(h_ref, p_ref, wg_ref, wp_ref, o_ref):
    h = h_ref[...]
    gate = _sigmoid(jnp.dot(h.astype(BF16), wg_ref[...], preferred_element_type=F32))
    proj = jnp.dot(p_ref[...].astype(BF16), wp_ref[...], preferred_element_type=F32)
    o_ref[...] = h + gate * proj


def _ple(h, p, wg, wp, tm):
    m = h.shape[0]
    pd = p.shape[1]
    return pl.pallas_call(
        _ple_kernel,
        grid=(m // tm,),
        in_specs=[
            pl.BlockSpec((tm, D_MODEL), lambda i: (i, 0)),
            pl.BlockSpec((tm, pd), lambda i: (i, 0)),
            pl.BlockSpec((D_MODEL, D_MODEL), lambda i: (0, 0)),
            pl.BlockSpec((pd, D_MODEL), lambda i: (0, 0)),
        ],
        out_specs=pl.BlockSpec((tm, D_MODEL), lambda i: (i, 0)),
        out_shape=jax.ShapeDtypeStruct((m, D_MODEL), F32),
        compiler_params=_cparams(("parallel",)),
    )(h, p, wg, wp)


def _rope_tables(pos):
    half = B_HEAD_DIM // 2
    inv = ROPE_THETA ** (-jnp.arange(half, dtype=F32) / half)
    ang = pos.astype(F32)[:, None] * inv[None, :]
    cos, sin = jnp.cos(ang), jnp.sin(ang)
    return jnp.concatenate([cos, cos], axis=1), jnp.concatenate([-sin, sin], axis=1)


def kernel(x_prompt, x_sample, state_shift, state_wkv, cache_kv_w128, cache_kv_w512, cache_kv_w2048,
           p_prompt, p_sample, ln_g, w_in, mu, w0, w2, a0, a2, k_k, k_a, r_k, gn_w, gn_b,
           q_gain, k_gain, w_up_a, w_up_b, w_out, w_ple_gate, w_ple_proj):
    depth = ln_g.shape[0]
    assert depth == 1
    bp, seq, d = x_prompt.shape
    bs, t_new, _ = x_sample.shape
    assert bp == 1 and d == D_MODEL
    rs = bs * t_new
    caches = (cache_kv_w128, cache_kv_w512, cache_kv_w2048)

    xp = x_prompt.reshape(seq, d)
    xs = x_sample.reshape(rs, d)
    hn = _rmsnorm(xp, xs, ln_g)

    w_in0 = w_in[0]
    off_b = ZA_WIDTH
    off_c = ZA_WIDTH + ZB_WIDTH
    wa_in = w_in0[:, :off_b].astype(BF16)
    wb_in = w_in0[:, off_b:off_c].astype(BF16)
    wc_in = w_in0[:, off_c:].astype(BF16)
    tm_in = 768
    za = _matmul(hn, wa_in, tm_in, ZA_WIDTH // 3)
    zb = _matmul(hn, wb_in, tm_in, ZB_WIDTH // 3)
    zc = _matmul(hn, wc_in, tm_in, ZC_WIDTH // 3)

    flat = lambda v: v.reshape(1, -1)
    rw = (mu, flat(w0[0]), w2[0], flat(a0[0]), a2[0], flat(k_k[0]), flat(k_a[0]), flat(r_k[0]),
          flat(gn_w[0]), flat(gn_b[0]))
    ya_p, s_p = _rwkv_chunked(za, seq, jnp.zeros((1, 1, A_SHIFT_WIDTH), F32),
                              jnp.zeros((1, A_QUADS, A_HEAD_DIM, QUAD_LANES), F32), *rw)
    assert t_new == SUBLANES
    ya_s, s_s = _rwkv_short(za, seq, bs, state_shift[0][:, None, :], state_wkv[0], *rw)
    shift_p = za[seq - 1, :A_SHIFT_WIDTH].reshape(1, 1, A_SHIFT_WIDTH)
    shift_s = za[seq:, :A_SHIFT_WIDTH].reshape(bs, t_new, A_SHIFT_WIDTH)[:, -1][None]
    wkv_p = _keymajor_to_state(s_p)[None]
    wkv_s = s_s[None]

    cos_p, sin_p = _rope_tables(jnp.arange(seq, dtype=jnp.int32))
    cos_s, sin_s = _rope_tables(PAST_LEN + jnp.arange(t_new, dtype=jnp.int32))
    o_p, l_p, o_s, l_s, kv_p, kv_s = [], [], [], [], [], []
    for gi, (window, dil) in enumerate(ATT_GROUPS):
        o, lse, kr = _attn_prompt(zb, gi, dil, seq, cos_p, sin_p, q_gain, k_gain)
        o_p.append(o)
        l_p.append(lse)
        w_keep = min(window, seq)
        v_g = zb[seq - w_keep:seq, 2 * B_QKV_WIDTH + gi * B_GROUP_WIDTH:2 * B_QKV_WIDTH + (gi + 1) * B_GROUP_WIDTH]
        kv_p.append(jnp.stack([kr[seq - w_keep:].reshape(w_keep, HEADS_PER_GROUP, B_HEAD_DIM),
                               v_g.reshape(w_keep, HEADS_PER_GROUP, B_HEAD_DIM)], axis=1)[None, None])
        cache = caches[gi][0]
        o, lse, new_cache = _attn_sample(zb, gi, cache.shape[1], dil, seq, bs, t_new, cache,
                                         cos_s, sin_s, q_gain, k_gain)
        o_s.append(o)
        l_s.append(lse)
        kv_s.append(new_cache[None])
    yb_p = _combine(o_p, l_p, zc, 0, 1024)
    yb_s = _combine(o_s, l_s, zc, seq // rs, rs)

    wa_up = w_up_a[0].astype(BF16)
    wb_up = w_up_b[0].astype(BF16)
    merged_p = _merge(ya_p, yb_p, zc, 0, wa_up, wb_up, 2048, B_GROUP_WIDTH)
    merged_s = _merge(ya_s, yb_s, zc, seq // rs, wa_up, wb_up, rs, B_GROUP_WIDTH)
    w_out_b = w_out[0].astype(BF16)
    h_p = _outproj(xp, merged_p, w_out_b, 512, D_MODEL)
    h_s = _outproj(xs, merged_s, w_out_b, rs, D_MODEL)
    wg = w_ple_gate[0].astype(BF16)
    wp = w_ple_proj[0].astype(BF16)
    y_p = _ple(h_p, p_prompt[0, 0], wg, wp, 512)
    y_s = _ple(h_s, p_sample[0].reshape(rs, -1), wg, wp, rs)

    return (y_p.reshape(bp, seq, d), y_s.reshape(bs, t_new, d),
            shift_p, wkv_p, kv_p[0], kv_p[1], kv_p[2],
            shift_s, wkv_s, kv_s[0], kv_s[1], kv_s[2])
```

```python
import functools

import jax
import jax.numpy as jnp
from jax import lax
from jax.experimental import pallas as pl
from jax.experimental.pallas import tpu as pltpu

F32 = jnp.float32
BF16 = jnp.bfloat16

LANES = 128
SUBLANES = 8
D_MODEL = 2048
A_HEAD_DIM = 64
A_WIDTH = 1024
A_HEADS = 16
QUAD_LANES = 256
A_QUADS = A_WIDTH // QUAD_LANES
CHUNK = A_HEAD_DIM
RWKV_SEQS_PER_STEP = 4
CHUNKS_PER_STEP = 2
LORA_RANK = 64
A_SHIFT_WIDTH = 3 * A_WIDTH + 2 * LORA_RANK
GN_EPS = 64e-5
NORM_EPS = 1e-6
ATT_GROUPS = ((128, 1), (512, 4), (2048, 16))
N_BACK = 128
HEADS_PER_GROUP = 4
B_HEAD_DIM = 128
B_GROUP_WIDTH = HEADS_PER_GROUP * B_HEAD_DIM
B_QKV_WIDTH = 3 * B_GROUP_WIDTH
ATT_PROMPT_ROWS = 1024
ATT_SAMPLE_WINDOW_ROWS = 2048
ATT_RESIDUE_GROUP = 4
ROPE_THETA = 10000.0
PAST_LEN = 16384
ZA_WIDTH = A_SHIFT_WIDTH + A_WIDTH
ZB_WIDTH = 3 * B_QKV_WIDTH
ZC_WIDTH = 2 * D_MODEL + B_GROUP_WIDTH
VMEM_LIMIT = 48 * 1024 * 1024


def _cparams(sem):
    return pltpu.CompilerParams(dimension_semantics=sem, vmem_limit_bytes=VMEM_LIMIT)


def _sigmoid(x):
    return 1.0 / (1.0 + jnp.exp(-x))


def _rmsnorm_kernel(xp_ref, xs_ref, g_ref, o_ref, *, n_prompt_tiles):
    def emit(x):
        y = x * lax.rsqrt(jnp.mean(x * x, axis=-1, keepdims=True) + NORM_EPS)
        o_ref[...] = (y * g_ref[...]).astype(BF16)

    i = pl.program_id(0)

    @pl.when(i < n_prompt_tiles)
    def _():
        emit(xp_ref[...])

    @pl.when(i >= n_prompt_tiles)
    def _():
        emit(xs_ref[...])


def _rmsnorm(xp, xs, g):
    rp, d = xp.shape
    rs = xs.shape[0]
    tm = rs
    n_p = rp // tm
    return pl.pallas_call(
        functools.partial(_rmsnorm_kernel, n_prompt_tiles=n_p),
        grid=(n_p + 1,),
        in_specs=[
            pl.BlockSpec((tm, d), lambda i: (jnp.minimum(i, n_p - 1), 0)),
            pl.BlockSpec((tm, d), lambda i: (0, 0)),
            pl.BlockSpec((1, d), lambda i: (0, 0)),
        ],
        out_specs=pl.BlockSpec((tm, d), lambda i: (i, 0)),
        out_shape=jax.ShapeDtypeStruct((rp + rs, d), BF16),
        compiler_params=_cparams(("arbitrary",)),
    )(xp, xs, g)


def _mm_kernel(a_ref, w_ref, o_ref):
    o_ref[...] = jnp.dot(a_ref[...], w_ref[...], preferred_element_type=F32)


def _matmul(a, w, tm, tn):
    m, k = a.shape
    n = w.shape[1]
    return pl.pallas_call(
        _mm_kernel,
        grid=(n // tn, m // tm),
        in_specs=[
            pl.BlockSpec((tm, k), lambda j, i: (i, 0)),
            pl.BlockSpec((k, tn), lambda j, i: (0, j)),
        ],
        out_specs=pl.BlockSpec((tm, tn), lambda j, i: (i, j)),
        out_shape=jax.ShapeDtypeStruct((m, n), F32),
        compiler_params=_cparams(("parallel", "arbitrary")),
    )(a, w)


def _split_dot(x, ones_bf16):
    rows = x.shape[0]
    hi = x.astype(BF16)
    lo = (x - hi.astype(F32)).astype(BF16)
    both = jnp.dot(jnp.concatenate([hi, lo], axis=0), ones_bf16, preferred_element_type=F32)
    return both[:rows] + both[rows:]


def _rwkv_kernel(z_ref, shift0_ref, s0_ref, mu_ref, w0_ref, w2_ref, a0_ref, a2_ref, kk_ref, ka_ref,
                 rk_ref, gnw_ref, gnb_ref,
                 ya_ref, sT_ref,
                 nkk_ref, w_ref, b_ref, k_ref, r_ref, v_ref, y_ref, bonus_ref,
                 *, n_seq):
    t_len = SUBLANES
    tc = n_seq * t_len
    heads_per_quad = QUAD_LANES // A_HEAD_DIM

    lane = lax.broadcasted_iota(jnp.int32, (QUAD_LANES, QUAD_LANES), 1)
    row = lax.broadcasted_iota(jnp.int32, (QUAD_LANES, QUAD_LANES), 0)
    head_ones = (lane // A_HEAD_DIM == row // A_HEAD_DIM).astype(BF16)
    diag = (lane % A_HEAD_DIM) == (row % A_HEAD_DIM)

    z = z_ref[:, :A_SHIFT_WIDTH]
    prev = pltpu.roll(z, 1, axis=0)
    row_id = lax.broadcasted_iota(jnp.int32, (tc, 1), 0)
    for g in range(n_seq):
        prev = jnp.where(row_id == g * t_len, shift0_ref[g], prev)
    zs = z + (prev - z) * mu_ref[...]

    lora = zs[:, 3 * A_WIDTH:]
    w_lo = jnp.tanh(lora[:, :LORA_RANK]).astype(BF16)
    a_lo = lora[:, LORA_RANK:].astype(BF16)
    lw = w0_ref[...] + jnp.dot(w_lo, w2_ref[...].astype(BF16), preferred_element_type=F32)
    nlw = -lw
    log_w = -(jnp.maximum(nlw, 0.0) + jnp.log1p(jnp.exp(-jnp.abs(nlw)))) - 0.5
    w_ref[...] = jnp.exp(-jnp.exp(log_w))
    a = _sigmoid(a0_ref[...] + jnp.dot(a_lo, a2_ref[...].astype(BF16), preferred_element_type=F32))

    for q in range(A_QUADS):
        sl = slice(q * QUAD_LANES, (q + 1) * QUAD_LANES)
        r_q = zs[:, q * QUAD_LANES:(q + 1) * QUAD_LANES]
        k_q = zs[:, A_WIDTH + q * QUAD_LANES:A_WIDTH + (q + 1) * QUAD_LANES]
        v_q = zs[:, 2 * A_WIDTH + q * QUAD_LANES:2 * A_WIDTH + (q + 1) * QUAD_LANES]
        a_q = a[:, sl]
        kk = k_q * kk_ref[:, sl]
        n2 = _split_dot(kk * kk, head_ones)
        kk = kk / jnp.maximum(jnp.sqrt(n2), 1e-12)
        k_mod = k_q * (1.0 + (a_q - 1.0) * ka_ref[:, sl])
        nkk_ref[:, sl] = -kk
        b_ref[:, sl] = kk * a_q
        k_ref[:, sl] = k_mod
        r_ref[:, sl] = r_q
        v_ref[:, sl] = v_q
        bonus_ref[:, sl] = _split_dot(r_q * k_mod * rk_ref[:, sl], head_ones) * v_q

    row8 = lax.broadcasted_iota(jnp.int32, (SUBLANES, QUAD_LANES), 0)

    def expand(tiles, u):
        return jnp.concatenate(
            [jnp.broadcast_to(t[u:u + 1], (A_HEAD_DIM, QUAD_LANES)) for t in tiles], axis=0)

    seqs = range(n_seq)
    tiles = lambda ref, g: [ref[g * t_len:(g + 1) * t_len, q * QUAD_LANES:(q + 1) * QUAD_LANES]
                            for q in range(A_QUADS)]
    vecs = [[tiles(ref, g) for ref in (nkk_ref, w_ref, b_ref, k_ref, r_ref, v_ref)] for g in seqs]
    s = [jnp.concatenate([jnp.concatenate([s0_ref[g, q * heads_per_quad + h] for h in range(heads_per_quad)],
                                          axis=1) for q in range(A_QUADS)], axis=0) for g in seqs]
    ytiles = [[jnp.zeros((SUBLANES, QUAD_LANES), F32) for _ in range(A_QUADS)] for _ in seqs]
    for u in range(t_len):
        for g in seqs:
            nkk8, w8, b8, k8, r8, v8 = vecs[g]
            sa = jnp.dot((s[g] * expand(nkk8, u)).astype(BF16), head_ones, preferred_element_type=F32)
            vcol = jnp.dot(jnp.where(diag, expand(v8, u), 0.0).astype(BF16), head_ones,
                           preferred_element_type=F32)
            s[g] = s[g] * expand(w8, u) + sa * expand(b8, u) + vcol * expand(k8, u)
            ycol = jnp.dot((s[g] * expand(r8, u)).astype(BF16), head_ones, preferred_element_type=F32)
            ysel = jnp.where(diag, ycol, 0.0)
            for q in range(A_QUADS):
                yrow = jnp.sum(ysel[q * A_HEAD_DIM:(q + 1) * A_HEAD_DIM], axis=0, keepdims=True)
                ytiles[g][q] = jnp.where(row8 == u, yrow, ytiles[g][q])
    for g in seqs:
        for q in range(A_QUADS):
            y_ref[g * t_len:(g + 1) * t_len, q * QUAD_LANES:(q + 1) * QUAD_LANES] = ytiles[g][q]
        for hd in range(A_HEADS):
            q, h = divmod(hd, heads_per_quad)
            sT_ref[g, hd] = s[g][q * A_HEAD_DIM:(q + 1) * A_HEAD_DIM, h * A_HEAD_DIM:(h + 1) * A_HEAD_DIM]

    for q in range(A_QUADS):
        sl = slice(q * QUAD_LANES, (q + 1) * QUAD_LANES)
        y = y_ref[:, sl]
        mean = _split_dot(y, head_ones) * (1.0 / A_HEAD_DIM)
        yc = y - mean
        var = _split_dot(yc * yc, head_ones) * (1.0 / A_HEAD_DIM)
        yn = yc * lax.rsqrt(var + GN_EPS) * gnw_ref[:, sl] + gnb_ref[:, sl]
        yn = yn + bonus_ref[:, sl]
        g = z_ref[:, A_SHIFT_WIDTH + q * QUAD_LANES:A_SHIFT_WIDTH + (q + 1) * QUAD_LANES]
        ya_ref[:, sl] = (yn * (g * _sigmoid(g))).astype(BF16)


def _rwkv_short(za, row0, n_seq, shift0, s0, mu, w0, w2, a0, a2, k_k, k_a, r_k, gn_w, gn_b):
    per_step = RWKV_SEQS_PER_STEP
    tr = per_step * SUBLANES
    vec = lambda n: pl.BlockSpec((1, n), lambda b: (0, 0))
    state_spec = pl.BlockSpec((per_step, A_HEADS, A_HEAD_DIM, A_HEAD_DIM), lambda b: (b, 0, 0, 0))
    return pl.pallas_call(
        functools.partial(_rwkv_kernel, n_seq=per_step),
        grid=(n_seq // per_step,),
        in_specs=[
            pl.BlockSpec((tr, ZA_WIDTH), lambda b: (row0 // tr + b, 0)),
            pl.BlockSpec((per_step, 1, A_SHIFT_WIDTH), lambda b: (b, 0, 0)),
            state_spec,
            vec(A_SHIFT_WIDTH), vec(A_WIDTH),
            pl.BlockSpec((LORA_RANK, A_WIDTH), lambda b: (0, 0)),
            vec(A_WIDTH),
            pl.BlockSpec((LORA_RANK, A_WIDTH), lambda b: (0, 0)),
            vec(A_WIDTH), vec(A_WIDTH), vec(A_WIDTH), vec(A_WIDTH), vec(A_WIDTH),
        ],
        out_specs=[
            pl.BlockSpec((tr, A_WIDTH), lambda b: (b, 0)),
            state_spec,
        ],
        out_shape=[
            jax.ShapeDtypeStruct((n_seq * SUBLANES, A_WIDTH), BF16),
            jax.ShapeDtypeStruct((n_seq, A_HEADS, A_HEAD_DIM, A_HEAD_DIM), F32),
        ],
        scratch_shapes=[pltpu.VMEM((tr, A_WIDTH), F32) for _ in range(8)],
        compiler_params=_cparams(("arbitrary",)),
    )(za, shift0, s0, mu, w0, w2, a0, a2, k_k, k_a, r_k, gn_w, gn_b)


def _head_blocks(y, same_head):
    return jnp.where(same_head, jnp.concatenate([y, y, y, y], axis=0), jnp.zeros((), y.dtype))


def _off_block(t, s, b):
    return jnp.logical_and(t // (2 * b) == s // (2 * b), jnp.logical_and(t % (2 * b) >= b, s % (2 * b) < b))


def _split2(x):
    hi = x.astype(BF16)
    return hi, (x - hi.astype(F32)).astype(BF16)


def _head_matmul3(x, ys, same_head):
    rows = x.shape[0]
    xh, xl = _split2(x)
    x_both = jnp.concatenate([xh, xl], axis=0)
    outs = []
    for y in ys:
        yh, yl = _split2(y)
        both = jnp.dot(x_both, _head_blocks(yh, same_head), preferred_element_type=F32)
        low = jnp.dot(xh, _head_blocks(yl, same_head), preferred_element_type=F32)
        outs.append(both[:rows] + both[rows:] + low)
    return outs


def _head_matmul(x, y, same_head):
    return jnp.dot(x.astype(BF16), _head_blocks(y.astype(BF16), same_head), preferred_element_type=F32)


def _rwkv_chunk_kernel(z_ref, shift0_ref, s0_ref, mu_ref, w0_ref, w2_ref, a0_ref, a2_ref, kk_ref, ka_ref,
                       rk_ref, gnw_ref, gnb_ref,
                       ya_ref, sT_ref, carry_ref, st_ref):
    c = pl.program_id(1)
    n_c = pl.num_programs(1)
    tc = CHUNK
    rows_all = tc * CHUNKS_PER_STEP

    @pl.when(c == 0)
    def _():
        carry_ref[...] = shift0_ref[0]
        st_ref[...] = s0_ref[0]

    lane = lax.broadcasted_iota(jnp.int32, (QUAD_LANES, QUAD_LANES), 1)
    row = lax.broadcasted_iota(jnp.int32, (QUAD_LANES, QUAD_LANES), 0)
    same_head = lane // A_HEAD_DIM == row // A_HEAD_DIM
    head_ones = same_head.astype(BF16)
    t_idx = lax.broadcasted_iota(jnp.int32, (tc, QUAD_LANES), 0)
    lane_t = lax.broadcasted_iota(jnp.int32, (tc, QUAD_LANES), 1)
    s_idx = lane_t % A_HEAD_DIM
    strict = s_idx < t_idx
    incl = s_idx <= t_idx
    eye = (s_idx == t_idx).astype(F32)
    lane_head = lane_t // A_HEAD_DIM
    tri = (lax.broadcasted_iota(jnp.int32, (tc, tc), 1) <= lax.broadcasted_iota(jnp.int32, (tc, tc), 0)
           ).astype(BF16)

    z = z_ref[:, :A_SHIFT_WIDTH]
    prev = pltpu.roll(z, 1, axis=0)
    first = lax.broadcasted_iota(jnp.int32, (rows_all, 1), 0) == 0
    prev = jnp.where(first, carry_ref[...], prev)
    carry_ref[...] = z[rows_all - 1:rows_all, :]
    zs = z + (prev - z) * mu_ref[...]

    lora = zs[:, 3 * A_WIDTH:]
    w_lo = jnp.tanh(lora[:, :LORA_RANK]).astype(BF16)
    a_lo = lora[:, LORA_RANK:].astype(BF16)
    lw = w0_ref[...] + jnp.dot(w_lo, w2_ref[...].astype(BF16), preferred_element_type=F32)
    nlw = -lw
    log_w = -(jnp.maximum(nlw, 0.0) + jnp.log1p(jnp.exp(-jnp.abs(nlw)))) - 0.5
    log_decay = -jnp.exp(log_w)
    a = _sigmoid(a0_ref[...] + jnp.dot(a_lo, a2_ref[...].astype(BF16), preferred_element_type=F32))

    quads = range(A_QUADS)
    lanes = [slice(q * QUAD_LANES, (q + 1) * QUAD_LANES) for q in quads]

    jobs = [(sub, q) for sub in range(CHUNKS_PER_STEP) for q in quads]
    n_jobs = range(len(jobs))
    job_rows = lambda stacked, j: stacked[j * tc:(j + 1) * tc]

    def raw(job):
        sub, q = job
        sl = lanes[q]
        rs = slice(sub * tc, (sub + 1) * tc)
        r_q = zs[rs, q * QUAD_LANES:(q + 1) * QUAD_LANES]
        k_q = zs[rs, A_WIDTH + q * QUAD_LANES:A_WIDTH + (q + 1) * QUAD_LANES]
        v_q = zs[rs, 2 * A_WIDTH + q * QUAD_LANES:2 * A_WIDTH + (q + 1) * QUAD_LANES]
        a_q = a[rs, sl]
        kk = k_q * kk_ref[:, sl]
        k_mod = k_q * (1.0 + (a_q - 1.0) * ka_ref[:, sl])
        return dict(r=r_q, v=v_q, a=a_q, kk=kk, k_mod=k_mod, rkk=r_q * k_mod * rk_ref[:, sl])

    rw = [raw(job) for job in jobs]
    sums = _split_dot(jnp.concatenate([t["kk"] * t["kk"] for t in rw] + [t["rkk"] for t in rw], axis=0), head_ones)
    cums = []
    for sub in range(CHUNKS_PER_STEP):
        ld = log_decay[sub * tc:(sub + 1) * tc, :]
        h1, h2 = _split2(ld)
        h3 = (ld - h1.astype(F32) - h2.astype(F32)).astype(BF16)
        cums.append(jnp.dot(tri, h1, preferred_element_type=F32) + jnp.dot(tri, h2, preferred_element_type=F32)
                    + jnp.dot(tri, h3, preferred_element_type=F32))
    cum_of = lambda job: cums[job[0]][:, lanes[job[1]]]
    p_cols = _split_dot(jnp.concatenate(
        [jnp.where(eye > 0, jnp.exp(cum_of(job)[tc - 1:tc, :]), 0.0) for job in jobs], axis=0), head_ones)

    def tokens(j):
        job, t = jobs[j], rw[j]
        sub, q = job
        kk = t["kk"] / jnp.maximum(jnp.sqrt(job_rows(sums, j)), 1e-12)
        k_mod = t["k_mod"]
        beta = kk * t["a"]
        bonus = job_rows(sums, len(jobs) + j) * t["v"]
        cum = cum_of(job)
        ld = log_decay[sub * tc:(sub + 1) * tc, lanes[q]]
        cum_last = cum[tc - 1:tc, :]
        alpha_p = -kk * jnp.exp(cum - ld)
        r_p = t["r"] * jnp.exp(cum)
        inv_p = jnp.exp(-cum)
        to_end = jnp.exp(cum_last - cum)
        lhs = jnp.concatenate([alpha_p, r_p], axis=0).astype(BF16)
        g_b = _dot_nt(lhs, _head_blocks((beta * inv_p).astype(BF16), same_head))
        g_k = _dot_nt(lhs, _head_blocks((k_mod * inv_p).astype(BF16), same_head))
        lhs_t = jnp.concatenate([beta * to_end, k_mod * to_end], axis=0).astype(BF16)
        return dict(v=t["v"], bonus=bonus, ar_p=lhs,
                    a_ab=jnp.where(strict, g_b[:tc], 0.0), a_rb=jnp.where(incl, g_b[tc:], 0.0),
                    a_ak=jnp.where(strict, g_k[:tc], 0.0), a_rk=jnp.where(incl, g_k[tc:], 0.0),
                    lhs_t=lhs_t, p_col=job_rows(p_cols, j))

    tk = [tokens(j) for j in n_jobs]

    inv = [eye + jnp.where(_off_block(t_idx, s_idx, 1), t["a_ab"], 0.0) for t in tk]
    b = 2
    while b < tc:
        off = _off_block(t_idx, s_idx, b)
        mid = [_head_matmul3(jnp.where(off, tk[j]["a_ab"], 0.0), [inv[j]], same_head)[0] for j in n_jobs]
        inv = [inv[j] + _head_matmul3(inv[j], [mid[j]], same_head)[0] for j in n_jobs]
        b *= 2
    av = [_head_matmul(jnp.concatenate([t["a_ak"], t["a_rk"]], axis=0), t["v"], same_head) for t in tk]
    w1 = [m[:tc] for m in av]
    y_v = [m[tc:] for m in av]

    st = [st_ref[q] for q in quads]
    for sub in range(CHUNKS_PER_STEP):
        js = [sub * A_QUADS + q for q in quads]
        st_blocks = [_head_blocks(s.astype(BF16), same_head) for s in st]
        from_st = [jnp.dot(tk[j]["ar_p"], st_blocks[q], preferred_element_type=F32)
                   for q, j in zip(quads, js)]
        rhs = [w1[j] + from_st[q][:tc] for q, j in zip(quads, js)]
        u = [_head_matmul3(inv[j], [rhs[q]], same_head)[0] for q, j in zip(quads, js)]
        new_st, ys = [], []
        for q, j in zip(quads, js):
            t = tk[j]
            ys.append(from_st[q][tc:] + _head_matmul(t["a_rb"], u[q], same_head) + y_v[j])
            rhs_t = jnp.concatenate([u[q], t["v"]], axis=0).astype(BF16)
            cross = lax.dot_general(t["lhs_t"], rhs_t, (((0,), (0,)), ((), ())), preferred_element_type=F32)
            new = t["p_col"] * st[q]
            for h in range(QUAD_LANES // A_HEAD_DIM):
                new += jnp.where(lane_head == h, cross[h * A_HEAD_DIM:(h + 1) * A_HEAD_DIM], 0.0)
            new_st.append(new)
        st = new_st

        y_all = jnp.concatenate(ys, axis=0)
        yc_all = y_all - _split_dot(y_all, head_ones) * (1.0 / A_HEAD_DIM)
        var_all = _split_dot(yc_all * yc_all, head_ones) * (1.0 / A_HEAD_DIM)
        rs = slice(sub * tc, (sub + 1) * tc)
        for q, j in zip(quads, js):
            sl = lanes[q]
            yn = job_rows(yc_all, q) * lax.rsqrt(job_rows(var_all, q) + GN_EPS) * gnw_ref[:, sl] + gnb_ref[:, sl]
            yn = yn + tk[j]["bonus"]
            g = z_ref[rs, A_SHIFT_WIDTH + q * QUAD_LANES:A_SHIFT_WIDTH + (q + 1) * QUAD_LANES]
            ya_ref[rs, sl] = (yn * (g * _sigmoid(g))).astype(BF16)
    for q in quads:
        st_ref[q] = st[q]

    @pl.when(c == n_c - 1)
    def _():
        sT_ref[0] = st_ref[...]


def _rwkv_chunked(za, t_len, shift0, s0, mu, w0, w2, a0, a2, k_k, k_a, r_k, gn_w, gn_b):
    rows = CHUNK * CHUNKS_PER_STEP
    n_c = t_len // rows
    vec = lambda n: pl.BlockSpec((1, n), lambda b, c: (0, 0))
    state_spec = pl.BlockSpec((1, A_QUADS, A_HEAD_DIM, QUAD_LANES), lambda b, c: (b, 0, 0, 0))
    return pl.pallas_call(
        _rwkv_chunk_kernel,
        grid=(1, n_c),
        in_specs=[
            pl.BlockSpec((rows, ZA_WIDTH), lambda b, c: (c, 0)),
            pl.BlockSpec((1, 1, A_SHIFT_WIDTH), lambda b, c: (b, 0, 0)),
            state_spec,
            vec(A_SHIFT_WIDTH), vec(A_WIDTH),
            pl.BlockSpec((LORA_RANK, A_WIDTH), lambda b, c: (0, 0)),
            vec(A_WIDTH),
            pl.BlockSpec((LORA_RANK, A_WIDTH), lambda b, c: (0, 0)),
            vec(A_WIDTH), vec(A_WIDTH), vec(A_WIDTH), vec(A_WIDTH), vec(A_WIDTH),
        ],
        out_specs=[
            pl.BlockSpec((rows, A_WIDTH), lambda b, c: (c, 0)),
            state_spec,
        ],
        out_shape=[
            jax.ShapeDtypeStruct((t_len, A_WIDTH), BF16),
            jax.ShapeDtypeStruct((1, A_QUADS, A_HEAD_DIM, QUAD_LANES), F32),
        ],
        scratch_shapes=[pltpu.VMEM((1, A_SHIFT_WIDTH), F32),
                        pltpu.VMEM((A_QUADS, A_HEAD_DIM, QUAD_LANES), F32)],
        compiler_params=_cparams(("arbitrary", "arbitrary")),
    )(za, shift0, s0, mu, w0, w2, a0, a2, k_k, k_a, r_k, gn_w, gn_b)


def _keymajor_to_state(s):
    b = s.shape[0]
    return s.reshape(b, A_QUADS, A_HEAD_DIM, 4, A_HEAD_DIM).transpose(0, 1, 3, 4, 2).reshape(
        b, A_HEADS, A_HEAD_DIM, A_HEAD_DIM)


def _norm_rope(x, gain, cos, sin):
    y = x * lax.rsqrt(jnp.mean(x * x, axis=-1, keepdims=True) + NORM_EPS) * gain
    return y * cos + pltpu.roll(y, B_HEAD_DIM // 2, axis=1) * sin


def _dot_nt(a, b):
    return lax.dot_general(a, b, (((1,), (1,)), ((), ())), preferred_element_type=F32)


def _attn_prompt_kernel(q_ref, k_ref, v_ref, cos_ref, sin_ref, qg_ref, kg_ref,
                        o_ref, lse_ref, kr_ref, kprev_ref, vprev_ref, qs_ref, *, dil, n_sub):
    i = pl.program_id(0)
    hd = pl.program_id(1)
    blk = N_BACK

    @pl.when(i == 0)
    def _():
        kprev_ref[hd] = jnp.zeros((dil, blk, B_HEAD_DIM), BF16)
        vprev_ref[hd] = jnp.zeros((dil, blk, B_HEAD_DIM), BF16)

    cos = cos_ref[...]
    sin = sin_ref[...]
    qs_ref[...] = _norm_rope(q_ref[...], qg_ref[...], cos, sin)
    kr_ref[...] = _norm_rope(k_ref[...], kg_ref[...], cos, sin)

    rowi = lax.broadcasted_iota(jnp.int32, (blk, blk), 0)
    coli = lax.broadcasted_iota(jnp.int32, (blk, blk), 1)
    cur_ok = coli <= rowi
    scale = B_HEAD_DIM ** -0.5
    span = lambda sub, rho: pl.ds(sub * blk * dil + rho, blk, stride=dil)
    units = [(sub, rho) for sub in range(n_sub) for rho in range(dil)]
    for u0 in range(0, len(units), ATT_RESIDUE_GROUP):
        group = units[u0:u0 + ATT_RESIDUE_GROUP]
        rows = [span(sub, rho) for sub, rho in group]
        qn = [qs_ref[r, :].astype(BF16) for r in rows]
        kb = [kr_ref[r, :].astype(BF16) for r in rows]
        vb = [v_ref[r, :].astype(BF16) for r in rows]
        kp = [kr_ref[span(sub - 1, rho), :].astype(BF16) if sub > 0 else kprev_ref[hd, rho] for sub, rho in group]
        vp = [v_ref[span(sub - 1, rho), :].astype(BF16) if sub > 0 else vprev_ref[hd, rho] for sub, rho in group]
        prev_ok = [coli >= rowi if sub > 0 else jnp.logical_and(coli >= rowi, i > 0) for sub, _ in group]
        s_cur = [jnp.where(cur_ok, _dot_nt(q, k) * scale, -jnp.inf) for q, k in zip(qn, kb)]
        s_prev = [jnp.where(ok, _dot_nt(q, k) * scale, -jnp.inf) for ok, q, k in zip(prev_ok, qn, kp)]
        m = [jnp.max(jnp.maximum(c, p), axis=-1, keepdims=True) for c, p in zip(s_cur, s_prev)]
        p_cur = [jnp.exp(c - mm) for c, mm in zip(s_cur, m)]
        p_prev = [jnp.exp(p - mm) for p, mm in zip(s_prev, m)]
        den = [jnp.sum(c + p, axis=-1, keepdims=True) for c, p in zip(p_cur, p_prev)]
        for n, r in enumerate(rows):
            o = jnp.dot((p_cur[n] / den[n]).astype(BF16), vb[n], preferred_element_type=F32)
            o += jnp.dot((p_prev[n] / den[n]).astype(BF16), vp[n], preferred_element_type=F32)
            o_ref[r, :] = o
            lse_ref[r, :] = jnp.broadcast_to(m[n] + jnp.log(den[n]), (blk, B_HEAD_DIM))
    for rho in range(dil):
        kprev_ref[hd, rho] = kr_ref[span(n_sub - 1, rho), :].astype(BF16)
        vprev_ref[hd, rho] = v_ref[span(n_sub - 1, rho), :].astype(BF16)


def _attn_prompt(zb, gi, dil, seq, cos, sin, q_gain, k_gain):
    n_sub = max(1, ATT_PROMPT_ROWS // (N_BACK * dil))
    tr = N_BACK * dil * n_sub
    heads_qkv = B_QKV_WIDTH // B_HEAD_DIM
    blk = lambda which: pl.BlockSpec(
        (tr, B_HEAD_DIM), lambda i, hd: (i, which * heads_qkv + gi * HEADS_PER_GROUP + hd))
    tab = pl.BlockSpec((tr, B_HEAD_DIM), lambda i, hd: (i, 0))
    gain = pl.BlockSpec((1, B_HEAD_DIM), lambda i, hd: (0, 0))
    out = pl.BlockSpec((tr, B_HEAD_DIM), lambda i, hd: (i, hd))
    shp = jax.ShapeDtypeStruct((seq, B_GROUP_WIDTH), F32)
    prev = pltpu.VMEM((HEADS_PER_GROUP, dil, N_BACK, B_HEAD_DIM), BF16)
    return pl.pallas_call(
        functools.partial(_attn_prompt_kernel, dil=dil, n_sub=n_sub),
        grid=(seq // tr, HEADS_PER_GROUP),
        in_specs=[blk(0), blk(1), blk(2), tab, tab, gain, gain],
        out_specs=[out, out, out],
        out_shape=[shp, shp, shp],
        scratch_shapes=[prev, prev, pltpu.VMEM((tr, B_HEAD_DIM), F32)],
        compiler_params=_cparams(("arbitrary", "arbitrary")),
    )(zb, zb, zb, cos, sin, q_gain, k_gain)


def _attn_sample_kernel(q_ref, k_ref, v_ref, cache_ref, cos_ref, sin_ref, qg_ref, kg_ref,
                        o_ref, lse_ref, new_ref, *, window, dil, t_new, n_seq):
    per_pos = 2 * HEADS_PER_GROUP
    cos = cos_ref[...]
    sin = sin_ref[...]
    scale = B_HEAD_DIM ** -0.5
    tq = lax.broadcasted_iota(jnp.int32, (t_new, window), 0)
    cc = lax.broadcasted_iota(jnp.int32, (t_new, window), 1)
    cache_ok = jnp.logical_and(((cc - tq + dil * N_BACK) & (dil - 1)) == 0, cc >= tq)
    tn = lax.broadcasted_iota(jnp.int32, (t_new, t_new), 0)
    un = lax.broadcasted_iota(jnp.int32, (t_new, t_new), 1)
    new_ok = jnp.logical_and(((tn - un + dil * N_BACK) & (dil - 1)) == 0, un <= tn)
    for g in range(n_seq):
        new_ref[g, 0:(window - t_new) * per_pos, :] = cache_ref[g, t_new * per_pos:window * per_pos, :]
    units = [(g, h) for g in range(n_seq) for h in range(HEADS_PER_GROUP)]
    for u0 in range(0, len(units), HEADS_PER_GROUP):
        group = units[u0:u0 + HEADS_PER_GROUP]
        rows = [slice(g * t_new, (g + 1) * t_new) for g, _ in group]
        cols = [slice(h * B_HEAD_DIM, (h + 1) * B_HEAD_DIM) for _, h in group]
        qn = [_norm_rope(q_ref[r, c], qg_ref[...], cos, sin).astype(BF16) for r, c in zip(rows, cols)]
        kr = [_norm_rope(k_ref[r, c], kg_ref[...], cos, sin) for r, c in zip(rows, cols)]
        v_new = [v_ref[r, c] for r, c in zip(rows, cols)]
        kc = [cache_ref[g, pl.ds(h, window, stride=per_pos), :].astype(BF16) for g, h in group]
        vc = [cache_ref[g, pl.ds(HEADS_PER_GROUP + h, window, stride=per_pos), :].astype(BF16) for g, h in group]
        s_c = [jnp.where(cache_ok, _dot_nt(q, k) * scale, -jnp.inf) for q, k in zip(qn, kc)]
        s_n = [jnp.where(new_ok, _dot_nt(q, k.astype(BF16)) * scale, -jnp.inf) for q, k in zip(qn, kr)]
        m = [jnp.maximum(jnp.max(c, axis=-1, keepdims=True), jnp.max(n, axis=-1, keepdims=True))
             for c, n in zip(s_c, s_n)]
        p_c = [jnp.exp(c - mm) for c, mm in zip(s_c, m)]
        p_n = [jnp.exp(n - mm) for n, mm in zip(s_n, m)]
        den = [jnp.sum(c, axis=-1, keepdims=True) + jnp.sum(n, axis=-1, keepdims=True) for c, n in zip(p_c, p_n)]
        for i, (g, h) in enumerate(group):
            o = jnp.dot((p_c[i] / den[i]).astype(BF16), vc[i], preferred_element_type=F32)
            o += jnp.dot((p_n[i] / den[i]).astype(BF16), v_new[i].astype(BF16), preferred_element_type=F32)
            o_ref[rows[i], cols[i]] = o
            lse_ref[rows[i], cols[i]] = jnp.broadcast_to(m[i] + jnp.log(den[i]), (t_new, B_HEAD_DIM))
            base = (window - t_new) * per_pos
            new_ref[g, pl.ds(base + h, t_new, stride=per_pos), :] = kr[i]
            new_ref[g, pl.ds(base + HEADS_PER_GROUP + h, t_new, stride=per_pos), :] = v_new[i]


def _attn_sample(zb, gi, window, dil, row0, n_seq, t_new, cache, cos, sin, q_gain, k_gain):
    rows = window * 2 * HEADS_PER_GROUP
    per_step = max(1, min(n_seq, ATT_SAMPLE_WINDOW_ROWS // window))
    tr = per_step * t_new
    cache2 = cache.reshape(n_seq, rows, B_HEAD_DIM)
    blk = lambda off: pl.BlockSpec((tr, B_GROUP_WIDTH), lambda b: (row0 // tr + b, off + gi))
    tab = pl.BlockSpec((t_new, B_HEAD_DIM), lambda b: (0, 0))
    gain = pl.BlockSpec((1, B_HEAD_DIM), lambda b: (0, 0))
    out = pl.BlockSpec((tr, B_GROUP_WIDTH), lambda b: (b, 0))
    win = pl.BlockSpec((per_step, rows, B_HEAD_DIM), lambda b: (b, 0, 0))
    shp = jax.ShapeDtypeStruct((n_seq * t_new, B_GROUP_WIDTH), F32)
    o, lse, new = pl.pallas_call(
        functools.partial(_attn_sample_kernel, window=window, dil=dil, t_new=t_new, n_seq=per_step),
        grid=(n_seq // per_step,),
        in_specs=[blk(0), blk(3), blk(6), win, tab, tab, gain, gain],
        out_specs=[out, out, win],
        out_shape=[shp, shp, jax.ShapeDtypeStruct((n_seq, rows, B_HEAD_DIM), F32)],
        compiler_params=_cparams(("arbitrary",)),
    )(zb, zb, zb, cache2, cos, sin, q_gain, k_gain)
    return o, lse, new.reshape(cache.shape)


def _combine_kernel(o0, o1, o2, l0, l1, l2, g_ref, y_ref):
    a, b, c = l0[...], l1[...], l2[...]
    m = jnp.maximum(jnp.maximum(a, b), c)
    ea, eb, ec = jnp.exp(a - m), jnp.exp(b - m), jnp.exp(c - m)
    o = (ea * o0[...] + eb * o1[...] + ec * o2[...]) / (ea + eb + ec)
    g = g_ref[...]
    y_ref[...] = (o * (g * _sigmoid(g))).astype(BF16)


def _combine(outs, lses, zc, row_block0, tm):
    rows = outs[0].shape[0]
    spec = pl.BlockSpec((tm, B_GROUP_WIDTH), lambda i: (i, 0))
    return pl.pallas_call(
        _combine_kernel,
        grid=(rows // tm,),
        in_specs=[spec] * 6 + [pl.BlockSpec((tm, B_GROUP_WIDTH), lambda i: (row_block0 + i, 0))],
        out_specs=spec,
        out_shape=jax.ShapeDtypeStruct((rows, B_GROUP_WIDTH), BF16),
        compiler_params=_cparams(("parallel",)),
    )(*outs, *lses, zc)


def _merge_kernel(ya_ref, yb_ref, ma_ref, mb_ref, wa_ref, wb_ref, o_ref):
    ua = jnp.dot(ya_ref[...], wa_ref[...], preferred_element_type=F32)
    ub = jnp.dot(yb_ref[...], wb_ref[...], preferred_element_type=F32)
    o_ref[...] = (_sigmoid(ma_ref[...]) * ua + _sigmoid(mb_ref[...]) * ub).astype(BF16)


def _merge(ya, yb, zc, row_block0, wa, wb, tm, tn):
    m = ya.shape[0]
    nb = D_MODEL // tn
    off_a = B_GROUP_WIDTH // tn
    off_b = (B_GROUP_WIDTH + D_MODEL) // tn
    assert off_a * tn == B_GROUP_WIDTH
    return pl.pallas_call(
        _merge_kernel,
        grid=(m // tm, nb),
        in_specs=[
            pl.BlockSpec((tm, A_WIDTH), lambda i, j: (i, 0)),
            pl.BlockSpec((tm, B_GROUP_WIDTH), lambda i, j: (i, 0)),
            pl.BlockSpec((tm, tn), lambda i, j: (row_block0 + i, off_a + j)),
            pl.BlockSpec((tm, tn), lambda i, j: (row_block0 + i, off_b + j)),
            pl.BlockSpec((A_WIDTH, tn), lambda i, j: (0, j)),
            pl.BlockSpec((B_GROUP_WIDTH, tn), lambda i, j: (0, j)),
        ],
        out_specs=pl.BlockSpec((tm, tn), lambda i, j: (i, j)),
        out_shape=jax.ShapeDtypeStruct((m, D_MODEL), BF16),
        compiler_params=_cparams(("parallel", "arbitrary")),
    )(ya, yb, zc, zc, wa, wb)


def _outproj_kernel(x_ref, a_ref, w_ref, o_ref):
    o_ref[...] = x_ref[...] + jnp.dot(a_ref[...], w_ref[...], preferred_element_type=F32)


def _outproj(x, merged, w, tm, tn):
    m = x.shape[0]
    return pl.pallas_call(
        _outproj_kernel,
        grid=(D_MODEL // tn, m // tm),
        in_specs=[
            pl.BlockSpec((tm, tn), lambda j, i: (i, j)),
            pl.BlockSpec((tm, D_MODEL), lambda j, i: (i, 0)),
            pl.BlockSpec((D_MODEL, tn), lambda j, i: (0, j)),
        ],
        out_specs=pl.BlockSpec((tm, tn), lambda j, i: (i, j)),
        out_shape=jax.ShapeDtypeStruct((m, D_MODEL), F32),
        compiler_params=_cparams(("parallel", "arbitrary")),
    )(x, merged, w)


def _ple_kernel(h_ref, p_ref, wg_ref, wp_ref, o_ref):
    h = h_ref[...]
    gate = _sigmoid(jnp.dot(h.astype(BF16), wg_ref[...], preferred_element_type=F32))
    proj = jnp.dot(p_ref[...].astype(BF16), wp_ref[...], preferred_element_type=F32)
    o_ref[...] = h + gate * proj


def _ple(h, p, wg, wp, tm):
    m = h.shape[0]
    pd = p.shape[1]
    return pl.pallas_call(
        _ple_kernel,
        grid=(m // tm,),
        in_specs=[
            pl.BlockSpec((tm, D_MODEL), lambda i: (i, 0)),
            pl.BlockSpec((tm, pd), lambda i: (i, 0)),
            pl.BlockSpec((D_MODEL, D_MODEL), lambda i: (0, 0)),
            pl.BlockSpec((pd, D_MODEL), lambda i: (0, 0)),
        ],
        out_specs=pl.BlockSpec((tm, D_MODEL), lambda i: (i, 0)),
        out_shape=jax.ShapeDtypeStruct((m, D_MODEL), F32),
        compiler_params=_cparams(("parallel",)),
    )(h, p, wg, wp)


def _rope_tables(pos):
    half = B_HEAD_DIM // 2
    inv = ROPE_THETA ** (-jnp.arange(half, dtype=F32) / half)
    ang = pos.astype(F32)[:, None] * inv[None, :]
    cos, sin = jnp.cos(ang), jnp.sin(ang)
    return jnp.concatenate([cos, cos], axis=1), jnp.concatenate([-sin, sin], axis=1)


def kernel(x_prompt, x_sample, state_shift, state_wkv, cache_kv_w128, cache_kv_w512, cache_kv_w2048,
           p_prompt, p_sample, ln_g, w_in, mu, w0, w2, a0, a2, k_k, k_a, r_k, gn_w, gn_b,
           q_gain, k_gain, w_up_a, w_up_b, w_out, w_ple_gate, w_ple_proj):
    depth = ln_g.shape[0]
    assert depth == 1
    bp, seq, d = x_prompt.shape
    bs, t_new, _ = x_sample.shape
    assert bp == 1 and d == D_MODEL
    rs = bs * t_new
    caches = (cache_kv_w128, cache_kv_w512, cache_kv_w2048)

    xp = x_prompt.reshape(seq, d)
    xs = x_sample.reshape(rs, d)
    hn = _rmsnorm(xp, xs, ln_g)

    w_in0 = w_in[0]
    off_b = ZA_WIDTH
    off_c = ZA_WIDTH + ZB_WIDTH
    wa_in = w_in0[:, :off_b].astype(BF16)
    wb_in = w_in0[:, off_b:off_c].astype(BF16)
    wc_in = w_in0[:, off_c:].astype(BF16)
    tm_in = 768
    za = _matmul(hn, wa_in, tm_in, ZA_WIDTH // 3)
    zb = _matmul(hn, wb_in, tm_in, ZB_WIDTH // 3)
    zc = _matmul(hn, wc_in, tm_in, ZC_WIDTH // 3)

    flat = lambda v: v.reshape(1, -1)
    rw = (mu, flat(w0[0]), w2[0], flat(a0[0]), a2[0], flat(k_k[0]), flat(k_a[0]), flat(r_k[0]),
          flat(gn_w[0]), flat(gn_b[0]))
    ya_p, s_p = _rwkv_chunked(za, seq, jnp.zeros((1, 1, A_SHIFT_WIDTH), F32),
                              jnp.zeros((1, A_QUADS, A_HEAD_DIM, QUAD_LANES), F32), *rw)
    assert t_new == SUBLANES
    ya_s, s_s = _rwkv_short(za, seq, bs, state_shift[0][:, None, :], state_wkv[0], *rw)
    shift_p = za[seq - 1, :A_SHIFT_WIDTH].reshape(1, 1, A_SHIFT_WIDTH)
    shift_s = za[seq:, :A_SHIFT_WIDTH].reshape(bs, t_new, A_SHIFT_WIDTH)[:, -1][None]
    wkv_p = _keymajor_to_state(s_p)[None]
    wkv_s = s_s[None]

    cos_p, sin_p = _rope_tables(jnp.arange(seq, dtype=jnp.int32))
    cos_s, sin_s = _rope_tables(PAST_LEN + jnp.arange(t_new, dtype=jnp.int32))
    o_p, l_p, o_s, l_s, kv_p, kv_s = [], [], [], [], [], []
    for gi, (window, dil) in enumerate(ATT_GROUPS):
        o, lse, kr = _attn_prompt(zb, gi, dil, seq, cos_p, sin_p, q_gain, k_gain)
        o_p.append(o)
        l_p.append(lse)
        w_keep = min(window, seq)
        v_g = zb[seq - w_keep:seq, 2 * B_QKV_WIDTH + gi * B_GROUP_WIDTH:2 * B_QKV_WIDTH + (gi + 1) * B_GROUP_WIDTH]
        kv_p.append(jnp.stack([kr[seq - w_keep:].reshape(w_keep, HEADS_PER_GROUP, B_HEAD_DIM),
                               v_g.reshape(w_keep, HEADS_PER_GROUP, B_HEAD_DIM)], axis=1)[None, None])
        cache = caches[gi][0]
        o, lse, new_cache = _attn_sample(zb, gi, cache.shape[1], dil, seq, bs, t_new, cache,
                                         cos_s, sin_s, q_gain, k_gain)
        o_s.append(o)
        l_s.append(lse)
        kv_s.append(new_cache[None])
    yb_p = _combine(o_p, l_p, zc, 0, 1024)
    yb_s = _combine(o_s, l_s, zc, seq // rs, rs)

    wa_up = w_up_a[0].astype(BF16)
    wb_up = w_up_b[0].astype(BF16)
    merged_p = _merge(ya_p, yb_p, zc, 0, wa_up, wb_up, 2048, B_GROUP_WIDTH)
    merged_s = _merge(ya_s, yb_s, zc, seq // rs, wa_up, wb_up, rs, B_GROUP_WIDTH)
    w_out_b = w_out[0].astype(BF16)
    h_p = _outproj(xp, merged_p, w_out_b, 512, D_MODEL)
    h_s = _outproj(xs, merged_s, w_out_b, rs, D_MODEL)
    wg = w_ple_gate[0].astype(BF16)
    wp = w_ple_proj[0].astype(BF16)
    y_p = _ple(h_p, p_prompt[0, 0], wg, wp, 512)
    y_s = _ple(h_s, p_sample[0].reshape(rs, -1), wg, wp, rs)

    return (y_p.reshape(bp, seq, d), y_s.reshape(bs, t_new, d),
            shift_p, wkv_p, kv_p[0], kv_p[1], kv_p[2],
            shift_s, wkv_s, kv_s[0], kv_s[1], kv_s[2])
```

```python
import functools

import jax
import jax.numpy as jnp
from jax import lax
from jax.experimental import pallas as pl
from jax.experimental.pallas import tpu as pltpu

F32 = jnp.float32
BF16 = jnp.bfloat16

SUBLANES = 8
D_MODEL = 2048
A_HEAD_DIM = 64
A_WIDTH = 1024
A_HEADS = 16
QUAD_LANES = 256
A_QUADS = A_WIDTH // QUAD_LANES
CHUNK = A_HEAD_DIM
RWKV_SEQS_PER_STEP = 4
CHUNKS_PER_STEP = 2
LORA_RANK = 64
A_SHIFT_WIDTH = 3 * A_WIDTH + 2 * LORA_RANK
GN_EPS = 64e-5
NORM_EPS = 1e-6
ATT_GROUPS = ((128, 1), (512, 4), (2048, 16))
N_BACK = 128
HEADS_PER_GROUP = 4
B_HEAD_DIM = 128
B_GROUP_WIDTH = HEADS_PER_GROUP * B_HEAD_DIM
B_QKV_WIDTH = 3 * B_GROUP_WIDTH
ATT_PROMPT_ROWS = 1024
ATT_SAMPLE_WINDOW_ROWS = 2048
ATT_RESIDUE_GROUP = 4
ROPE_THETA = 10000.0
PAST_LEN = 16384
ZA_WIDTH = A_SHIFT_WIDTH + A_WIDTH
ZB_WIDTH = 3 * B_QKV_WIDTH
ZC_WIDTH = 2 * D_MODEL + B_GROUP_WIDTH
VMEM_LIMIT = 48 * 1024 * 1024
IN_PROJ_ROWS = 768
IN_PROJ_COL_TILES = 3
COMBINE_ROWS = 1024
MERGE_ROWS = 2048
OUT_ROWS = 512


def _cparams(sem):
    return pltpu.CompilerParams(dimension_semantics=sem, vmem_limit_bytes=VMEM_LIMIT)


def _sigmoid(x):
    return 1.0 / (1.0 + jnp.exp(-x))


def _rmsnorm_kernel(xp_ref, xs_ref, g_ref, o_ref, *, n_prompt_tiles):
    def emit(x):
        y = x * lax.rsqrt(jnp.mean(x * x, axis=-1, keepdims=True) + NORM_EPS)
        o_ref[...] = (y * g_ref[...]).astype(BF16)

    i = pl.program_id(0)

    @pl.when(i < n_prompt_tiles)
    def _():
        emit(xp_ref[...])

    @pl.when(i >= n_prompt_tiles)
    def _():
        emit(xs_ref[...])


def _rmsnorm(xp, xs, g):
    rp, d = xp.shape
    rs = xs.shape[0]
    tm = rs
    n_p = rp // tm
    return pl.pallas_call(
        functools.partial(_rmsnorm_kernel, n_prompt_tiles=n_p),
        grid=(n_p + 1,),
        in_specs=[
            pl.BlockSpec((tm, d), lambda i: (jnp.minimum(i, n_p - 1), 0)),
            pl.BlockSpec((tm, d), lambda i: (0, 0)),
            pl.BlockSpec((1, d), lambda i: (0, 0)),
        ],
        out_specs=pl.BlockSpec((tm, d), lambda i: (i, 0)),
        out_shape=jax.ShapeDtypeStruct((rp + rs, d), BF16),
        compiler_params=_cparams(("arbitrary",)),
    )(xp, xs, g)


def _mm_kernel(a_ref, w_ref, o_ref):
    o_ref[...] = jnp.dot(a_ref[...], w_ref[...], preferred_element_type=F32)


def _matmul(a, w, tm, tn):
    m, k = a.shape
    n = w.shape[1]
    return pl.pallas_call(
        _mm_kernel,
        grid=(n // tn, m // tm),
        in_specs=[
            pl.BlockSpec((tm, k), lambda j, i: (i, 0)),
            pl.BlockSpec((k, tn), lambda j, i: (0, j)),
        ],
        out_specs=pl.BlockSpec((tm, tn), lambda j, i: (i, j)),
        out_shape=jax.ShapeDtypeStruct((m, n), F32),
        compiler_params=_cparams(("parallel", "arbitrary")),
    )(a, w)


def _split_dot(x, ones_bf16):
    rows = x.shape[0]
    hi = x.astype(BF16)
    lo = (x - hi.astype(F32)).astype(BF16)
    both = jnp.dot(jnp.concatenate([hi, lo], axis=0), ones_bf16, preferred_element_type=F32)
    return both[:rows] + both[rows:]


def _rwkv_kernel(z_ref, shift0_ref, s0_ref, mu_ref, w0_ref, w2_ref, a0_ref, a2_ref, kk_ref, ka_ref,
                 rk_ref, gnw_ref, gnb_ref,
                 ya_ref, sT_ref,
                 nkk_ref, w_ref, b_ref, k_ref, r_ref, v_ref, y_ref, bonus_ref,
                 *, n_seq):
    t_len = SUBLANES
    tc = n_seq * t_len
    heads_per_quad = QUAD_LANES // A_HEAD_DIM

    lane = lax.broadcasted_iota(jnp.int32, (QUAD_LANES, QUAD_LANES), 1)
    row = lax.broadcasted_iota(jnp.int32, (QUAD_LANES, QUAD_LANES), 0)
    head_ones = (lane // A_HEAD_DIM == row // A_HEAD_DIM).astype(BF16)
    diag = (lane % A_HEAD_DIM) == (row % A_HEAD_DIM)

    z = z_ref[:, :A_SHIFT_WIDTH]
    prev = pltpu.roll(z, 1, axis=0)
    row_id = lax.broadcasted_iota(jnp.int32, (tc, 1), 0)
    for g in range(n_seq):
        prev = jnp.where(row_id == g * t_len, shift0_ref[g], prev)
    zs = z + (prev - z) * mu_ref[...]

    lora = zs[:, 3 * A_WIDTH:]
    w_lo = jnp.tanh(lora[:, :LORA_RANK]).astype(BF16)
    a_lo = lora[:, LORA_RANK:].astype(BF16)
    lw = w0_ref[...] + jnp.dot(w_lo, w2_ref[...].astype(BF16), preferred_element_type=F32)
    nlw = -lw
    log_w = -(jnp.maximum(nlw, 0.0) + jnp.log1p(jnp.exp(-jnp.abs(nlw)))) - 0.5
    w_ref[...] = jnp.exp(-jnp.exp(log_w))
    a = _sigmoid(a0_ref[...] + jnp.dot(a_lo, a2_ref[...].astype(BF16), preferred_element_type=F32))

    for q in range(A_QUADS):
        sl = slice(q * QUAD_LANES, (q + 1) * QUAD_LANES)
        r_q = zs[:, q * QUAD_LANES:(q + 1) * QUAD_LANES]
        k_q = zs[:, A_WIDTH + q * QUAD_LANES:A_WIDTH + (q + 1) * QUAD_LANES]
        v_q = zs[:, 2 * A_WIDTH + q * QUAD_LANES:2 * A_WIDTH + (q + 1) * QUAD_LANES]
        a_q = a[:, sl]
        kk = k_q * kk_ref[:, sl]
        n2 = _split_dot(kk * kk, head_ones)
        kk = kk / jnp.maximum(jnp.sqrt(n2), 1e-12)
        k_mod = k_q * (1.0 + (a_q - 1.0) * ka_ref[:, sl])
        nkk_ref[:, sl] = -kk
        b_ref[:, sl] = kk * a_q
        k_ref[:, sl] = k_mod
        r_ref[:, sl] = r_q
        v_ref[:, sl] = v_q
        bonus_ref[:, sl] = _split_dot(r_q * k_mod * rk_ref[:, sl], head_ones) * v_q

    row8 = lax.broadcasted_iota(jnp.int32, (SUBLANES, QUAD_LANES), 0)

    def expand(tiles, u):
        return jnp.concatenate(
            [jnp.broadcast_to(t[u:u + 1], (A_HEAD_DIM, QUAD_LANES)) for t in tiles], axis=0)

    seqs = range(n_seq)
    tiles = lambda ref, g: [ref[g * t_len:(g + 1) * t_len, q * QUAD_LANES:(q + 1) * QUAD_LANES]
                            for q in range(A_QUADS)]
    vecs = [[tiles(ref, g) for ref in (nkk_ref, w_ref, b_ref, k_ref, r_ref, v_ref)] for g in seqs]
    s = [jnp.concatenate([jnp.concatenate([s0_ref[g, q * heads_per_quad + h] for h in range(heads_per_quad)],
                                          axis=1) for q in range(A_QUADS)], axis=0) for g in seqs]
    ytiles = [[jnp.zeros((SUBLANES, QUAD_LANES), F32) for _ in range(A_QUADS)] for _ in seqs]
    for u in range(t_len):
        for g in seqs:
            nkk8, w8, b8, k8, r8, v8 = vecs[g]
            sa = jnp.dot((s[g] * expand(nkk8, u)).astype(BF16), head_ones, preferred_element_type=F32)
            vcol = jnp.dot(jnp.where(diag, expand(v8, u), 0.0).astype(BF16), head_ones,
                           preferred_element_type=F32)
            s[g] = s[g] * expand(w8, u) + sa * expand(b8, u) + vcol * expand(k8, u)
            ycol = jnp.dot((s[g] * expand(r8, u)).astype(BF16), head_ones, preferred_element_type=F32)
            ysel = jnp.where(diag, ycol, 0.0)
            for q in range(A_QUADS):
                yrow = jnp.sum(ysel[q * A_HEAD_DIM:(q + 1) * A_HEAD_DIM], axis=0, keepdims=True)
                ytiles[g][q] = jnp.where(row8 == u, yrow, ytiles[g][q])
    for g in seqs:
        for q in range(A_QUADS):
            y_ref[g * t_len:(g + 1) * t_len, q * QUAD_LANES:(q + 1) * QUAD_LANES] = ytiles[g][q]
        for hd in range(A_HEADS):
            q, h = divmod(hd, heads_per_quad)
            sT_ref[g, hd] = s[g][q * A_HEAD_DIM:(q + 1) * A_HEAD_DIM, h * A_HEAD_DIM:(h + 1) * A_HEAD_DIM]

    for q in range(A_QUADS):
        sl = slice(q * QUAD_LANES, (q + 1) * QUAD_LANES)
        y = y_ref[:, sl]
        mean = _split_dot(y, head_ones) * (1.0 / A_HEAD_DIM)
        yc = y - mean
        var = _split_dot(yc * yc, head_ones) * (1.0 / A_HEAD_DIM)
        yn = yc * lax.rsqrt(var + GN_EPS) * gnw_ref[:, sl] + gnb_ref[:, sl]
        yn = yn + bonus_ref[:, sl]
        g = z_ref[:, A_SHIFT_WIDTH + q * QUAD_LANES:A_SHIFT_WIDTH + (q + 1) * QUAD_LANES]
        ya_ref[:, sl] = (yn * (g * _sigmoid(g))).astype(BF16)


def _rwkv_short(za, row0, n_seq, shift0, s0, mu, w0, w2, a0, a2, k_k, k_a, r_k, gn_w, gn_b):
    per_step = RWKV_SEQS_PER_STEP
    tr = per_step * SUBLANES
    vec = lambda n: pl.BlockSpec((1, n), lambda b: (0, 0))
    state_spec = pl.BlockSpec((per_step, A_HEADS, A_HEAD_DIM, A_HEAD_DIM), lambda b: (b, 0, 0, 0))
    return pl.pallas_call(
        functools.partial(_rwkv_kernel, n_seq=per_step),
        grid=(n_seq // per_step,),
        in_specs=[
            pl.BlockSpec((tr, ZA_WIDTH), lambda b: (row0 // tr + b, 0)),
            pl.BlockSpec((per_step, 1, A_SHIFT_WIDTH), lambda b: (b, 0, 0)),
            state_spec,
            vec(A_SHIFT_WIDTH), vec(A_WIDTH),
            pl.BlockSpec((LORA_RANK, A_WIDTH), lambda b: (0, 0)),
            vec(A_WIDTH),
            pl.BlockSpec((LORA_RANK, A_WIDTH), lambda b: (0, 0)),
            vec(A_WIDTH), vec(A_WIDTH), vec(A_WIDTH), vec(A_WIDTH), vec(A_WIDTH),
        ],
        out_specs=[
            pl.BlockSpec((tr, A_WIDTH), lambda b: (b, 0)),
            state_spec,
        ],
        out_shape=[
            jax.ShapeDtypeStruct((n_seq * SUBLANES, A_WIDTH), BF16),
            jax.ShapeDtypeStruct((n_seq, A_HEADS, A_HEAD_DIM, A_HEAD_DIM), F32),
        ],
        scratch_shapes=[pltpu.VMEM((tr, A_WIDTH), F32) for _ in range(8)],
        compiler_params=_cparams(("arbitrary",)),
    )(za, shift0, s0, mu, w0, w2, a0, a2, k_k, k_a, r_k, gn_w, gn_b)


def _head_blocks(y, same_head):
    return jnp.where(same_head, jnp.concatenate([y, y, y, y], axis=0), jnp.zeros((), y.dtype))


def _off_block(t, s, b):
    return jnp.logical_and(t // (2 * b) == s // (2 * b), jnp.logical_and(t % (2 * b) >= b, s % (2 * b) < b))


def _split2(x):
    hi = x.astype(BF16)
    return hi, (x - hi.astype(F32)).astype(BF16)


def _head_matmul3(x, ys, same_head):
    rows = x.shape[0]
    xh, xl = _split2(x)
    x_both = jnp.concatenate([xh, xl], axis=0)
    outs = []
    for y in ys:
        yh, yl = _split2(y)
        both = jnp.dot(x_both, _head_blocks(yh, same_head), preferred_element_type=F32)
        low = jnp.dot(xh, _head_blocks(yl, same_head), preferred_element_type=F32)
        outs.append(both[:rows] + both[rows:] + low)
    return outs


def _head_matmul(x, y, same_head):
    return jnp.dot(x.astype(BF16), _head_blocks(y.astype(BF16), same_head), preferred_element_type=F32)


def _rwkv_chunk_kernel(z_ref, shift0_ref, s0_ref, mu_ref, w0_ref, w2_ref, a0_ref, a2_ref, kk_ref, ka_ref,
                       rk_ref, gnw_ref, gnb_ref,
                       ya_ref, sT_ref, carry_ref, st_ref):
    c = pl.program_id(1)
    n_c = pl.num_programs(1)
    tc = CHUNK
    rows_all = tc * CHUNKS_PER_STEP

    @pl.when(c == 0)
    def _():
        carry_ref[...] = shift0_ref[0]
        st_ref[...] = s0_ref[0]

    lane = lax.broadcasted_iota(jnp.int32, (QUAD_LANES, QUAD_LANES), 1)
    row = lax.broadcasted_iota(jnp.int32, (QUAD_LANES, QUAD_LANES), 0)
    same_head = lane // A_HEAD_DIM == row // A_HEAD_DIM
    head_ones = same_head.astype(BF16)
    t_idx = lax.broadcasted_iota(jnp.int32, (tc, QUAD_LANES), 0)
    lane_t = lax.broadcasted_iota(jnp.int32, (tc, QUAD_LANES), 1)
    s_idx = lane_t % A_HEAD_DIM
    strict = s_idx < t_idx
    incl = s_idx <= t_idx
    eye = (s_idx == t_idx).astype(F32)
    lane_head = lane_t // A_HEAD_DIM
    tri = (lax.broadcasted_iota(jnp.int32, (tc, tc), 1) <= lax.broadcasted_iota(jnp.int32, (tc, tc), 0)
           ).astype(BF16)

    z = z_ref[:, :A_SHIFT_WIDTH]
    prev = pltpu.roll(z, 1, axis=0)
    first = lax.broadcasted_iota(jnp.int32, (rows_all, 1), 0) == 0
    prev = jnp.where(first, carry_ref[...], prev)
    carry_ref[...] = z[rows_all - 1:rows_all, :]
    zs = z + (prev - z) * mu_ref[...]

    lora = zs[:, 3 * A_WIDTH:]
    w_lo = jnp.tanh(lora[:, :LORA_RANK]).astype(BF16)
    a_lo = lora[:, LORA_RANK:].astype(BF16)
    lw = w0_ref[...] + jnp.dot(w_lo, w2_ref[...].astype(BF16), preferred_element_type=F32)
    nlw = -lw
    log_w = -(jnp.maximum(nlw, 0.0) + jnp.log1p(jnp.exp(-jnp.abs(nlw)))) - 0.5
    log_decay = -jnp.exp(log_w)
    a = _sigmoid(a0_ref[...] + jnp.dot(a_lo, a2_ref[...].astype(BF16), preferred_element_type=F32))

    quads = range(A_QUADS)
    lanes = [slice(q * QUAD_LANES, (q + 1) * QUAD_LANES) for q in quads]

    jobs = [(sub, q) for sub in range(CHUNKS_PER_STEP) for q in quads]
    n_jobs = range(len(jobs))
    job_rows = lambda stacked, j: stacked[j * tc:(j + 1) * tc]

    def raw(job):
        sub, q = job
        sl = lanes[q]
        rs = slice(sub * tc, (sub + 1) * tc)
        r_q = zs[rs, q * QUAD_LANES:(q + 1) * QUAD_LANES]
        k_q = zs[rs, A_WIDTH + q * QUAD_LANES:A_WIDTH + (q + 1) * QUAD_LANES]
        v_q = zs[rs, 2 * A_WIDTH + q * QUAD_LANES:2 * A_WIDTH + (q + 1) * QUAD_LANES]
        a_q = a[rs, sl]
        kk = k_q * kk_ref[:, sl]
        k_mod = k_q * (1.0 + (a_q - 1.0) * ka_ref[:, sl])
        return dict(r=r_q, v=v_q, a=a_q, kk=kk, k_mod=k_mod, rkk=r_q * k_mod * rk_ref[:, sl])

    rw = [raw(job) for job in jobs]
    sums = _split_dot(jnp.concatenate([t["kk"] * t["kk"] for t in rw] + [t["rkk"] for t in rw], axis=0), head_ones)
    cums = []
    for sub in range(CHUNKS_PER_STEP):
        ld = log_decay[sub * tc:(sub + 1) * tc, :]
        h1, h2 = _split2(ld)
        h3 = (ld - h1.astype(F32) - h2.astype(F32)).astype(BF16)
        cums.append(jnp.dot(tri, h1, preferred_element_type=F32) + jnp.dot(tri, h2, preferred_element_type=F32)
                    + jnp.dot(tri, h3, preferred_element_type=F32))
    cum_of = lambda job: cums[job[0]][:, lanes[job[1]]]
    p_cols = _split_dot(jnp.concatenate(
        [jnp.where(eye > 0, jnp.exp(cum_of(job)[tc - 1:tc, :]), 0.0) for job in jobs], axis=0), head_ones)

    def tokens(j):
        job, t = jobs[j], rw[j]
        sub, q = job
        kk = t["kk"] / jnp.maximum(jnp.sqrt(job_rows(sums, j)), 1e-12)
        k_mod = t["k_mod"]
        beta = kk * t["a"]
        bonus = job_rows(sums, len(jobs) + j) * t["v"]
        cum = cum_of(job)
        ld = log_decay[sub * tc:(sub + 1) * tc, lanes[q]]
        cum_last = cum[tc - 1:tc, :]
        alpha_p = -kk * jnp.exp(cum - ld)
        r_p = t["r"] * jnp.exp(cum)
        inv_p = jnp.exp(-cum)
        to_end = jnp.exp(cum_last - cum)
        lhs = jnp.concatenate([alpha_p, r_p], axis=0).astype(BF16)
        g_b = _dot_nt(lhs, _head_blocks((beta * inv_p).astype(BF16), same_head))
        g_k = _dot_nt(lhs, _head_blocks((k_mod * inv_p).astype(BF16), same_head))
        lhs_t = jnp.concatenate([beta * to_end, k_mod * to_end], axis=0).astype(BF16)
        return dict(v=t["v"], bonus=bonus, ar_p=lhs,
                    a_ab=jnp.where(strict, g_b[:tc], 0.0), a_rb=jnp.where(incl, g_b[tc:], 0.0),
                    a_ak=jnp.where(strict, g_k[:tc], 0.0), a_rk=jnp.where(incl, g_k[tc:], 0.0),
                    lhs_t=lhs_t, p_col=job_rows(p_cols, j))

    tk = [tokens(j) for j in n_jobs]

    inv = [eye + jnp.where(_off_block(t_idx, s_idx, 1), t["a_ab"], 0.0) for t in tk]
    b = 2
    while b < tc:
        off = _off_block(t_idx, s_idx, b)
        mid = [_head_matmul3(jnp.where(off, tk[j]["a_ab"], 0.0), [inv[j]], same_head)[0] for j in n_jobs]
        inv = [inv[j] + _head_matmul3(inv[j], [mid[j]], same_head)[0] for j in n_jobs]
        b *= 2
    av = [_head_matmul(jnp.concatenate([t["a_ak"], t["a_rk"]], axis=0), t["v"], same_head) for t in tk]
    w1 = [m[:tc] for m in av]
    y_v = [m[tc:] for m in av]

    st = [st_ref[q] for q in quads]
    for sub in range(CHUNKS_PER_STEP):
        js = [sub * A_QUADS + q for q in quads]
        st_blocks = [_head_blocks(s.astype(BF16), same_head) for s in st]
        from_st = [jnp.dot(tk[j]["ar_p"], st_blocks[q], preferred_element_type=F32)
                   for q, j in zip(quads, js)]
        rhs = [w1[j] + from_st[q][:tc] for q, j in zip(quads, js)]
        u = [_head_matmul3(inv[j], [rhs[q]], same_head)[0] for q, j in zip(quads, js)]
        new_st, ys = [], []
        for q, j in zip(quads, js):
            t = tk[j]
            ys.append(from_st[q][tc:] + _head_matmul(t["a_rb"], u[q], same_head) + y_v[j])
            rhs_t = jnp.concatenate([u[q], t["v"]], axis=0).astype(BF16)
            cross = lax.dot_general(t["lhs_t"], rhs_t, (((0,), (0,)), ((), ())), preferred_element_type=F32)
            new = t["p_col"] * st[q]
            for h in range(QUAD_LANES // A_HEAD_DIM):
                new += jnp.where(lane_head == h, cross[h * A_HEAD_DIM:(h + 1) * A_HEAD_DIM], 0.0)
            new_st.append(new)
        st = new_st

        y_all = jnp.concatenate(ys, axis=0)
        yc_all = y_all - _split_dot(y_all, head_ones) * (1.0 / A_HEAD_DIM)
        var_all = _split_dot(yc_all * yc_all, head_ones) * (1.0 / A_HEAD_DIM)
        rs = slice(sub * tc, (sub + 1) * tc)
        for q, j in zip(quads, js):
            sl = lanes[q]
            yn = job_rows(yc_all, q) * lax.rsqrt(job_rows(var_all, q) + GN_EPS) * gnw_ref[:, sl] + gnb_ref[:, sl]
            yn = yn + tk[j]["bonus"]
            g = z_ref[rs, A_SHIFT_WIDTH + q * QUAD_LANES:A_SHIFT_WIDTH + (q + 1) * QUAD_LANES]
            ya_ref[rs, sl] = (yn * (g * _sigmoid(g))).astype(BF16)
    for q in quads:
        st_ref[q] = st[q]

    @pl.when(c == n_c - 1)
    def _():
        sT_ref[0] = st_ref[...]


def _rwkv_chunked(za, t_len, shift0, s0, mu, w0, w2, a0, a2, k_k, k_a, r_k, gn_w, gn_b):
    rows = CHUNK * CHUNKS_PER_STEP
    n_c = t_len // rows
    vec = lambda n: pl.BlockSpec((1, n), lambda b, c: (0, 0))
    state_spec = pl.BlockSpec((1, A_QUADS, A_HEAD_DIM, QUAD_LANES), lambda b, c: (b, 0, 0, 0))
    return pl.pallas_call(
        _rwkv_chunk_kernel,
        grid=(1, n_c),
        in_specs=[
            pl.BlockSpec((rows, ZA_WIDTH), lambda b, c: (c, 0)),
            pl.BlockSpec((1, 1, A_SHIFT_WIDTH), lambda b, c: (b, 0, 0)),
            state_spec,
            vec(A_SHIFT_WIDTH), vec(A_WIDTH),
            pl.BlockSpec((LORA_RANK, A_WIDTH), lambda b, c: (0, 0)),
            vec(A_WIDTH),
            pl.BlockSpec((LORA_RANK, A_WIDTH), lambda b, c: (0, 0)),
            vec(A_WIDTH), vec(A_WIDTH), vec(A_WIDTH), vec(A_WIDTH), vec(A_WIDTH),
        ],
        out_specs=[
            pl.BlockSpec((rows, A_WIDTH), lambda b, c: (c, 0)),
            state_spec,
        ],
        out_shape=[
            jax.ShapeDtypeStruct((t_len, A_WIDTH), BF16),
            jax.ShapeDtypeStruct((1, A_QUADS, A_HEAD_DIM, QUAD_LANES), F32),
        ],
        scratch_shapes=[pltpu.VMEM((1, A_SHIFT_WIDTH), F32),
                        pltpu.VMEM((A_QUADS, A_HEAD_DIM, QUAD_LANES), F32)],
        compiler_params=_cparams(("arbitrary", "arbitrary")),
    )(za, shift0, s0, mu, w0, w2, a0, a2, k_k, k_a, r_k, gn_w, gn_b)


def _keymajor_to_state(s):
    b = s.shape[0]
    return s.reshape(b, A_QUADS, A_HEAD_DIM, 4, A_HEAD_DIM).transpose(0, 1, 3, 4, 2).reshape(
        b, A_HEADS, A_HEAD_DIM, A_HEAD_DIM)


def _norm_rope(x, gain, cos, sin):
    y = x * lax.rsqrt(jnp.mean(x * x, axis=-1, keepdims=True) + NORM_EPS) * gain
    return y * cos + pltpu.roll(y, B_HEAD_DIM // 2, axis=1) * sin


def _dot_nt(a, b):
    return lax.dot_general(a, b, (((1,), (1,)), ((), ())), preferred_element_type=F32)


def _attn_prompt_kernel(q_ref, k_ref, v_ref, cos_ref, sin_ref, qg_ref, kg_ref,
                        o_ref, lse_ref, kr_ref, kprev_ref, vprev_ref, qs_ref, *, dil, n_sub):
    i = pl.program_id(0)
    hd = pl.program_id(1)
    blk = N_BACK

    @pl.when(i == 0)
    def _():
        kprev_ref[hd] = jnp.zeros((dil, blk, B_HEAD_DIM), BF16)
        vprev_ref[hd] = jnp.zeros((dil, blk, B_HEAD_DIM), BF16)

    cos = cos_ref[...]
    sin = sin_ref[...]
    qs_ref[...] = _norm_rope(q_ref[...], qg_ref[...], cos, sin)
    kr_ref[...] = _norm_rope(k_ref[...], kg_ref[...], cos, sin)

    rowi = lax.broadcasted_iota(jnp.int32, (blk, blk), 0)
    coli = lax.broadcasted_iota(jnp.int32, (blk, blk), 1)
    cur_ok = coli <= rowi
    scale = B_HEAD_DIM ** -0.5
    span = lambda sub, rho: pl.ds(sub * blk * dil + rho, blk, stride=dil)
    units = [(sub, rho) for sub in range(n_sub) for rho in range(dil)]
    for u0 in range(0, len(units), ATT_RESIDUE_GROUP):
        group = units[u0:u0 + ATT_RESIDUE_GROUP]
        rows = [span(sub, rho) for sub, rho in group]
        qn = [qs_ref[r, :].astype(BF16) for r in rows]
        kb = [kr_ref[r, :].astype(BF16) for r in rows]
        vb = [v_ref[r, :].astype(BF16) for r in rows]
        kp = [kr_ref[span(sub - 1, rho), :].astype(BF16) if sub > 0 else kprev_ref[hd, rho] for sub, rho in group]
        vp = [v_ref[span(sub - 1, rho), :].astype(BF16) if sub > 0 else vprev_ref[hd, rho] for sub, rho in group]
        prev_ok = [coli >= rowi if sub > 0 else jnp.logical_and(coli >= rowi, i > 0) for sub, _ in group]
        s_cur = [jnp.where(cur_ok, _dot_nt(q, k) * scale, -jnp.inf) for q, k in zip(qn, kb)]
        s_prev = [jnp.where(ok, _dot_nt(q, k) * scale, -jnp.inf) for ok, q, k in zip(prev_ok, qn, kp)]
        m = [jnp.max(jnp.maximum(c, p), axis=-1, keepdims=True) for c, p in zip(s_cur, s_prev)]
        p_cur = [jnp.exp(c - mm) for c, mm in zip(s_cur, m)]
        p_prev = [jnp.exp(p - mm) for p, mm in zip(s_prev, m)]
        den = [jnp.sum(c + p, axis=-1, keepdims=True) for c, p in zip(p_cur, p_prev)]
        for n, r in enumerate(rows):
            o = jnp.dot((p_cur[n] / den[n]).astype(BF16), vb[n], preferred_element_type=F32)
            o += jnp.dot((p_prev[n] / den[n]).astype(BF16), vp[n], preferred_element_type=F32)
            o_ref[r, :] = o
            lse_ref[r, :] = jnp.broadcast_to(m[n] + jnp.log(den[n]), (blk, B_HEAD_DIM))
    for rho in range(dil):
        kprev_ref[hd, rho] = kr_ref[span(n_sub - 1, rho), :].astype(BF16)
        vprev_ref[hd, rho] = v_ref[span(n_sub - 1, rho), :].astype(BF16)


def _attn_prompt(zb, gi, dil, seq, cos, sin, q_gain, k_gain):
    n_sub = max(1, ATT_PROMPT_ROWS // (N_BACK * dil))
    tr = N_BACK * dil * n_sub
    heads_qkv = B_QKV_WIDTH // B_HEAD_DIM
    blk = lambda which: pl.BlockSpec(
        (tr, B_HEAD_DIM), lambda i, hd: (i, which * heads_qkv + gi * HEADS_PER_GROUP + hd))
    tab = pl.BlockSpec((tr, B_HEAD_DIM), lambda i, hd: (i, 0))
    gain = pl.BlockSpec((1, B_HEAD_DIM), lambda i, hd: (0, 0))
    out = pl.BlockSpec((tr, B_HEAD_DIM), lambda i, hd: (i, hd))
    shp = jax.ShapeDtypeStruct((seq, B_GROUP_WIDTH), F32)
    prev = pltpu.VMEM((HEADS_PER_GROUP, dil, N_BACK, B_HEAD_DIM), BF16)
    return pl.pallas_call(
        functools.partial(_attn_prompt_kernel, dil=dil, n_sub=n_sub),
        grid=(seq // tr, HEADS_PER_GROUP),
        in_specs=[blk(0), blk(1), blk(2), tab, tab, gain, gain],
        out_specs=[out, out, out],
        out_shape=[shp, shp, shp],
        scratch_shapes=[prev, prev, pltpu.VMEM((tr, B_HEAD_DIM), F32)],
        compiler_params=_cparams(("arbitrary", "arbitrary")),
    )(zb, zb, zb, cos, sin, q_gain, k_gain)


def _attn_sample_kernel(q_ref, k_ref, v_ref, cache_ref, cos_ref, sin_ref, qg_ref, kg_ref,
                        o_ref, lse_ref, new_ref, *, window, dil, t_new, n_seq):
    per_pos = 2 * HEADS_PER_GROUP
    cos = cos_ref[...]
    sin = sin_ref[...]
    scale = B_HEAD_DIM ** -0.5
    tq = lax.broadcasted_iota(jnp.int32, (t_new, window), 0)
    cc = lax.broadcasted_iota(jnp.int32, (t_new, window), 1)
    cache_ok = jnp.logical_and(((cc - tq + dil * N_BACK) & (dil - 1)) == 0, cc >= tq)
    tn = lax.broadcasted_iota(jnp.int32, (t_new, t_new), 0)
    un = lax.broadcasted_iota(jnp.int32, (t_new, t_new), 1)
    new_ok = jnp.logical_and(((tn - un + dil * N_BACK) & (dil - 1)) == 0, un <= tn)
    for g in range(n_seq):
        new_ref[g, 0:(window - t_new) * per_pos, :] = cache_ref[g, t_new * per_pos:window * per_pos, :]
    units = [(g, h) for g in range(n_seq) for h in range(HEADS_PER_GROUP)]
    for u0 in range(0, len(units), HEADS_PER_GROUP):
        group = units[u0:u0 + HEADS_PER_GROUP]
        rows = [slice(g * t_new, (g + 1) * t_new) for g, _ in group]
        cols = [slice(h * B_HEAD_DIM, (h + 1) * B_HEAD_DIM) for _, h in group]
        qn = [_norm_rope(q_ref[r, c], qg_ref[...], cos, sin).astype(BF16) for r, c in zip(rows, cols)]
        kr = [_norm_rope(k_ref[r, c], kg_ref[...], cos, sin) for r, c in zip(rows, cols)]
        v_new = [v_ref[r, c] for r, c in zip(rows, cols)]
        kc = [cache_ref[g, pl.ds(h, window, stride=per_pos), :].astype(BF16) for g, h in group]
        vc = [cache_ref[g, pl.ds(HEADS_PER_GROUP + h, window, stride=per_pos), :].astype(BF16) for g, h in group]
        s_c = [jnp.where(cache_ok, _dot_nt(q, k) * scale, -jnp.inf) for q, k in zip(qn, kc)]
        s_n = [jnp.where(new_ok, _dot_nt(q, k.astype(BF16)) * scale, -jnp.inf) for q, k in zip(qn, kr)]
        m = [jnp.maximum(jnp.max(c, axis=-1, keepdims=True), jnp.max(n, axis=-1, keepdims=True))
             for c, n in zip(s_c, s_n)]
        p_c = [jnp.exp(c - mm) for c, mm in zip(s_c, m)]
        p_n = [jnp.exp(n - mm) for n, mm in zip(s_n, m)]
        den = [jnp.sum(c, axis=-1, keepdims=True) + jnp.sum(n, axis=-1, keepdims=True) for c, n in zip(p_c, p_n)]
        for i, (g, h) in enumerate(group):
            o = jnp.dot((p_c[i] / den[i]).astype(BF16), vc[i], preferred_element_type=F32)
            o += jnp.dot((p_n[i] / den[i]).astype(BF16), v_new[i].astype(BF16), preferred_element_type=F32)
            o_ref[rows[i], cols[i]] = o
            lse_ref[rows[i], cols[i]] = jnp.broadcast_to(m[i] + jnp.log(den[i]), (t_new, B_HEAD_DIM))
            base = (window - t_new) * per_pos
            new_ref[g, pl.ds(base + h, t_new, stride=per_pos), :] = kr[i]
            new_ref[g, pl.ds(base + HEADS_PER_GROUP + h, t_new, stride=per_pos), :] = v_new[i]


def _attn_sample(zb, gi, window, dil, row0, n_seq, t_new, cache, cos, sin, q_gain, k_gain):
    rows = window * 2 * HEADS_PER_GROUP
    per_step = max(1, min(n_seq, ATT_SAMPLE_WINDOW_ROWS // window))
    tr = per_step * t_new
    cache2 = cache.reshape(n_seq, rows, B_HEAD_DIM)
    blk = lambda off: pl.BlockSpec((tr, B_GROUP_WIDTH), lambda b: (row0 // tr + b, off + gi))
    tab = pl.BlockSpec((t_new, B_HEAD_DIM), lambda b: (0, 0))
    gain = pl.BlockSpec((1, B_HEAD_DIM), lambda b: (0, 0))
    out = pl.BlockSpec((tr, B_GROUP_WIDTH), lambda b: (b, 0))
    win = pl.BlockSpec((per_step, rows, B_HEAD_DIM), lambda b: (b, 0, 0))
    shp = jax.ShapeDtypeStruct((n_seq * t_new, B_GROUP_WIDTH), F32)
    o, lse, new = pl.pallas_call(
        functools.partial(_attn_sample_kernel, window=window, dil=dil, t_new=t_new, n_seq=per_step),
        grid=(n_seq // per_step,),
        in_specs=[blk(0), blk(3), blk(6), win, tab, tab, gain, gain],
        out_specs=[out, out, win],
        out_shape=[shp, shp, jax.ShapeDtypeStruct((n_seq, rows, B_HEAD_DIM), F32)],
        compiler_params=_cparams(("arbitrary",)),
    )(zb, zb, zb, cache2, cos, sin, q_gain, k_gain)
    return o, lse, new.reshape(cache.shape)


def _combine_kernel(o0, o1, o2, l0, l1, l2, g_ref, y_ref):
    a, b, c = l0[...], l1[...], l2[...]
    m = jnp.maximum(jnp.maximum(a, b), c)
    ea, eb, ec = jnp.exp(a - m), jnp.exp(b - m), jnp.exp(c - m)
    o = (ea * o0[...] + eb * o1[...] + ec * o2[...]) / (ea + eb + ec)
    g = g_ref[...]
    y_ref[...] = (o * (g * _sigmoid(g))).astype(BF16)


def _combine(outs, lses, zc, row_block0, tm):
    rows = outs[0].shape[0]
    spec = pl.BlockSpec((tm, B_GROUP_WIDTH), lambda i: (i, 0))
    return pl.pallas_call(
        _combine_kernel,
        grid=(rows // tm,),
        in_specs=[spec] * 6 + [pl.BlockSpec((tm, B_GROUP_WIDTH), lambda i: (row_block0 + i, 0))],
        out_specs=spec,
        out_shape=jax.ShapeDtypeStruct((rows, B_GROUP_WIDTH), BF16),
        compiler_params=_cparams(("parallel",)),
    )(*outs, *lses, zc)


def _merge_kernel(ya_ref, yb_ref, ma_ref, mb_ref, wa_ref, wb_ref, o_ref):
    ua = jnp.dot(ya_ref[...], wa_ref[...], preferred_element_type=F32)
    ub = jnp.dot(yb_ref[...], wb_ref[...], preferred_element_type=F32)
    o_ref[...] = (_sigmoid(ma_ref[...]) * ua + _sigmoid(mb_ref[...]) * ub).astype(BF16)


def _merge(ya, yb, zc, row_block0, wa, wb, tm, tn):
    m = ya.shape[0]
    nb = D_MODEL // tn
    off_a = B_GROUP_WIDTH // tn
    off_b = (B_GROUP_WIDTH + D_MODEL) // tn
    assert off_a * tn == B_GROUP_WIDTH
    return pl.pallas_call(
        _merge_kernel,
        grid=(m // tm, nb),
        in_specs=[
            pl.BlockSpec((tm, A_WIDTH), lambda i, j: (i, 0)),
            pl.BlockSpec((tm, B_GROUP_WIDTH), lambda i, j: (i, 0)),
            pl.BlockSpec((tm, tn), lambda i, j: (row_block0 + i, off_a + j)),
            pl.BlockSpec((tm, tn), lambda i, j: (row_block0 + i, off_b + j)),
            pl.BlockSpec((A_WIDTH, tn), lambda i, j: (0, j)),
            pl.BlockSpec((B_GROUP_WIDTH, tn), lambda i, j: (0, j)),
        ],
        out_specs=pl.BlockSpec((tm, tn), lambda i, j: (i, j)),
        out_shape=jax.ShapeDtypeStruct((m, D_MODEL), BF16),
        compiler_params=_cparams(("parallel", "arbitrary")),
    )(ya, yb, zc, zc, wa, wb)


def _outproj_kernel(x_ref, a_ref, w_ref, o_ref):
    o_ref[...] = x_ref[...] + jnp.dot(a_ref[...], w_ref[...], preferred_element_type=F32)


def _outproj(x, merged, w, tm, tn):
    m = x.shape[0]
    return pl.pallas_call(
        _outproj_kernel,
        grid=(D_MODEL // tn, m // tm),
        in_specs=[
            pl.BlockSpec((tm, tn), lambda j, i: (i, j)),
            pl.BlockSpec((tm, D_MODEL), lambda j, i: (i, 0)),
            pl.BlockSpec((D_MODEL, tn), lambda j, i: (0, j)),
        ],
        out_specs=pl.BlockSpec((tm, tn), lambda j, i: (i, j)),
        out_shape=jax.ShapeDtypeStruct((m, D_MODEL), F32),
        compiler_params=_cparams(("parallel", "arbitrary")),
    )(x, merged, w)


def _ple_kernel(h_ref, p_ref, wg_ref, wp_ref, o_ref):
    h = h_ref[...]
    gate = _sigmoid(jnp.dot(h.astype(BF16), wg_ref[...], preferred_element_type=F32))
    proj = jnp.dot(p_ref[...].astype(BF16), wp_ref[...], preferred_element_type=F32)
    o_ref[...] = h + gate * proj


def _ple(h, p, wg, wp, tm):
    m = h.shape[0]
    pd = p.shape[1]
    return pl.pallas_call(
        _ple_kernel,
        grid=(m // tm,),
        in_specs=[
            pl.BlockSpec((tm, D_MODEL), lambda i: (i, 0)),
            pl.BlockSpec((tm, pd), lambda i: (i, 0)),
            pl.BlockSpec((D_MODEL, D_MODEL), lambda i: (0, 0)),
            pl.BlockSpec((pd, D_MODEL), lambda i: (0, 0)),
        ],
        out_specs=pl.BlockSpec((tm, D_MODEL), lambda i: (i, 0)),
        out_shape=jax.ShapeDtypeStruct((m, D_MODEL), F32),
        compiler_params=_cparams(("parallel",)),
    )(h, p, wg, wp)


def _rope_tables(pos):
    half = B_HEAD_DIM // 2
    inv = ROPE_THETA ** (-jnp.arange(half, dtype=F32) / half)
    ang = pos.astype(F32)[:, None] * inv[None, :]
    cos, sin = jnp.cos(ang), jnp.sin(ang)
    return jnp.concatenate([cos, cos], axis=1), jnp.concatenate([-sin, sin], axis=1)


def kernel(x_prompt, x_sample, state_shift, state_wkv, cache_kv_w128, cache_kv_w512, cache_kv_w2048,
           p_prompt, p_sample, ln_g, w_in, mu, w0, w2, a0, a2, k_k, k_a, r_k, gn_w, gn_b,
           q_gain, k_gain, w_up_a, w_up_b, w_out, w_ple_gate, w_ple_proj):
    depth = ln_g.shape[0]
    assert depth == 1
    bp, seq, d = x_prompt.shape
    bs, t_new, _ = x_sample.shape
    assert bp == 1 and d == D_MODEL
    rs = bs * t_new
    caches = (cache_kv_w128, cache_kv_w512, cache_kv_w2048)

    xp = x_prompt.reshape(seq, d)
    xs = x_sample.reshape(rs, d)
    hn = _rmsnorm(xp, xs, ln_g)

    w_in0 = w_in[0]
    off_b = ZA_WIDTH
    off_c = ZA_WIDTH + ZB_WIDTH
    wa_in = w_in0[:, :off_b].astype(BF16)
    wb_in = w_in0[:, off_b:off_c].astype(BF16)
    wc_in = w_in0[:, off_c:].astype(BF16)
    za = _matmul(hn, wa_in, IN_PROJ_ROWS, ZA_WIDTH // IN_PROJ_COL_TILES)
    zb = _matmul(hn, wb_in, IN_PROJ_ROWS, ZB_WIDTH // IN_PROJ_COL_TILES)
    zc = _matmul(hn, wc_in, IN_PROJ_ROWS, ZC_WIDTH // IN_PROJ_COL_TILES)

    flat = lambda v: v.reshape(1, -1)
    rw = (mu, flat(w0[0]), w2[0], flat(a0[0]), a2[0], flat(k_k[0]), flat(k_a[0]), flat(r_k[0]),
          flat(gn_w[0]), flat(gn_b[0]))
    ya_p, s_p = _rwkv_chunked(za, seq, jnp.zeros((1, 1, A_SHIFT_WIDTH), F32),
                              jnp.zeros((1, A_QUADS, A_HEAD_DIM, QUAD_LANES), F32), *rw)
    assert t_new == SUBLANES
    ya_s, s_s = _rwkv_short(za, seq, bs, state_shift[0][:, None, :], state_wkv[0], *rw)
    shift_p = za[seq - 1, :A_SHIFT_WIDTH].reshape(1, 1, A_SHIFT_WIDTH)
    shift_s = za[seq:, :A_SHIFT_WIDTH].reshape(bs, t_new, A_SHIFT_WIDTH)[:, -1][None]
    wkv_p = _keymajor_to_state(s_p)[None]
    wkv_s = s_s[None]

    cos_p, sin_p = _rope_tables(jnp.arange(seq, dtype=jnp.int32))
    cos_s, sin_s = _rope_tables(PAST_LEN + jnp.arange(t_new, dtype=jnp.int32))
    o_p, l_p, o_s, l_s, kv_p, kv_s = [], [], [], [], [], []
    for gi, (window, dil) in enumerate(ATT_GROUPS):
        o, lse, kr = _attn_prompt(zb, gi, dil, seq, cos_p, sin_p, q_gain, k_gain)
        o_p.append(o)
        l_p.append(lse)
        w_keep = min(window, seq)
        v_g = zb[seq - w_keep:seq, 2 * B_QKV_WIDTH + gi * B_GROUP_WIDTH:2 * B_QKV_WIDTH + (gi + 1) * B_GROUP_WIDTH]
        kv_p.append(jnp.stack([kr[seq - w_keep:].reshape(w_keep, HEADS_PER_GROUP, B_HEAD_DIM),
                               v_g.reshape(w_keep, HEADS_PER_GROUP, B_HEAD_DIM)], axis=1)[None, None])
        cache = caches[gi][0]
        o, lse, new_cache = _attn_sample(zb, gi, cache.shape[1], dil, seq, bs, t_new, cache,
                                         cos_s, sin_s, q_gain, k_gain)
        o_s.append(o)
        l_s.append(lse)
        kv_s.append(new_cache[None])
    yb_p = _combine(o_p, l_p, zc, 0, COMBINE_ROWS)
    yb_s = _combine(o_s, l_s, zc, seq // rs, rs)

    wa_up = w_up_a[0].astype(BF16)
    wb_up = w_up_b[0].astype(BF16)
    merged_p = _merge(ya_p, yb_p, zc, 0, wa_up, wb_up, MERGE_ROWS, B_GROUP_WIDTH)
    merged_s = _merge(ya_s, yb_s, zc, seq // rs, wa_up, wb_up, rs, B_GROUP_WIDTH)
    w_out_b = w_out[0].astype(BF16)
    h_p = _outproj(xp, merged_p, w_out_b, OUT_ROWS, D_MODEL)
    h_s = _outproj(xs, merged_s, w_out_b, rs, D_MODEL)
    wg = w_ple_gate[0].astype(BF16)
    wp = w_ple_proj[0].astype(BF16)
    y_p = _ple(h_p, p_prompt[0, 0], wg, wp, OUT_ROWS)
    y_s = _ple(h_s, p_sample[0].reshape(rs, -1), wg, wp, rs)

    return (y_p.reshape(bp, seq, d), y_s.reshape(bs, t_new, d),
            shift_p, wkv_p, kv_p[0], kv_p[1], kv_p[2],
            shift_s, wkv_s, kv_s[0], kv_s[1], kv_s[2])
```
